```python
import math
import jax, jax.numpy as jnp
from jax import lax
import numpy as np


D_MODEL = 1024
BATCH = 2
SEQ = 8192
DEPTH = 4
DEC_BATCH = 128
DEC_SEQ = 4
PAST_LEN = 2048
PAGE_SIZE = 128

N_EVEN = (DEPTH + 1) // 2
N_ODD = DEPTH // 2
EPS = 1e-6
H_A = 4
DK_A = D_MODEL // 8
DV_A = D_MODEL // 8
CONV_W = 4
CHUNK_A = 64
H_B = 4
DQK_B = D_MODEL // 16
DV_B = 2 * DQK_B
ROPE_DIM = DQK_B // 4
ROPE_THETA = 500000.0
Q_BLOCK = 128
H_C = 4
DQK_C = D_MODEL // 8
DV_C = D_MODEL // 4
CHUNK_C = 64

A_CONV_CH = H_A * (2 * DK_A + DV_A)
W_A = H_A * DV_A
W_B = H_B * DV_B
W_C = H_C * DV_C
EVEN_SPLITS = (A_CONV_CH, W_A, H_A, H_A, H_B * 2 * DQK_B, H_B * 2 * DQK_B, W_B, W_B)
EVEN_IN = sum(EVEN_SPLITS)
ODD_SPLITS = (H_C * DQK_C, H_C * DQK_C, W_C, W_C, W_C, H_C, H_C)
ODD_IN = sum(ODD_SPLITS)

kernel_name = 'hybrid_gdn_diffattn_mlstm_step'


def _split(x, sizes):
    idx = [int(v) for v in np.cumsum(sizes)[:-1]]
    return jnp.split(x, idx, axis=-1)


def _rms_norm(x, w):
    xf = x.astype(jnp.float32)
    y = xf * lax.rsqrt(jnp.mean(xf * xf, axis=-1, keepdims=True) + EPS)
    return (y * w.astype(jnp.float32)).astype(x.dtype)


def _l2norm(x):
    xf = x.astype(jnp.float32)
    return xf * lax.rsqrt(jnp.sum(xf * xf, axis=-1, keepdims=True) + EPS)


def _partial_rope(x, pos):
    half = ROPE_DIM // 2
    inv = jnp.power(ROPE_THETA, -jnp.arange(half, dtype=jnp.float32) / half)
    ang = pos.astype(jnp.float32)[:, None] * inv[None, :]
    shape = (1, pos.shape[0]) + (1,) * (x.ndim - 3) + (half,)
    cos = jnp.cos(ang).reshape(shape)
    sin = jnp.sin(ang).reshape(shape)
    xf = x.astype(jnp.float32)
    x1 = xf[..., :half]
    x2 = xf[..., half:ROPE_DIM]
    rot = jnp.concatenate([x1 * cos - x2 * sin, x2 * cos + x1 * sin, xf[..., ROPE_DIM:]], axis=-1)
    return rot.astype(x.dtype)


def _causal_conv(x, buf, w):
    L = x.shape[1]
    xp = jnp.concatenate([buf.astype(x.dtype), x], axis=1)
    y = xp[:, 0:L] * w[0]
    for j in range(1, CONV_W):
        y = y + xp[:, j:j + L] * w[j]
    return jax.nn.silu(y), xp[:, L:]


def _to_chunks(t, n, c):
    b, _, h, d = t.shape
    return t.astype(jnp.float32).reshape(b, n, c, h, d).transpose(1, 0, 3, 2, 4)


def _gate_chunks(t, n, c):
    b, _, h = t.shape
    return t.astype(jnp.float32).reshape(b, n, c, h).transpose(1, 0, 3, 2)


def _from_chunks(o):
    n, b, h, c, d = o.shape
    return o.transpose(1, 0, 3, 2, 4).reshape(b, n * c, h, d)


def _gated_delta(q, k, v, g, beta, s0):
    L = q.shape[1]
    c = math.gcd(L, CHUNK_A)
    n = L // c
    qc = _to_chunks(q, n, c) * (DK_A ** -0.5)
    kc = _to_chunks(k, n, c)
    vc = _to_chunks(v, n, c)
    gcum = jnp.cumsum(_gate_chunks(g, n, c), axis=-1)
    bc = _gate_chunks(beta, n, c)
    incl = jnp.tril(jnp.ones((c, c), dtype=bool))
    strict = jnp.tril(jnp.ones((c, c), dtype=bool), -1)
    diff = gcum[..., :, None] - gcum[..., None, :]
    decay = jnp.where(incl, jnp.exp(jnp.where(incl, diff, 0.0)), 0.0)
    kk = jnp.einsum('nbhid,nbhjd->nbhij', kc, kc)
    a_low = jnp.where(strict, kk * decay * bc[..., :, None], 0.0)
    lhs = a_low + jnp.eye(c, dtype=jnp.float32)
    rhs = jnp.concatenate([vc * bc[..., None], kc * (bc * jnp.exp(gcum))[..., None]], axis=-1)
    sol = lax.linalg.triangular_solve(lhs, rhs, left_side=True, lower=True, unit_diagonal=True)
    u = sol[..., :DV_A]
    wk = sol[..., DV_A:]
    qk_intra = jnp.einsum('nbhid,nbhjd->nbhij', qc, kc) * decay

    def step(S, xs):
        q_i, k_i, u_i, w_i, g_i, a_i = xs
        v_new = u_i - jnp.einsum('bhcd,bhde->bhce', w_i, S)
        o = jnp.einsum('bhcd,bhde->bhce', q_i * jnp.exp(g_i)[..., None], S) + jnp.einsum('bhij,bhje->bhie', a_i, v_new)
        g_last = g_i[..., -1]
        k_w = k_i * jnp.exp(g_last[..., None] - g_i)[..., None]
        S = S * jnp.exp(g_last)[..., None, None] + jnp.einsum('bhcd,bhce->bhde', k_w, v_new)
        return S, o

    s_fin, o = lax.scan(step, s0.astype(jnp.float32), (qc, kc, u, wk, gcum, qk_intra))
    return _from_chunks(o), s_fin


def _mlstm(q, k, v, log_i, log_f, c0, n0, m0):
    L = q.shape[1]
    c = math.gcd(L, CHUNK_C)
    n = L // c
    qc = _to_chunks(q, n, c) * (DQK_C ** -0.5)
    kc = _to_chunks(k, n, c)
    vc = _to_chunks(v, n, c)
    li = _gate_chunks(log_i, n, c)
    bcum = jnp.cumsum(_gate_chunks(log_f, n, c), axis=-1)
    incl = jnp.tril(jnp.ones((c, c), dtype=bool))
    dmat = jnp.where(incl, bcum[..., :, None] - bcum[..., None, :] + li[..., None, :], -jnp.inf)

    def step(carry, xs):
        cs, ns, ms = carry
        q_i, k_i, v_i, b_i, d_i, l_i = xs
        inter = b_i + ms[..., None]
        m_t = jnp.maximum(inter, jnp.max(d_i, axis=-1))
        w_t = jnp.exp(d_i - m_t[..., None])
        sc = jnp.exp(inter - m_t)
        qk = jnp.einsum('bhid,bhjd->bhij', q_i, k_i) * w_t
        num = sc[..., None] * jnp.einsum('bhcd,bhde->bhce', q_i, cs) + jnp.einsum('bhij,bhje->bhie', qk, v_i)
        den = sc * jnp.einsum('bhcd,bhd->bhc', q_i, ns) + jnp.sum(qk, axis=-1)
        h = num / jnp.maximum(jnp.abs(den), jnp.exp(-m_t))[..., None]
        m_new = m_t[..., -1]
        dec = jnp.exp(b_i[..., -1] + ms - m_new)
        wk = jnp.exp(b_i[..., -1:] - b_i + l_i - m_new[..., None])
        cs = cs * dec[..., None, None] + jnp.einsum('bhcd,bhce->bhde', k_i * wk[..., None], v_i)
        ns = ns * dec[..., None] + jnp.einsum('bhcd,bhc->bhd', k_i, wk)
        return (cs, ns, m_new), h

    carry0 = (c0.astype(jnp.float32), n0.astype(jnp.float32), m0.astype(jnp.float32))
    (c_f, n_f, m_f), h = lax.scan(step, carry0, (qc, kc, vc, bcum, dmat, li))
    return _from_chunks(h), c_f, n_f, m_f


def _diff_attention(q, k, v, lam, start):
    bn, lq = q.shape[0], q.shape[1]
    qb = math.gcd(lq, Q_BLOCK)
    nb = lq // qb
    scale = DQK_B ** -0.5
    kpos = jnp.arange(k.shape[1])
    kf = k.astype(jnp.float32)
    vf = v.astype(jnp.float32)
    q_blocks = q.astype(jnp.float32).reshape(bn, nb, qb, H_B, 2, DQK_B).swapaxes(0, 1)
    starts = start + qb * jnp.arange(nb)

    def block(args):
        qi, s0 = args
        s = jnp.einsum('bqhmd,bkhmd->bhmqk', qi, kf) * scale
        qpos = s0 + jnp.arange(qb)
        s = jnp.where(kpos[None, :] <= qpos[:, None], s, -jnp.inf)
        p = jax.nn.softmax(s, axis=-1)
        a = p[:, :, 0] - lam * p[:, :, 1]
        return jnp.einsum('bhqk,bkhe->bqhe', a, vf)

    o = lax.map(block, (q_blocks, starts))
    return o.swapaxes(0, 1).reshape(bn, lq, H_B, DV_B)


def _even_mixer(h, start, conv_buf, s0, k_past, v_past, w_in, w_out, conv_w, a_log, dt_bias,
                gdn_norm_w, lam_p, diff_norm_w, lambda_init):
    bn, L, _ = h.shape
    proj = h @ w_in
    qkv_a, z_a, b_raw, a_raw, q_b, k_b, v_b, z_b = _split(proj, EVEN_SPLITS)
    qkv_c, new_buf = _causal_conv(qkv_a, conv_buf, conv_w)
    q_a, k_a, v_a = _split(qkv_c, (H_A * DK_A, H_A * DK_A, H_A * DV_A))
    q_a = _l2norm(q_a.reshape(bn, L, H_A, DK_A))
    k_a = _l2norm(k_a.reshape(bn, L, H_A, DK_A))
    v_a = v_a.reshape(bn, L, H_A, DV_A)
    beta = jax.nn.sigmoid(b_raw.astype(jnp.float32))
    g = -jnp.exp(a_log.astype(jnp.float32)) * jax.nn.softplus(a_raw.astype(jnp.float32) + dt_bias.astype(jnp.float32))
    o_a, s_new = _gated_delta(q_a, k_a, v_a, g, beta, s0)
    o_a = _rms_norm(o_a, gdn_norm_w) * jax.nn.silu(z_a.astype(jnp.float32)).reshape(bn, L, H_A, DV_A)
    pos = start + jnp.arange(L)
    q_b = _partial_rope(q_b.reshape(bn, L, H_B, 2, DQK_B), pos)
    k_b = _partial_rope(k_b.reshape(bn, L, H_B, 2, DQK_B), pos)
    v_b = v_b.reshape(bn, L, H_B, DV_B)
    if k_past is None:
        k_all, v_all = k_b, v_b
    else:
        k_all = jnp.concatenate([k_past.astype(k_b.dtype), k_b], axis=1)
        v_all = jnp.concatenate([v_past.astype(v_b.dtype), v_b], axis=1)
    lp = lam_p.astype(jnp.float32)
    lam = jnp.exp(jnp.sum(lp[0] * lp[1])) - jnp.exp(jnp.sum(lp[2] * lp[3])) + lambda_init
    o_b = _diff_attention(q_b, k_all, v_all, lam, start)
    o_b = _rms_norm(o_b, diff_norm_w) * (1.0 - lambda_init) * jax.nn.silu(z_b.astype(jnp.float32)).reshape(bn, L, H_B, DV_B)
    merged = jnp.concatenate([o_a.reshape(bn, L, W_A), o_b.reshape(bn, L, W_B)], axis=-1).astype(h.dtype)
    y = merged @ w_out
    return y, new_buf, s_new, k_b.reshape(bn, L, H_B, 2 * DQK_B), v_b


def _odd_mixer(h, c0, n0, m0, w_in, w_out, b_i, b_f, norm_w):
    bn, L, _ = h.shape
    proj = h @ w_in
    q, k, v, z, o_raw, i_raw, f_raw = _split(proj, ODD_SPLITS)
    q = q.reshape(bn, L, H_C, DQK_C)
    k = k.reshape(bn, L, H_C, DQK_C)
    v = v.reshape(bn, L, H_C, DV_C)
    log_i = i_raw.astype(jnp.float32) + b_i.astype(jnp.float32)
    log_f = jax.nn.log_sigmoid(f_raw.astype(jnp.float32) + b_f.astype(jnp.float32))
    h_t, c_f, n_f, m_f = _mlstm(q, k, v, log_i, log_f, c0, n0, m0)
    h_t = jax.nn.sigmoid(o_raw.astype(jnp.float32)).reshape(bn, L, H_C, DV_C) * h_t
    y = _rms_norm(h_t, norm_w) * jax.nn.silu(z.astype(jnp.float32)).reshape(bn, L, H_C, DV_C)
    return y.reshape(bn, L, W_C).astype(h.dtype) @ w_out, c_f, n_f, m_f


def _trunk(x, start, gdn_conv, gdn_s, ml_c, ml_n, ml_m, cache_k, cache_v, page_table,
           norm_w, final_norm_w, w_in_even, w_out_even, conv_w, a_log, dt_bias, gdn_norm_w,
           lambda_qk, diff_norm_w, w_in_odd, w_out_odd, b_i, b_f, mlstm_norm_w):
    bn = x.shape[0]
    k_rows, v_rows, conv_new, s_new, c_new, n_new, m_new = [], [], [], [], [], [], []
    for layer in range(DEPTH):
        h = _rms_norm(x, norm_w[layer])
        if layer % 2 == 0:
            e = layer // 2
            if page_table is None:
                k_past, v_past = None, None
            else:
                k_past = cache_k[e][page_table].reshape(bn, -1, H_B, 2, DQK_B)
                v_past = cache_v[e][page_table].reshape(bn, -1, H_B, DV_B)
            lambda_init = 0.8 - 0.6 * math.exp(-0.3 * layer)
            y, cb, s, kr, vr = _even_mixer(h, start, gdn_conv[e], gdn_s[e], k_past, v_past,
                                           w_in_even[e], w_out_even[e], conv_w[e], a_log[e], dt_bias[e],
                                           gdn_norm_w[e], lambda_qk[e], diff_norm_w[e], lambda_init)
            k_rows.append(kr)
            v_rows.append(vr)
            conv_new.append(cb)
            s_new.append(s)
        else:
            o = layer // 2
            y, c, n, m = _odd_mixer(h, ml_c[o], ml_n[o], ml_m[o], w_in_odd[o], w_out_odd[o],
                                    b_i[o], b_f[o], mlstm_norm_w[o])
            c_new.append(c)
            n_new.append(n)
            m_new.append(m)
        x = x + y
    states = (jnp.stack(k_rows), jnp.stack(v_rows), jnp.stack(conv_new), jnp.stack(s_new),
              jnp.stack(c_new), jnp.stack(n_new), jnp.stack(m_new))
    return _rms_norm(x, final_norm_w), states


def setup_inputs(seed: int = 0) -> dict:
    key = jax.random.key(seed)
    ks = jax.random.split(key, 26)
    f32 = jnp.float32
    n_pages = PAST_LEN // PAGE_SIZE
    n_pool = (DEC_BATCH * n_pages * 5) // 4

    def nrm(k, shape, s=1.0):
        return jax.random.normal(k, shape, f32) * s

    x_prompt = nrm(ks[0], (BATCH, SEQ, D_MODEL))
    x_sample = nrm(ks[1], (DEC_BATCH, DEC_SEQ, D_MODEL))
    cache_k = nrm(ks[2], (N_EVEN, n_pool, PAGE_SIZE, H_B, 2 * DQK_B))
    cache_v = nrm(ks[3], (N_EVEN, n_pool, PAGE_SIZE, H_B, DV_B))
    page_table = jax.random.permutation(ks[4], n_pool)[: DEC_BATCH * n_pages].reshape(DEC_BATCH, n_pages).astype(jnp.int32)
    state_gdn_conv = nrm(ks[5], (N_EVEN, DEC_BATCH, CONV_W - 1, A_CONV_CH))
    state_gdn_s = nrm(ks[6], (N_EVEN, DEC_BATCH, H_A, DK_A, DV_A), 0.1)
    state_mlstm_c = nrm(ks[7], (N_ODD, DEC_BATCH, H_C, DQK_C, DV_C), 0.1)
    state_mlstm_n = nrm(ks[8], (N_ODD, DEC_BATCH, H_C, DQK_C), 0.1)
    state_mlstm_m = nrm(ks[9], (N_ODD, DEC_BATCH, H_C), 0.5)
    norm_w = 1.0 + nrm(ks[10], (DEPTH, D_MODEL), 0.02)
    final_norm_w = 1.0 + nrm(ks[11], (D_MODEL,), 0.02)
    w_in_even = nrm(ks[12], (N_EVEN, D_MODEL, EVEN_IN), D_MODEL ** -0.5)
    w_out_even = nrm(ks[13], (N_EVEN, W_A + W_B, D_MODEL), (W_A + W_B) ** -0.5)
    conv_w = nrm(ks[14], (N_EVEN, CONV_W, A_CONV_CH), CONV_W ** -0.5)
    a_log = jnp.log(jax.random.uniform(ks[15], (N_EVEN, H_A), f32, 1.0, 16.0))
    dt = jnp.exp(jax.random.uniform(ks[16], (N_EVEN, H_A), f32, math.log(1e-3), math.log(1e-1)))
    dt_bias = dt + jnp.log(-jnp.expm1(-dt))
    gdn_norm_w = 1.0 + nrm(ks[17], (N_EVEN, DV_A), 0.02)
    lambda_qk = nrm(ks[18], (N_EVEN, 4, DQK_B), 0.1)
    diff_norm_w = 1.0 + nrm(ks[19], (N_EVEN, DV_B), 0.02)
    w_in_odd = nrm(ks[20], (N_ODD, D_MODEL, ODD_IN), D_MODEL ** -0.5)
    w_out_odd = nrm(ks[21], (N_ODD, W_C, D_MODEL), W_C ** -0.5)
    b_i = nrm(ks[22], (N_ODD, H_C), 0.1)
    b_f = 3.0 + nrm(ks[23], (N_ODD, H_C), 0.5)
    mlstm_norm_w = 1.0 + nrm(ks[24], (N_ODD, DV_C), 0.02)
    return {'x_prompt': x_prompt, 'x_sample': x_sample, 'cache_k': cache_k, 'cache_v': cache_v,
            'page_table': page_table, 'state_gdn_conv': state_gdn_conv, 'state_gdn_s': state_gdn_s,
            'state_mlstm_c': state_mlstm_c, 'state_mlstm_n': state_mlstm_n, 'state_mlstm_m': state_mlstm_m,
            'norm_w': norm_w, 'final_norm_w': final_norm_w, 'w_in_even': w_in_even, 'w_out_even': w_out_even,
            'conv_w': conv_w, 'a_log': a_log, 'dt_bias': dt_bias, 'gdn_norm_w': gdn_norm_w,
            'lambda_qk': lambda_qk, 'diff_norm_w': diff_norm_w, 'w_in_odd': w_in_odd, 'w_out_odd': w_out_odd,
            'b_i': b_i, 'b_f': b_f, 'mlstm_norm_w': mlstm_norm_w}


def reference(x_prompt, x_sample, cache_k, cache_v, page_table, state_gdn_conv, state_gdn_s,
              state_mlstm_c, state_mlstm_n, state_mlstm_m, norm_w, final_norm_w, w_in_even, w_out_even,
              conv_w, a_log, dt_bias, gdn_norm_w, lambda_qk, diff_norm_w, w_in_odd, w_out_odd,
              b_i, b_f, mlstm_norm_w):
    weights = (norm_w, final_norm_w, w_in_even, w_out_even, conv_w, a_log, dt_bias, gdn_norm_w,
               lambda_qk, diff_norm_w, w_in_odd, w_out_odd, b_i, b_f, mlstm_norm_w)
    bp = x_prompt.shape[0]
    f32 = jnp.float32
    conv0 = jnp.zeros((N_EVEN, bp, CONV_W - 1, A_CONV_CH), x_prompt.dtype)
    s0 = jnp.zeros((N_EVEN, bp, H_A, DK_A, DV_A), f32)
    c0 = jnp.zeros((N_ODD, bp, H_C, DQK_C, DV_C), f32)
    n0 = jnp.zeros((N_ODD, bp, H_C, DQK_C), f32)
    m0 = jnp.zeros((N_ODD, bp, H_C), f32)
    y_prompt, st_p = _trunk(x_prompt, 0, conv0, s0, c0, n0, m0, None, None, None, *weights)
    k_rows_p, v_rows_p, gdn_conv_p, gdn_s_p, ml_c_p, ml_n_p, ml_m_p = st_p
    past_len = page_table.shape[1] * cache_k.shape[2]
    y_sample, st_s = _trunk(x_sample, past_len, state_gdn_conv, state_gdn_s, state_mlstm_c, state_mlstm_n,
                            state_mlstm_m, cache_k, cache_v, page_table, *weights)
    k_rows_s, v_rows_s, gdn_conv_s, gdn_s_s, ml_c_s, ml_n_s, ml_m_s = st_s
    return (y_prompt, y_sample, k_rows_p, v_rows_p, k_rows_s, v_rows_s, gdn_conv_p, gdn_s_p, gdn_conv_s, gdn_s_s,
            ml_c_p, ml_n_p, ml_m_p, ml_c_s, ml_n_s, ml_m_s)
```

```python
import functools
import math

import jax
import jax.numpy as jnp
from jax import lax
from jax.experimental import pallas as pl
from jax.experimental.pallas import tpu as pltpu

F32 = jnp.float32
BF16 = jnp.bfloat16

EPS = 1e-6
NEG = -1e30
N_HEADS = 4
CONV_W = 4
ROPE_THETA = 500000.0
CHUNK = 64
SAMPLE_PAD = 8
PAGES_PER_STEP = 4
V7X_VMEM_LIMIT = 56 * 1024 * 1024


def _cparams(sem):
    return pltpu.CompilerParams(dimension_semantics=sem, vmem_limit_bytes=V7X_VMEM_LIMIT)


def _dot(a, b):
    return jnp.dot(a, b, preferred_element_type=F32)


def _dot_nt(a, b):
    return lax.dot_general(a, b, (((1,), (1,)), ((), ())), preferred_element_type=F32)


def _dot_tn(a, b):
    return lax.dot_general(a, b, (((0,), (0,)), ((), ())), preferred_element_type=F32)


def _split3(x):
    hi = x.astype(BF16)
    r = x - hi.astype(F32)
    mid = r.astype(BF16)
    lo = (r - mid.astype(F32)).astype(BF16)
    return hi, mid, lo


def _exact_dot(a_bf16, x):
    hi, mid, lo = _split3(x)
    return _dot(a_bf16, hi) + (_dot(a_bf16, mid) + _dot(a_bf16, lo))


def _exact_dot_nt(a_bf16, x):
    hi, mid, lo = _split3(x)
    return _dot_nt(a_bf16, hi) + (_dot_nt(a_bf16, mid) + _dot_nt(a_bf16, lo))


def _dot3(a, b):
    ah = a.astype(BF16)
    al = (a - ah.astype(F32)).astype(BF16)
    bh = b.astype(BF16)
    bl = (b - bh.astype(F32)).astype(BF16)
    return _dot(ah, bh) + (_dot(ah, bl) + _dot(al, bh))


def _sigmoid(x):
    return 1.0 / (1.0 + jnp.exp(-x))


def _silu(x):
    return x * _sigmoid(x)


def _softplus(x):
    return jnp.maximum(x, 0.0) + jnp.log(1.0 + jnp.exp(-jnp.abs(x)))


def _rms(x, w):
    return x * lax.rsqrt(jnp.mean(x * x, axis=-1, keepdims=True) + EPS) * w


def _iota(shape, dim):
    return lax.broadcasted_iota(jnp.int32, shape, dim)


def _proj_kernel(*refs, segs, rope_blocks):
    n_out = len(segs)
    if rope_blocks:
        x_ref, nw_ref, w_ref, cos_ref, sa_ref, sb_ref = refs[:6]
        outs = refs[6:6 + n_out]
    else:
        x_ref, nw_ref, w_ref = refs[:3]
        outs = refs[3:3 + n_out]
    x = x_ref[...]
    h = _rms(x, nw_ref[...]).astype(BF16)
    for (c0, width), o_ref in zip(segs, outs):
        for s0 in range(0, width, 512):
            sw = min(512, width - s0)
            acc = _dot(h, w_ref[:, c0 + s0:c0 + s0 + sw])
            if rope_blocks and (c0 + s0) // 128 in rope_blocks:
                cos = cos_ref[...]
                sa = sa_ref[...]
                sb = sb_ref[...]
                parts = []
                for t0 in range(0, sw, 128):
                    a = acc[:, t0:t0 + 128]
                    parts.append(a * cos + pltpu.roll(a, 8, 1) * sa + pltpu.roll(a, 120, 1) * sb)
                acc = jnp.concatenate(parts, axis=1)
            o_ref[:, s0:s0 + sw] = acc


def _proj(x, nw, w, segs, rope, tm):
    n, d = x.shape
    ncols = w.shape[1]
    grid = (n // tm,)
    in_specs = [pl.BlockSpec((tm, d), lambda i: (i, 0)),
                pl.BlockSpec((1, d), lambda i: (0, 0)),
                pl.BlockSpec((d, ncols), lambda i: (0, 0))]
    args = [x, nw.reshape(1, d), w]
    rope_blocks = ()
    if rope is not None:
        rope_blocks, tables = rope
        nt = tables[0].shape[0] // tm
        for t in tables:
            in_specs.append(pl.BlockSpec((tm, 128), lambda i, nt=nt: (i % nt, 0)))
            args.append(t)
    out_shape = [jax.ShapeDtypeStruct((n, width), F32) for _, width in segs]
    out_specs = [pl.BlockSpec((tm, width), lambda i: (i, 0)) for _, width in segs]
    return pl.pallas_call(
        functools.partial(_proj_kernel, segs=tuple(segs), rope_blocks=tuple(rope_blocks)),
        grid=grid, in_specs=in_specs, out_specs=out_specs, out_shape=out_shape,
        compiler_params=_cparams(("parallel",)), name="norm_proj")(*args)


def _out_kernel(a_ref, b_ref, w_ref, x_ref, fw_ref, o_ref, *, final):
    ka = a_ref.shape[1]
    y = _dot(a_ref[...].astype(BF16), w_ref[0:ka, :]) + _dot(b_ref[...].astype(BF16), w_ref[ka:, :])
    xn = x_ref[...] + y
    if final:
        xn = _rms(xn, fw_ref[...])
    o_ref[...] = xn


def _out_proj(a, b, ka, b_col, w, x, fw, final, tm):
    n, d = x.shape
    kb = w.shape[0] - ka
    return pl.pallas_call(
        functools.partial(_out_kernel, final=final),
        grid=(n // tm,),
        in_specs=[pl.BlockSpec((tm, ka), lambda i: (i, 0)),
                  pl.BlockSpec((tm, kb), lambda i: (i, b_col)),
                  pl.BlockSpec(w.shape, lambda i: (0, 0)),
                  pl.BlockSpec((tm, d), lambda i: (i, 0)),
                  pl.BlockSpec((1, d), lambda i: (0, 0))],
        out_specs=pl.BlockSpec((tm, d), lambda i: (i, 0)),
        out_shape=jax.ShapeDtypeStruct((n, d), F32),
        compiler_params=_cparams(("parallel",)), name="out_proj")(a, b, w, x, fw.reshape(1, d))


def _neumann(a, c):
    eye = jnp.where(_iota((c, c), 0) == _iota((c, c), 1), 1.0, 0.0)
    x = eye - a
    p = a
    for _ in range(int(math.log2(c)) - 1):
        p = _dot3(p, p)
        x = x + _dot3(x, p)
    return x


def _gdn_kernel(x_ref, z_ref, gt_ref, conv0_ref, s0_ref, cw_ref, hp_ref, nw_ref,
                o_ref, convo_ref, so_ref,
                xbuf, qkv_scr, g_scr, b_scr, s_scr, *, c, nch, tv, dk):
    gi = pl.program_id(1)
    tg = c * nch
    hq = N_HEADS * dk

    @pl.when(gi == 0)
    def _():
        xbuf[0:8, :] = conv0_ref[0]
        s_scr[...] = s0_ref[0]

    xbuf[8:8 + tg, :] = x_ref[...]
    cw = cw_ref[...]
    y = ((xbuf[5:5 + tg, :] * cw[0:1, :] + xbuf[6:6 + tg, :] * cw[1:2, :])
         + (xbuf[7:7 + tg, :] * cw[2:3, :] + xbuf[8:8 + tg, :] * cw[3:4, :]))
    qkv_scr[...] = _silu(y)
    last = tg if tv == c else tv
    hist = xbuf[8 + last - 3:8 + last, :]
    xbuf[5:8, :] = hist

    gt = gt_ref[...]
    hp = hp_ref[...]
    lane = _iota((tg, 128), 1)
    beta = _sigmoid(gt)
    g = -jnp.exp(hp[0:1, :]) * _softplus(gt + hp[1:2, :])
    g = jnp.where(lane >= N_HEADS, jnp.where(lane < 2 * N_HEADS, g, 0.0), 0.0)
    if tv < c:
        valid = (_iota((tg, 128), 0) % c) < tv
        g = jnp.where(valid, g, 0.0)
        beta = jnp.where(valid, beta, 0.0)
    g_scr[...] = g
    b_scr[...] = beta

    row = _iota((c, c), 0)
    col = _iota((c, c), 1)
    ige = row >= col
    igt = row > col
    tri = jnp.where(ige, 1.0, 0.0).astype(BF16)
    lane_c = _iota((c, 128), 1)
    nw = nw_ref[...]

    def body(ci, carry):
        r0 = pl.multiple_of(ci * c, c)
        gcum = _exact_dot(tri, g_scr[pl.ds(r0, c), :])
        bch = b_scr[pl.ds(r0, c), :]
        for h in range(N_HEADS):
            qh = qkv_scr[pl.ds(r0, c), h * dk:(h + 1) * dk]
            kh = qkv_scr[pl.ds(r0, c), hq + h * dk:hq + (h + 1) * dk]
            vh = qkv_scr[pl.ds(r0, c), 2 * hq + h * dk:2 * hq + (h + 1) * dk]
            qn = qh * lax.rsqrt(jnp.sum(qh * qh, axis=-1, keepdims=True) + EPS) * (dk ** -0.5)
            kn = kh * lax.rsqrt(jnp.sum(kh * kh, axis=-1, keepdims=True) + EPS)
            gcol = gcum[:, N_HEADS + h:N_HEADS + h + 1]
            bcol = bch[:, h:h + 1]
            sel = jnp.where(lane_c == N_HEADS + h, 1.0, 0.0).astype(BF16)
            grow = _exact_dot_nt(sel, gcum)
            decay = jnp.where(ige, jnp.exp(jnp.where(ige, gcol - grow, 0.0)), 0.0)
            knb = kn.astype(BF16)
            qkk = _dot_nt(jnp.concatenate([qn, kn], axis=0).astype(BF16), knb)
            qk = qkk[:c]
            kk = qkk[c:]
            a_low = jnp.where(igt, kk * decay * bcol, 0.0)
            t_inv = _neumann(a_low, c)
            eg = jnp.exp(gcol)
            rhs = jnp.concatenate([vh * bcol, kn * (bcol * eg)], axis=1).astype(BF16)
            sol = _dot(t_inv.astype(BF16), rhs)
            u = sol[:, :dk]
            wk = sol[:, dk:]
            glast = gcum[c - 1:c, N_HEADS + h:N_HEADS + h + 1]
            qg = qn * eg
            kw = kn * jnp.exp(glast - gcol)
            s_old = s_scr[h]
            ws = _dot(jnp.concatenate([wk, qg], axis=0).astype(BF16), s_old.astype(BF16))
            vnew = u - ws[:c]
            vnb = vnew.astype(BF16)
            o = ws[c:] + _dot((qk * decay).astype(BF16), vnb)
            s_scr[h] = s_old * jnp.exp(glast) + _dot_tn(kw.astype(BF16), vnb)
            zz = z_ref[pl.ds(r0, c), h * dk:(h + 1) * dk]
            o_ref[pl.ds(r0, c), h * dk:(h + 1) * dk] = _rms(o, nw) * _silu(zz)
        return carry

    lax.fori_loop(0, nch, body, 0)

    @pl.when(gi == pl.num_programs(1) - 1)
    def _():
        convo_ref[0] = xbuf[0:8, :]
        so_ref[0] = s_scr[...]


def _gdn(qkvz, gates, conv0, s0, conv_w, hp, nw, *, bn, c, nch, tv):
    n = qkvz.shape[0]
    dk = s0.shape[-1]
    cc = 3 * N_HEADS * dk
    tg = c * nch
    steps = n // (bn * tg)
    kern = functools.partial(_gdn_kernel, c=c, nch=nch, tv=tv, dk=dk)
    return pl.pallas_call(
        kern, grid=(bn, steps),
        in_specs=[pl.BlockSpec((tg, cc), lambda b, g: (b * steps + g, 0)),
                  pl.BlockSpec((tg, N_HEADS * dk), lambda b, g: (b * steps + g, 3)),
                  pl.BlockSpec((tg, 128), lambda b, g: (b * steps + g, 0)),
                  pl.BlockSpec((1, 8, cc), lambda b, g: (b, 0, 0)),
                  pl.BlockSpec((1, N_HEADS, dk, dk), lambda b, g: (b, 0, 0, 0)),
                  pl.BlockSpec((CONV_W, cc), lambda b, g: (0, 0)),
                  pl.BlockSpec((8, 128), lambda b, g: (0, 0)),
                  pl.BlockSpec((1, dk), lambda b, g: (0, 0))],
        out_specs=[pl.BlockSpec((tg, N_HEADS * dk), lambda b, g: (b * steps + g, 0)),
                   pl.BlockSpec((1, 8, cc), lambda b, g: (b, 0, 0)),
                   pl.BlockSpec((1, N_HEADS, dk, dk), lambda b, g: (b, 0, 0, 0))],
        out_shape=[jax.ShapeDtypeStruct((n, N_HEADS * dk), F32),
                   jax.ShapeDtypeStruct((bn, 8, cc), F32),
                   jax.ShapeDtypeStruct((bn, N_HEADS, dk, dk), F32)],
        scratch_shapes=[pltpu.VMEM((8 + tg, cc), F32), pltpu.VMEM((tg, cc), F32),
                        pltpu.VMEM((tg, 128), F32), pltpu.VMEM((tg, 128), F32),
                        pltpu.VMEM((N_HEADS, dk, dk), F32)],
        compiler_params=_cparams(("parallel", "arbitrary")), name="gated_delta")(
            qkvz, qkvz, gates, conv0, s0, conv_w, hp, nw.reshape(1, dk))


def _mlstm_kernel(q_ref, k_ref, v_ref, z_ref, og_ref, gt_ref, c0_ref, n0_ref, m0_ref, hp_ref, nw_ref,
                  o_ref, co_ref, no_ref, mo_ref,
                  li_scr, lf_scr, cext, m_scr, *, c, nch, tv, dqk, dv):
    gi = pl.program_id(1)
    tg = c * nch
    lane1 = _iota((dqk, 128), 1)

    @pl.when(gi == 0)
    def _():
        for h in range(N_HEADS):
            cext[h, :, 0:dv] = c0_ref[0, h]
            cext[h, :, dv:dv + 128] = jnp.where(lane1 == 0, n0_ref[0, h], 0.0)
        m_scr[...] = m0_ref[0]

    gt = gt_ref[...]
    hp = hp_ref[...]
    lane = _iota((tg, 128), 1)
    li = gt + hp[0:1, :]
    x = gt + hp[1:2, :]
    lf = jnp.minimum(x, 0.0) - jnp.log(1.0 + jnp.exp(-jnp.abs(x)))
    lf = jnp.where(lane >= N_HEADS, jnp.where(lane < 2 * N_HEADS, lf, 0.0), 0.0)
    if tv < c:
        valid = (_iota((tg, 128), 0) % c) < tv
        lf = jnp.where(valid, lf, 0.0)
        li = jnp.where(valid, li, NEG)
    li_scr[...] = li
    lf_scr[...] = lf

    row = _iota((c, c), 0)
    col = _iota((c, c), 1)
    ige = row >= col
    tri = jnp.where(ige, 1.0, 0.0).astype(BF16)
    ones_b = jnp.ones((c, 128), BF16)
    lane_c = _iota((c, 128), 1)
    one_col = jnp.where(lane_c == 0, 1.0, 0.0)
    nw = nw_ref[...]

    def body(ci, carry):
        r0 = pl.multiple_of(ci * c, c)
        bcum = _exact_dot(tri, lf_scr[pl.ds(r0, c), :])
        lich = li_scr[pl.ds(r0, c), :]
        for h in range(N_HEADS):
            q = q_ref[pl.ds(r0, c), h * dqk:(h + 1) * dqk] * (dqk ** -0.5)
            k = k_ref[pl.ds(r0, c), h * dqk:(h + 1) * dqk]
            v = v_ref[pl.ds(r0, c), h * dv:(h + 1) * dv]
            bcol = bcum[:, N_HEADS + h:N_HEADS + h + 1]
            licol = lich[:, h:h + 1]
            src = jnp.where(lane_c == 0, licol - bcol, 0.0)
            rowv = _exact_dot_nt(ones_b, src)
            dmat = jnp.where(ige, bcol + rowv, NEG)
            dmax = jnp.max(dmat, axis=-1, keepdims=True)
            mprev = m_scr[0:1, h:h + 1]
            inter = bcol + mprev
            mt = jnp.maximum(inter, dmax)
            w = jnp.exp(dmat - mt)
            sc = jnp.exp(inter - mt)
            qb = q.astype(BF16)
            qk = _dot_nt(qb, k.astype(BF16)) * w
            c_old = cext[h]
            vext = jnp.concatenate([v, one_col], axis=1).astype(BF16)
            tot = sc * _dot(qb, c_old.astype(BF16)) + _dot(qk.astype(BF16), vext)
            num = tot[:, :dv]
            den = tot[:, dv:dv + 1]
            hh = num / jnp.maximum(jnp.abs(den), jnp.exp(-mt))
            mnew = mt[c - 1:c, :]
            blast = bcum[c - 1:c, N_HEADS + h:N_HEADS + h + 1]
            dec = jnp.exp(blast + mprev - mnew)
            wkc = jnp.exp(blast - bcol + licol - mnew)
            cext[h] = c_old * dec + _dot_tn((k * wkc).astype(BF16), vext)
            m_scr[0:1, h:h + 1] = mnew
            og = og_ref[pl.ds(r0, c), h * dv:(h + 1) * dv]
            zz = z_ref[pl.ds(r0, c), h * dv:(h + 1) * dv]
            o_ref[pl.ds(r0, c), h * dv:(h + 1) * dv] = _rms(_sigmoid(og) * hh, nw) * _silu(zz)
        return carry

    lax.fori_loop(0, nch, body, 0)

    @pl.when(gi == pl.num_programs(1) - 1)
    def _():
        for h in range(N_HEADS):
            co_ref[0, h] = cext[h, :, 0:dv]
            no_ref[0, h] = cext[h, :, dv:dv + 1]
        mo_ref[0] = m_scr[...]


def _mlstm(q, k, v, z, og, gates, c0, n0, m0, hp, nw, *, bn, c, nch, tv):
    n = q.shape[0]
    dqk, dv = c0.shape[-2], c0.shape[-1]
    tg = c * nch
    steps = n // (bn * tg)
    hq, hv = N_HEADS * dqk, N_HEADS * dv
    row = lambda b, g: (b * steps + g, 0)
    kern = functools.partial(_mlstm_kernel, c=c, nch=nch, tv=tv, dqk=dqk, dv=dv)
    return pl.pallas_call(
        kern, grid=(bn, steps),
        in_specs=[pl.BlockSpec((tg, hq), row), pl.BlockSpec((tg, hq), row),
                  pl.BlockSpec((tg, hv), row), pl.BlockSpec((tg, hv), row),
                  pl.BlockSpec((tg, hv), row), pl.BlockSpec((tg, 128), row),
                  pl.BlockSpec((1, N_HEADS, dqk, dv), lambda b, g: (b, 0, 0, 0)),
                  pl.BlockSpec((1, N_HEADS, dqk, 1), lambda b, g: (b, 0, 0, 0)),
                  pl.BlockSpec((1, 8, 128), lambda b, g: (b, 0, 0)),
                  pl.BlockSpec((8, 128), lambda b, g: (0, 0)),
                  pl.BlockSpec((1, dv), lambda b, g: (0, 0))],
        out_specs=[pl.BlockSpec((tg, hv), row),
                   pl.BlockSpec((1, N_HEADS, dqk, dv), lambda b, g: (b, 0, 0, 0)),
                   pl.BlockSpec((1, N_HEADS, dqk, 1), lambda b, g: (b, 0, 0, 0)),
                   pl.BlockSpec((1, 8, 128), lambda b, g: (b, 0, 0))],
        out_shape=[jax.ShapeDtypeStruct((n, hv), F32),
                   jax.ShapeDtypeStruct((bn, N_HEADS, dqk, dv), F32),
                   jax.ShapeDtypeStruct((bn, N_HEADS, dqk, 1), F32),
                   jax.ShapeDtypeStruct((bn, 8, 128), F32)],
        scratch_shapes=[pltpu.VMEM((tg, 128), F32), pltpu.VMEM((tg, 128), F32),
                        pltpu.VMEM((N_HEADS, dqk, dv + 128), F32), pltpu.VMEM((8, 128), F32)],
        compiler_params=_cparams(("parallel", "arbitrary")), name="mlstm")(
            q, k, v, z, og, gates, c0, n0, m0, hp, nw.reshape(1, dv))


def _lambda(lq, lam_init):
    a = jnp.sum(lq[0:1, :] * lq[1:2, :], axis=-1, keepdims=True)
    b = jnp.sum(lq[2:3, :] * lq[3:4, :], axis=-1, keepdims=True)
    return jnp.exp(a) - jnp.exp(b) + lam_init


def _attn_kernel(q_ref, k_ref, v_ref, z_ref, lq_ref, nw_ref, o_ref, m_scr, l_scr, acc_scr, *, t, lam_init):
    qi = pl.program_id(2)
    dqk = q_ref.shape[1] // 2
    q = q_ref[...] * (dqk ** -0.5)
    lane = _iota(q.shape, 1)
    qs = jnp.concatenate([jnp.where(lane < dqk, q, 0.0), jnp.where(lane >= dqk, q, 0.0)], axis=0).astype(BF16)
    m_scr[...] = jnp.full(m_scr.shape, NEG, F32)
    l_scr[...] = jnp.zeros(l_scr.shape, F32)
    acc_scr[...] = jnp.zeros(acc_scr.shape, F32)

    def step(j, masked):
        r0 = pl.multiple_of(j * t, t)
        kb = k_ref[pl.ds(r0, t), :].astype(BF16)
        vb = v_ref[pl.ds(r0, t), :].astype(BF16)
        s = _dot_nt(qs, kb)
        if masked:
            rr = _iota(s.shape, 0)
            rr = jnp.where(rr >= t, rr - t, rr)
            s = jnp.where(_iota(s.shape, 1) <= rr, s, NEG)
        m_prev = m_scr[...]
        m_new = jnp.maximum(m_prev, jnp.max(s, axis=-1, keepdims=True))
        alpha = jnp.exp(m_prev - m_new)
        p = jnp.exp(s - m_new[:, 0:1])
        l_scr[...] = alpha * l_scr[...] + jnp.sum(p, axis=-1, keepdims=True)
        acc_scr[...] = alpha * acc_scr[...] + _dot(p.astype(BF16), vb)
        m_scr[...] = m_new

    def loop_body(j, carry):
        step(j, False)
        return carry

    lax.fori_loop(0, qi, loop_body, 0)
    step(qi, True)

    acc = acc_scr[...]
    l = l_scr[...]
    lam = _lambda(lq_ref[...], lam_init)
    o = acc[:t] / l[:t] - lam * (acc[t:] / l[t:])
    o_ref[...] = _rms(o, nw_ref[...]) * (1.0 - lam_init) * _silu(z_ref[...])


def _attn_prompt(q, k, v, z, lq, nw, *, bn, seq, t, lam_init):
    n = q.shape[0]
    dh = q.shape[1] // N_HEADS
    nq = seq // t
    kern = functools.partial(_attn_kernel, t=t, lam_init=lam_init)
    return pl.pallas_call(
        kern, grid=(bn, N_HEADS, nq),
        in_specs=[pl.BlockSpec((t, dh), lambda b, h, i: (b * nq + i, h)),
                  pl.BlockSpec((seq, dh), lambda b, h, i: (b, h)),
                  pl.BlockSpec((seq, dh), lambda b, h, i: (b, h)),
                  pl.BlockSpec((t, dh), lambda b, h, i: (b * nq + i, h)),
                  pl.BlockSpec(lq.shape, lambda b, h, i: (0, 0)),
                  pl.BlockSpec((1, dh), lambda b, h, i: (0, 0))],
        out_specs=pl.BlockSpec((t, dh), lambda b, h, i: (b * nq + i, h)),
        out_shape=jax.ShapeDtypeStruct((n, N_HEADS * dh), F32),
        scratch_shapes=[pltpu.VMEM((2 * t, 128), F32), pltpu.VMEM((2 * t, 128), F32),
                        pltpu.VMEM((2 * t, dh), F32)],
        compiler_params=_cparams(("parallel", "parallel", "arbitrary")), name="diff_attn_prompt")(
            q, k, v, z, lq, nw.reshape(1, dh))


def _attn_s_kernel(pt_ref, q_ref, kn_ref, vn_ref, z_ref, lq_ref, nw_ref, *rest, pg, tv, lam_init):
    k_pages = rest[:pg]
    v_pages = rest[pg:2 * pg]
    o_ref, qbd, m_scr, l_scr, acc_scr = rest[2 * pg:]
    j = pl.program_id(1)
    tp, hd = q_ref.shape
    dh = hd // N_HEADS
    dqk = dh // 2
    nmap = 2 * N_HEADS
    rows = nmap * tp

    @pl.when(j == 0)
    def _():
        q = q_ref[...] * (dqk ** -0.5)
        lane = _iota(q.shape, 1)
        for i in range(nmap):
            qbd[i * tp:(i + 1) * tp, :] = jnp.where(lane >= i * dqk, jnp.where(lane < (i + 1) * dqk, q, 0.0), 0.0)
        m_scr[...] = jnp.full(m_scr.shape, NEG, F32)
        l_scr[...] = jnp.zeros(l_scr.shape, F32)
        acc_scr[...] = jnp.zeros(acc_scr.shape, F32)

    def update(s, pv):
        m_prev = m_scr[...]
        m_new = jnp.maximum(m_prev, jnp.max(s, axis=-1, keepdims=True))
        alpha = jnp.exp(m_prev - m_new)
        p = jnp.exp(s - m_new[:, 0:1])
        l_scr[...] = alpha * l_scr[...] + jnp.sum(p, axis=-1, keepdims=True)
        acc_scr[...] = jnp.concatenate([alpha] * (hd // 128), axis=1) * acc_scr[...] + pv(p)
        m_scr[...] = m_new

    qb = qbd[...].astype(BF16)
    s = jnp.concatenate([_dot_nt(qb, kp[...].astype(BF16)) for kp in k_pages], axis=1)

    def pv_pages(p):
        pb = p.astype(BF16)
        page = k_pages[0].shape[0]
        out = _dot(pb[:, 0:page], v_pages[0][...].astype(BF16))
        for i in range(1, pg):
            out = out + _dot(pb[:, i * page:(i + 1) * page], v_pages[i][...].astype(BF16))
        return out

    update(s, pv_pages)

    @pl.when(j == pl.num_programs(1) - 1)
    def _():
        sn = _dot_nt(qbd[...], kn_ref[...])
        tq = _iota(sn.shape, 0) % tp
        tk = _iota(sn.shape, 1)
        sn = jnp.where(tk <= tq, jnp.where(tk < tv, sn, NEG), NEG)
        update(sn, lambda p: _dot(p, vn_ref[...]))
        acc = acc_scr[...]
        inv = 1.0 / l_scr[...]
        lam = _lambda(lq_ref[...], lam_init)
        lane = _iota((tp, hd), 1)
        o = jnp.zeros((tp, hd), F32)
        for h in range(N_HEADS):
            r1 = 2 * h * tp
            r2 = r1 + tp
            invt1 = jnp.concatenate([inv[r1:r1 + tp]] * N_HEADS, axis=1)
            invt2 = jnp.concatenate([inv[r2:r2 + tp]] * N_HEADS, axis=1)
            oh = acc[r1:r1 + tp] * invt1 - lam * (acc[r2:r2 + tp] * invt2)
            o = o + jnp.where(lane >= h * dh, jnp.where(lane < (h + 1) * dh, oh, 0.0), 0.0)
        nw = nw_ref[...]
        normed = jnp.concatenate([_rms(o[:, h * dh:(h + 1) * dh], nw) for h in range(N_HEADS)], axis=1)
        o_ref[...] = normed * (1.0 - lam_init) * _silu(z_ref[...])


def _attn_sample(page_table, q, kn, vn, z, lq, nw, cache_k, cache_v, *, e, tv, lam_init):
    bs, npages = page_table.shape
    n, hd = q.shape
    tp = n // bs
    dh = hd // N_HEADS
    pg = PAGES_PER_STEP
    page = cache_k.shape[2]
    steps = npages // pg
    rows = 2 * N_HEADS * tp
    tok = lambda b, j, pt: (b, 0)
    page_specs = [pl.BlockSpec((None, None, page, hd), lambda b, j, pt, i=i: (e, pt[b, j * pg + i], 0, 0))
                  for i in range(pg)]
    kern = functools.partial(_attn_s_kernel, pg=pg, tv=tv, lam_init=lam_init)
    grid_spec = pltpu.PrefetchScalarGridSpec(
        num_scalar_prefetch=1, grid=(bs, steps),
        in_specs=[pl.BlockSpec((tp, hd), tok), pl.BlockSpec((tp, hd), tok), pl.BlockSpec((tp, hd), tok),
                  pl.BlockSpec((tp, hd), tok),
                  pl.BlockSpec(lq.shape, lambda b, j, pt: (0, 0)),
                  pl.BlockSpec((1, dh), lambda b, j, pt: (0, 0))] + page_specs + page_specs,
        out_specs=pl.BlockSpec((tp, hd), tok),
        scratch_shapes=[pltpu.VMEM((rows, hd), F32), pltpu.VMEM((rows, 128), F32),
                        pltpu.VMEM((rows, 128), F32), pltpu.VMEM((rows, hd), F32)])
    return pl.pallas_call(
        kern, grid_spec=grid_spec, out_shape=jax.ShapeDtypeStruct((n, hd), F32),
        compiler_params=_cparams(("parallel", "arbitrary")), name="diff_attn_sample")(
            page_table, q, kn, vn, z, lq, nw.reshape(1, dh), *([cache_k] * pg), *([cache_v] * pg))


def _rope_tables(pos, dqk):
    rope_dim = dqk // 4
    half = rope_dim // 2
    inv = jnp.power(ROPE_THETA, -jnp.arange(half, dtype=F32) / half)
    ang = pos.astype(F32)[:, None] * inv[None, :]
    lane = jnp.arange(128) % dqk
    idx = lane % half
    cos = jnp.where(lane < rope_dim, jnp.cos(ang)[:, idx], 1.0)
    sin = jnp.sin(ang)[:, idx]
    sa = jnp.where((lane >= half) & (lane < rope_dim), sin, 0.0)
    sb = jnp.where(lane < half, -sin, 0.0)
    return cos.astype(F32), sa.astype(F32), sb.astype(F32)


def _lane_row(vals, offset):
    out = jnp.zeros((128,), F32)
    return out.at[offset:offset + vals.shape[0]].set(vals.astype(F32))


def kernel(x_prompt, x_sample, cache_k, cache_v, page_table, state_gdn_conv, state_gdn_s, state_mlstm_c,
           state_mlstm_n, state_mlstm_m, norm_w, final_norm_w, w_in_even, w_out_even, conv_w, a_log, dt_bias,
           gdn_norm_w, lambda_qk, diff_norm_w, w_in_odd, w_out_odd, b_i, b_f, mlstm_norm_w):
    bp, seq, d = x_prompt.shape
    bs, ts, _ = x_sample.shape
    depth = norm_w.shape[0]
    tp = SAMPLE_PAD
    dk_a = state_gdn_s.shape[-1]
    conv_ch = state_gdn_conv.shape[-1]
    w_a = N_HEADS * dk_a
    dh_b = cache_v.shape[-1]
    dqk_b = dh_b // 2
    w_b = N_HEADS * dh_b
    dqk_c, dv_c = state_mlstm_c.shape[-2], state_mlstm_c.shape[-1]
    hq_c, w_c = N_HEADS * dqk_c, N_HEADS * dv_c
    past = page_table.shape[1] * cache_k.shape[2]
    n_pool, page = cache_k.shape[1], cache_k.shape[2]

    xp = x_prompt.reshape(bp * seq, d)
    xs = jnp.pad(x_sample, ((0, 0), (0, tp - ts), (0, 0))).reshape(bs * tp, d)
    tm_p = math.gcd(bp * seq, 256)
    tm_s = math.gcd(bs * tp, 256)
    t_attn = math.gcd(seq, 512)
    nch_p = math.gcd(seq // CHUNK, 4)

    rope_p = _rope_tables(jnp.arange(seq), dqk_b)
    rope_s = _rope_tables(past + (jnp.arange(tm_s) % tp), dqk_b)

    even_w = conv_ch + w_a
    seg_even = [(0, even_w), (even_w, w_b), (even_w + w_b, w_b), (even_w + 2 * w_b, w_b),
                (even_w + 3 * w_b, w_b), (even_w + 4 * w_b, 128)]
    rope_even = tuple(range(even_w // 128, (even_w + 2 * w_b) // 128))
    odd_w = 2 * hq_c + 3 * w_c
    seg_odd = [(0, hq_c), (hq_c, hq_c), (2 * hq_c, w_c), (2 * hq_c + w_c, w_c), (2 * hq_c + 2 * w_c, w_c),
               (odd_w, 128)]

    ck = cache_k.reshape(cache_k.shape[0], n_pool, page, w_b)
    cv = cache_v.reshape(cache_v.shape[0], n_pool, page, w_b)

    k_rows_p, v_rows_p, k_rows_s, v_rows_s = [], [], [], []
    conv_p, s_p, conv_s, s_s = [], [], [], []
    c_p, n_p, m_p, c_s, n_s, m_s = [], [], [], [], [], []

    for layer in range(depth):
        last = layer == depth - 1
        if layer % 2 == 0:
            e = layer // 2
            lam_init = 0.8 - 0.6 * math.exp(-0.3 * layer)
            w = w_in_even[e]
            g0 = even_w
            w_r = jnp.concatenate([w[:, :g0], w[:, g0 + 2 * N_HEADS:], w[:, g0:g0 + 2 * N_HEADS],
                                   jnp.zeros((d, 128 - 2 * N_HEADS), F32)], axis=1).astype(BF16)
            w_o = w_out_even[e].astype(BF16)
            hp = jnp.zeros((8, 128), F32).at[0].set(_lane_row(a_log[e], N_HEADS)).at[1].set(
                _lane_row(dt_bias[e], N_HEADS))
            lq = lambda_qk[e]

            az, qb, kb, vb, zb, gt = _proj(xp, norm_w[layer], w_r, seg_even, (rope_even, rope_p), tm_p)
            oa, cst, sst = _gdn(az, gt, jnp.zeros((bp, 8, conv_ch), F32), jnp.zeros((bp, N_HEADS, dk_a, dk_a), F32),
                                conv_w[e], hp, gdn_norm_w[e], bn=bp, c=CHUNK, nch=nch_p, tv=CHUNK)
            ob = _attn_prompt(qb, kb, vb, zb, lq, diff_norm_w[e], bn=bp, seq=seq, t=t_attn, lam_init=lam_init)
            xp = _out_proj(oa, ob, w_a, 0, w_o, xp, final_norm_w, last, tm_p)
            k_rows_p.append(kb.reshape(bp, seq, N_HEADS, dh_b))
            v_rows_p.append(vb.reshape(bp, seq, N_HEADS, dh_b))
            conv_p.append(cst[:, 8 - (CONV_W - 1):])
            s_p.append(sst)

            az, qb, kb, vb, zb, gt = _proj(xs, norm_w[layer], w_r, seg_even, (rope_even, rope_s), tm_s)
            conv0 = jnp.pad(state_gdn_conv[e], ((0, 0), (8 - (CONV_W - 1), 0), (0, 0)))
            oa, cst, sst = _gdn(az, gt, conv0, state_gdn_s[e], conv_w[e], hp, gdn_norm_w[e],
                                bn=bs, c=tp, nch=1, tv=ts)
            ob = _attn_sample(page_table, qb, kb, vb, zb, lq, diff_norm_w[e], ck, cv, e=e, tv=ts,
                              lam_init=lam_init)
            xs = _out_proj(oa, ob, w_a, 0, w_o, xs, final_norm_w, last, tm_s)
            k_rows_s.append(kb.reshape(bs, tp, N_HEADS, dh_b)[:, :ts])
            v_rows_s.append(vb.reshape(bs, tp, N_HEADS, dh_b)[:, :ts])
            conv_s.append(cst[:, 8 - (CONV_W - 1):])
            s_s.append(sst)
        else:
            o = layer // 2
            w_r = jnp.concatenate([w_in_odd[o], jnp.zeros((d, 128 - 2 * N_HEADS), F32)], axis=1).astype(BF16)
            w_o = w_out_odd[o].astype(BF16)
            hp = jnp.zeros((8, 128), F32).at[0].set(_lane_row(b_i[o], 0)).at[1].set(_lane_row(b_f[o], N_HEADS))

            q, k, v, z, og, gt = _proj(xp, norm_w[layer], w_r, seg_odd, None, tm_p)
            hm, cf, nf, mf = _mlstm(q, k, v, z, og, gt, jnp.zeros((bp, N_HEADS, dqk_c, dv_c), F32),
                                    jnp.zeros((bp, N_HEADS, dqk_c, 1), F32), jnp.zeros((bp, 8, 128), F32),
                                    hp, mlstm_norm_w[o], bn=bp, c=CHUNK, nch=nch_p, tv=CHUNK)
            xp = _out_proj(hm, hm, w_c // 2, 1, w_o, xp, final_norm_w, last, tm_p)
            c_p.append(cf)
            n_p.append(nf[..., 0])
            m_p.append(mf[:, 0, :N_HEADS])

            q, k, v, z, og, gt = _proj(xs, norm_w[layer], w_r, seg_odd, None, tm_s)
            m0 = jnp.zeros((bs, 8, 128), F32).at[:, 0, :N_HEADS].set(state_mlstm_m[o])
            hm, cf, nf, mf = _mlstm(q, k, v, z, og, gt, state_mlstm_c[o], state_mlstm_n[o][..., None], m0,
                                    hp, mlstm_norm_w[o], bn=bs, c=tp, nch=1, tv=ts)
            xs = _out_proj(hm, hm, w_c // 2, 1, w_o, xs, final_norm_w, last, tm_s)
            c_s.append(cf)
            n_s.append(nf[..., 0])
            m_s.append(mf[:, 0, :N_HEADS])

    y_prompt = xp.reshape(bp, seq, d)
    y_sample = xs.reshape(bs, tp, d)[:, :ts]
    st = jnp.stack
    return (y_prompt, y_sample, st(k_rows_p), st(v_rows_p), st(k_rows_s), st(v_rows_s),
            st(conv_p), st(s_p), st(conv_s), st(s_s),
            st(c_p), st(n_p), st(m_p), st(c_s), st(n_s), st(m_s))
```

```python
import functools
import math

import jax
import jax.numpy as jnp
from jax import lax
from jax.experimental import pallas as pl
from jax.experimental.pallas import tpu as pltpu

F32 = jnp.float32
BF16 = jnp.bfloat16

EPS = 1e-6
NEG = -1e30
N_HEADS = 4
CONV_W = 4
ROPE_THETA = 500000.0
CHUNK = 64
SAMPLE_PAD = 8
PAGES_PER_STEP = 4
V7X_VMEM_LIMIT = 56 * 1024 * 1024


def _cparams(sem):
    return pltpu.CompilerParams(dimension_semantics=sem, vmem_limit_bytes=V7X_VMEM_LIMIT)


def _dot(a, b):
    return jnp.dot(a, b, preferred_element_type=F32)


def _dot_nt(a, b):
    return lax.dot_general(a, b, (((1,), (1,)), ((), ())), preferred_element_type=F32)


def _dot_tn(a, b):
    return lax.dot_general(a, b, (((0,), (0,)), ((), ())), preferred_element_type=F32)


def _split3(x):
    hi = x.astype(BF16)
    r = x - hi.astype(F32)
    mid = r.astype(BF16)
    lo = (r - mid.astype(F32)).astype(BF16)
    return hi, mid, lo


def _exact_dot(a_bf16, x):
    hi, mid, lo = _split3(x)
    return _dot(a_bf16, hi) + (_dot(a_bf16, mid) + _dot(a_bf16, lo))


def _exact_dot_nt(a_bf16, x):
    hi, mid, lo = _split3(x)
    return _dot_nt(a_bf16, hi) + (_dot_nt(a_bf16, mid) + _dot_nt(a_bf16, lo))


def _sigmoid(x):
    return 1.0 / (1.0 + jnp.exp(-x))


def _silu(x):
    return x * _sigmoid(x)


def _softplus(x):
    return jnp.maximum(x, 0.0) + jnp.log(1.0 + jnp.exp(-jnp.abs(x)))


def _rms(x, w):
    return x * lax.rsqrt(jnp.mean(x * x, axis=-1, keepdims=True) + EPS) * w


def _iota(shape, dim):
    return lax.broadcasted_iota(jnp.int32, shape, dim)


def _proj_kernel(*refs, segs, rope_blocks, k16_seg, feat_major):
    n_out = len(segs)
    pos = 3
    x_ref, nw_ref, w_ref = refs[:3]
    if rope_blocks:
        cos_ref, sa_ref, sb_ref = refs[pos:pos + 3]
        pos += 3
    if feat_major:
        wqt_ref, wvt_ref, cost_ref, sint_ref = refs[pos:pos + 4]
        pos += 4
    outs = refs[pos:pos + n_out]
    pos += n_out
    x = x_ref[...]
    h = _rms(x, nw_ref[...]).astype(BF16)
    for si, ((c0, width), o_ref) in enumerate(zip(segs, outs)):
        for s0 in range(0, width, 512):
            sw = min(512, width - s0)
            acc = _dot(h, w_ref[:, c0 + s0:c0 + s0 + sw])
            if rope_blocks and (c0 + s0) // 128 in rope_blocks:
                cos = cos_ref[...]
                sa = sa_ref[...]
                sb = sb_ref[...]
                parts = []
                for t0 in range(0, sw, 128):
                    a = acc[:, t0:t0 + 128]
                    parts.append(a * cos + pltpu.roll(a, 8, 1) * sa + pltpu.roll(a, 120, 1) * sb)
                acc = jnp.concatenate(parts, axis=1)
            o_ref[:, s0:s0 + sw] = acc
            if si == k16_seg:
                refs[pos][:, s0:s0 + sw] = acc.astype(BF16)
    if feat_major:
        wq_ref, vt_ref = refs[pos + 1], refs[pos + 2]
        tm = x.shape[0]
        vt_ref[0] = _dot_nt(wvt_ref[...], h).astype(BF16)
        qt = _dot_nt(wqt_ref[...], h)
        dqk = qt.shape[0] // (2 * N_HEADS)
        half = dqk // 8
        cos = cost_ref[...]
        sin = sint_ref[...]
        scale = (dqk ** -0.5) * math.log2(math.e)
        zero = jnp.zeros((dqk, tm), F32)
        for hd in range(N_HEADS):
            maps = []
            for m in range(2):
                r0 = (2 * hd + m) * dqk
                x1 = qt[r0:r0 + half]
                x2 = qt[r0 + half:r0 + 2 * half]
                maps.append(jnp.concatenate([x1 * cos - x2 * sin, x2 * cos + x1 * sin,
                                             qt[r0 + 2 * half:r0 + dqk]], axis=0) * scale)
            top = jnp.concatenate([maps[0], zero], axis=1)
            bot = jnp.concatenate([zero, maps[1]], axis=1)
            wq_ref[0, hd] = jnp.concatenate([top, bot], axis=0).astype(BF16)


def _proj(x, nw, w, segs, rope, tm, k16_seg=None, feat=None):
    n, d = x.shape
    ncols = w.shape[1]
    grid = (n // tm,)
    in_specs = [pl.BlockSpec((tm, d), lambda i: (i, 0)),
                pl.BlockSpec((1, d), lambda i: (0, 0)),
                pl.BlockSpec((d, ncols), lambda i: (0, 0))]
    args = [x, nw.reshape(1, d), w]
    rope_blocks = ()
    if rope is not None:
        rope_blocks, tables = rope
        nt = tables[0].shape[0] // tm
        for t in tables:
            in_specs.append(pl.BlockSpec((tm, 128), lambda i, nt=nt: (i % nt, 0)))
            args.append(t)
    out_shape = [jax.ShapeDtypeStruct((n, width), F32) for _, width in segs]
    out_specs = [pl.BlockSpec((tm, width), lambda i: (i, 0)) for _, width in segs]
    if feat is not None:
        wqt, wvt, cost, sint = feat
        ntt = cost.shape[1] // tm
        in_specs += [pl.BlockSpec(wqt.shape, lambda i: (0, 0)), pl.BlockSpec(wvt.shape, lambda i: (0, 0)),
                     pl.BlockSpec((cost.shape[0], tm), lambda i, ntt=ntt: (0, i % ntt)),
                     pl.BlockSpec((sint.shape[0], tm), lambda i, ntt=ntt: (0, i % ntt))]
        args += [wqt, wvt, cost, sint]
    if k16_seg is not None:
        kw = segs[k16_seg][1]
        out_shape.append(jax.ShapeDtypeStruct((n, kw), BF16))
        out_specs.append(pl.BlockSpec((tm, kw), lambda i: (i, 0)))
    if feat is not None:
        hq, hv = wqt.shape[0], wvt.shape[0]
        dh = hq // N_HEADS
        out_shape += [jax.ShapeDtypeStruct((n // tm, N_HEADS, dh, 2 * tm), BF16),
                      jax.ShapeDtypeStruct((n // tm, hv, tm), BF16)]
        out_specs += [pl.BlockSpec((1, N_HEADS, dh, 2 * tm), lambda i: (i, 0, 0, 0)),
                      pl.BlockSpec((1, hv, tm), lambda i: (i, 0, 0))]
    return pl.pallas_call(
        functools.partial(_proj_kernel, segs=tuple(segs), rope_blocks=tuple(rope_blocks), k16_seg=k16_seg,
                          feat_major=feat is not None),
        grid=grid, in_specs=in_specs, out_specs=out_specs, out_shape=out_shape,
        compiler_params=_cparams(("parallel",)), name="norm_proj")(*args)


def _out_kernel(a_ref, b_ref, w_ref, x_ref, fw_ref, o_ref, *, final):
    ka = a_ref.shape[1]
    y = _dot(a_ref[...].astype(BF16), w_ref[0:ka, :]) + _dot(b_ref[...].astype(BF16), w_ref[ka:, :])
    xn = x_ref[...] + y
    if final:
        xn = _rms(xn, fw_ref[...])
    o_ref[...] = xn


def _out_proj(a, b, ka, b_col, w, x, fw, final, tm):
    n, d = x.shape
    kb = w.shape[0] - ka
    return pl.pallas_call(
        functools.partial(_out_kernel, final=final),
        grid=(n // tm,),
        in_specs=[pl.BlockSpec((tm, ka), lambda i: (i, 0)),
                  pl.BlockSpec((tm, kb), lambda i: (i, b_col)),
                  pl.BlockSpec(w.shape, lambda i: (0, 0)),
                  pl.BlockSpec((tm, d), lambda i: (i, 0)),
                  pl.BlockSpec((1, d), lambda i: (0, 0))],
        out_specs=pl.BlockSpec((tm, d), lambda i: (i, 0)),
        out_shape=jax.ShapeDtypeStruct((n, d), F32),
        compiler_params=_cparams(("parallel",)), name="out_proj")(a, b, w, x, fw.reshape(1, d))


def _neumann(a_list, c):
    eye = jnp.where(_iota((c, c), 0) == _iota((c, c), 1), 1.0, 0.0)
    ts = [eye - a for a in a_list]
    abs_ = [a.astype(BF16) for a in a_list]
    ms = [_dot(ab, ab) for ab in abs_]
    levels = int(math.log2(c)) - 1
    for k in range(levels):
        mbs = [m.astype(BF16) for m in ms]
        ts = [t + _dot(mb, t.astype(BF16)) for mb, t in zip(mbs, ts)]
        if k + 1 < levels:
            ms = [_dot(mb, mb) for mb in mbs]
    return ts


def _gdn_kernel(x_ref, z_ref, gt_ref, conv0_ref, s0_ref, cw_ref, hp_ref, nw_ref,
                o_ref, convo_ref, so_ref,
                xbuf, qkv_scr, g_scr, b_scr, s_scr, *, c, nch, tv, dk, unroll):
    gi = pl.program_id(1)
    tg = c * nch
    hq = N_HEADS * dk

    @pl.when(gi == 0)
    def _():
        xbuf[0:8, :] = conv0_ref[0]
        s_scr[...] = s0_ref[0]

    xbuf[8:8 + tg, :] = x_ref[...]
    cw = cw_ref[...]
    y = ((xbuf[5:5 + tg, :] * cw[0:1, :] + xbuf[6:6 + tg, :] * cw[1:2, :])
         + (xbuf[7:7 + tg, :] * cw[2:3, :] + xbuf[8:8 + tg, :] * cw[3:4, :]))
    qkv_scr[...] = _silu(y)
    last = tg if tv == c else tv
    hist = xbuf[8 + last - 3:8 + last, :]
    xbuf[5:8, :] = hist

    gt = gt_ref[...]
    hp = hp_ref[...]
    lane = _iota((tg, 128), 1)
    beta = _sigmoid(gt)
    g = -jnp.exp(hp[0:1, :]) * _softplus(gt + hp[1:2, :])
    g = jnp.where(lane >= N_HEADS, jnp.where(lane < 2 * N_HEADS, g, 0.0), 0.0)
    if tv < c:
        valid = (_iota((tg, 128), 0) % c) < tv
        g = jnp.where(valid, g, 0.0)
        beta = jnp.where(valid, beta, 0.0)
    g_scr[...] = g
    b_scr[...] = beta

    row = _iota((c, c), 0)
    col = _iota((c, c), 1)
    ige = row >= col
    igt = row > col
    tri = jnp.where(ige, 1.0, 0.0).astype(BF16)
    lane_c = _iota((c, 128), 1)
    nw = nw_ref[...]

    sels = [jnp.where(lane_c == N_HEADS + h, 1.0, 0.0).astype(BF16) for h in range(N_HEADS)]

    def prep(starts):
        items = [(ci, h) for ci in range(len(starts)) for h in range(N_HEADS)]
        gcums = [_exact_dot(tri, g_scr[pl.ds(r0, c), :]) for r0 in starts]
        bchs = [b_scr[pl.ds(r0, c), :] for r0 in starts]
        qns, kns, vhs = [], [], []
        for ci, h in items:
            r0 = starts[ci]
            qh = qkv_scr[pl.ds(r0, c), h * dk:(h + 1) * dk]
            kh = qkv_scr[pl.ds(r0, c), hq + h * dk:hq + (h + 1) * dk]
            vhs.append(qkv_scr[pl.ds(r0, c), 2 * hq + h * dk:2 * hq + (h + 1) * dk])
            qns.append(qh * lax.rsqrt(jnp.sum(qh * qh, axis=-1, keepdims=True) + EPS) * (dk ** -0.5))
            kns.append(kh * lax.rsqrt(jnp.sum(kh * kh, axis=-1, keepdims=True) + EPS))
        qkks = [_dot_nt(jnp.concatenate([qn, kn], axis=0).astype(BF16), kn.astype(BF16))
                for qn, kn in zip(qns, kns)]
        grows = [_exact_dot_nt(sels[h], gcums[ci]) for ci, h in items]
        gcols = [gcums[ci][:, N_HEADS + h:N_HEADS + h + 1] for ci, h in items]
        bcols = [bchs[ci][:, h:h + 1] for ci, h in items]
        decays = [jnp.where(ige, jnp.exp(jnp.where(ige, gcol - grow, 0.0)), 0.0)
                  for gcol, grow in zip(gcols, grows)]
        t_invs = _neumann([jnp.where(igt, qkk[c:] * decay * bcol, 0.0)
                           for qkk, decay, bcol in zip(qkks, decays, bcols)], c)
        egs = [jnp.exp(gcol) for gcol in gcols]
        sols = [_dot(t_inv.astype(BF16),
                     jnp.concatenate([vh * bcol, kn * (bcol * eg)], axis=1).astype(BF16))
                for t_inv, vh, kn, bcol, eg in zip(t_invs, vhs, kns, bcols, egs)]
        out = []
        for i, (ci, h) in enumerate(items):
            glast = gcums[ci][c - 1:c, N_HEADS + h:N_HEADS + h + 1]
            wq = jnp.concatenate([sols[i][:, dk:], qns[i] * egs[i]], axis=0).astype(BF16)
            out.append((sols[i][:, :dk], wq, (qkks[i][:c] * decays[i]).astype(BF16),
                        (kns[i] * jnp.exp(glast - gcols[i])).astype(BF16), jnp.exp(glast)))
        return [out[ci * N_HEADS:(ci + 1) * N_HEADS] for ci in range(len(starts))]

    def body(it, carry):
        starts = [pl.multiple_of((it * unroll + i) * c, c) for i in range(unroll)]
        for r0, heads in zip(starts, prep(starts)):
            s_olds = [s_scr[h] for h in range(N_HEADS)]
            wss = [_dot(heads[h][1], s_olds[h].astype(BF16)) for h in range(N_HEADS)]
            vnbs = [(heads[h][0] - wss[h][:c]).astype(BF16) for h in range(N_HEADS)]
            for h in range(N_HEADS):
                s_scr[h] = s_olds[h] * heads[h][4] + _dot_tn(heads[h][3], vnbs[h])
            os_ = [wss[h][c:] + _dot(heads[h][2], vnbs[h]) for h in range(N_HEADS)]
            for h in range(N_HEADS):
                zz = z_ref[pl.ds(r0, c), h * dk:(h + 1) * dk]
                o_ref[pl.ds(r0, c), h * dk:(h + 1) * dk] = _rms(os_[h], nw) * _silu(zz)
        return carry

    lax.fori_loop(0, nch // unroll, body, 0)

    @pl.when(gi == pl.num_programs(1) - 1)
    def _():
        convo_ref[0] = xbuf[0:8, :]
        so_ref[0] = s_scr[...]


def _gdn(qkvz, gates, conv0, s0, s_idx, conv_w, hp, nw, *, bn, c, nch, tv):
    n = qkvz.shape[0]
    dk = s0.shape[-1]
    cc = 3 * N_HEADS * dk
    tg = c * nch
    steps = n // (bn * tg)
    kern = functools.partial(_gdn_kernel, c=c, nch=nch, tv=tv, dk=dk, unroll=math.gcd(nch, 2))
    return pl.pallas_call(
        kern, grid=(bn, steps),
        in_specs=[pl.BlockSpec((tg, cc), lambda b, g: (b * steps + g, 0)),
                  pl.BlockSpec((tg, N_HEADS * dk), lambda b, g: (b * steps + g, 3)),
                  pl.BlockSpec((tg, 128), lambda b, g: (b * steps + g, 0)),
                  pl.BlockSpec((1, 8, cc), lambda b, g: (b, 0, 0)),
                  pl.BlockSpec((None, 1, N_HEADS, dk, dk), lambda b, g: (s_idx, b, 0, 0, 0)),
                  pl.BlockSpec((CONV_W, cc), lambda b, g: (0, 0)),
                  pl.BlockSpec((8, 128), lambda b, g: (0, 0)),
                  pl.BlockSpec((1, dk), lambda b, g: (0, 0))],
        out_specs=[pl.BlockSpec((tg, N_HEADS * dk), lambda b, g: (b * steps + g, 0)),
                   pl.BlockSpec((1, 8, cc), lambda b, g: (b, 0, 0)),
                   pl.BlockSpec((1, N_HEADS, dk, dk), lambda b, g: (b, 0, 0, 0))],
        out_shape=[jax.ShapeDtypeStruct((n, N_HEADS * dk), F32),
                   jax.ShapeDtypeStruct((bn, 8, cc), F32),
                   jax.ShapeDtypeStruct((bn, N_HEADS, dk, dk), F32)],
        scratch_shapes=[pltpu.VMEM((8 + tg, cc), F32), pltpu.VMEM((tg, cc), F32),
                        pltpu.VMEM((tg, 128), F32), pltpu.VMEM((tg, 128), F32),
                        pltpu.VMEM((N_HEADS, dk, dk), F32)],
        compiler_params=_cparams(("parallel", "arbitrary")), name="gated_delta")(
            qkvz, qkvz, gates, conv0, s0, conv_w, hp, nw.reshape(1, dk))


def _mlstm_kernel(q_ref, k_ref, v_ref, z_ref, og_ref, gt_ref, c0_ref, n0_ref, m0_ref, hp_ref, nw_ref,
                  o_ref, co_ref, no_ref, mo_ref,
                  li_scr, lf_scr, cext, m_scr, *, c, nch, tv, dqk, dv, unroll):
    gi = pl.program_id(1)
    tg = c * nch
    lane1 = _iota((dqk, 128), 1)

    @pl.when(gi == 0)
    def _():
        for h in range(N_HEADS):
            cext[h, :, 0:dv] = c0_ref[0, h]
            cext[h, :, dv:dv + 128] = jnp.where(lane1 == 0, n0_ref[0, h], 0.0)
        m_scr[...] = m0_ref[0]

    gt = gt_ref[...]
    hp = hp_ref[...]
    lane = _iota((tg, 128), 1)
    li = gt + hp[0:1, :]
    x = gt + hp[1:2, :]
    lf = jnp.minimum(x, 0.0) - jnp.log(1.0 + jnp.exp(-jnp.abs(x)))
    lf = jnp.where(lane >= N_HEADS, jnp.where(lane < 2 * N_HEADS, lf, 0.0), 0.0)
    if tv < c:
        valid = (_iota((tg, 128), 0) % c) < tv
        lf = jnp.where(valid, lf, 0.0)
        li = jnp.where(valid, li, NEG)
    li_scr[...] = li
    lf_scr[...] = lf

    row = _iota((c, c), 0)
    col = _iota((c, c), 1)
    ige = row >= col
    tri = jnp.where(ige, 1.0, 0.0).astype(BF16)
    ones_b = jnp.ones((c, 128), BF16)
    lane_c = _iota((c, 128), 1)
    one_col = jnp.where(lane_c == 0, 1.0, 0.0)
    nw = nw_ref[...]

    def prep(starts):
        items = [(ci, h) for ci in range(len(starts)) for h in range(N_HEADS)]
        bcums = [_exact_dot(tri, lf_scr[pl.ds(r0, c), :]) for r0 in starts]
        lichs = [li_scr[pl.ds(r0, c), :] for r0 in starts]
        qbs = [(q_ref[pl.ds(starts[ci], c), h * dqk:(h + 1) * dqk] * (dqk ** -0.5)).astype(BF16)
               for ci, h in items]
        ks = [k_ref[pl.ds(starts[ci], c), h * dqk:(h + 1) * dqk] for ci, h in items]
        qk_raws = [_dot_nt(qb, k.astype(BF16)) for qb, k in zip(qbs, ks)]
        bcols = [bcums[ci][:, N_HEADS + h:N_HEADS + h + 1] for ci, h in items]
        licols = [lichs[ci][:, h:h + 1] for ci, h in items]
        rowvs = [_exact_dot_nt(ones_b, jnp.where(lane_c == 0, licol - bcol, 0.0))
                 for licol, bcol in zip(licols, bcols)]
        out = []
        for i, (ci, h) in enumerate(items):
            dmat = jnp.where(ige, bcols[i] + rowvs[i], NEG)
            v = v_ref[pl.ds(starts[ci], c), h * dv:(h + 1) * dv]
            blast = bcums[ci][c - 1:c, N_HEADS + h:N_HEADS + h + 1]
            out.append((qbs[i], ks[i], jnp.concatenate([v, one_col], axis=1).astype(BF16), bcols[i], dmat,
                        jnp.max(dmat, axis=-1, keepdims=True), qk_raws[i], blast, blast - bcols[i] + licols[i]))
        return [out[ci * N_HEADS:(ci + 1) * N_HEADS] for ci in range(len(starts))]

    def body(it, carry):
        starts = [pl.multiple_of((it * unroll + i) * c, c) for i in range(unroll)]
        hs = range(N_HEADS)
        for r0, heads in zip(starts, prep(starts)):
            mprevs = [m_scr[0:1, h:h + 1] for h in hs]
            c_olds = [cext[h] for h in hs]
            qcs = [_dot(heads[h][0], c_olds[h].astype(BF16)) for h in hs]
            inters = [heads[h][3] + mprevs[h] for h in hs]
            mts = [jnp.maximum(inters[h], heads[h][5]) for h in hs]
            mnews = [mts[h][c - 1:c, :] for h in hs]
            for h in hs:
                k, vext, blast, wk_log = heads[h][1], heads[h][2], heads[h][7], heads[h][8]
                cext[h] = (c_olds[h] * jnp.exp(blast + mprevs[h] - mnews[h])
                           + _dot_tn((k * jnp.exp(wk_log - mnews[h])).astype(BF16), vext))
                m_scr[0:1, h:h + 1] = mnews[h]
            for h in hs:
                qk = heads[h][6] * jnp.exp(heads[h][4] - mts[h])
                tot = jnp.exp(inters[h] - mts[h]) * qcs[h] + _dot(qk.astype(BF16), heads[h][2])
                hh = tot[:, :dv] / jnp.maximum(jnp.abs(tot[:, dv:dv + 1]), jnp.exp(-mts[h]))
                og = og_ref[pl.ds(r0, c), h * dv:(h + 1) * dv]
                zz = z_ref[pl.ds(r0, c), h * dv:(h + 1) * dv]
                o_ref[pl.ds(r0, c), h * dv:(h + 1) * dv] = _rms(_sigmoid(og) * hh, nw) * _silu(zz)
        return carry

    lax.fori_loop(0, nch // unroll, body, 0)

    @pl.when(gi == pl.num_programs(1) - 1)
    def _():
        for h in range(N_HEADS):
            co_ref[0, h] = cext[h, :, 0:dv]
            no_ref[0, h] = cext[h, :, dv:dv + 1]
        mo_ref[0] = m_scr[...]


def _mlstm(q, k, v, z, og, gates, c0, c_idx, n0, m0, hp, nw, *, bn, c, nch, tv):
    n = q.shape[0]
    dqk, dv = c0.shape[-2], c0.shape[-1]
    tg = c * nch
    steps = n // (bn * tg)
    hq, hv = N_HEADS * dqk, N_HEADS * dv
    row = lambda b, g: (b * steps + g, 0)
    kern = functools.partial(_mlstm_kernel, c=c, nch=nch, tv=tv, dqk=dqk, dv=dv, unroll=math.gcd(nch, 2))
    return pl.pallas_call(
        kern, grid=(bn, steps),
        in_specs=[pl.BlockSpec((tg, hq), row), pl.BlockSpec((tg, hq), row),
                  pl.BlockSpec((tg, hv), row), pl.BlockSpec((tg, hv), row),
                  pl.BlockSpec((tg, hv), row), pl.BlockSpec((tg, 128), row),
                  pl.BlockSpec((None, 1, N_HEADS, dqk, dv), lambda b, g: (c_idx, b, 0, 0, 0)),
                  pl.BlockSpec((1, N_HEADS, dqk, 1), lambda b, g: (b, 0, 0, 0)),
                  pl.BlockSpec((1, 8, 128), lambda b, g: (b, 0, 0)),
                  pl.BlockSpec((8, 128), lambda b, g: (0, 0)),
                  pl.BlockSpec((1, dv), lambda b, g: (0, 0))],
        out_specs=[pl.BlockSpec((tg, hv), row),
                   pl.BlockSpec((1, N_HEADS, dqk, dv), lambda b, g: (b, 0, 0, 0)),
                   pl.BlockSpec((1, N_HEADS, dqk, 1), lambda b, g: (b, 0, 0, 0)),
                   pl.BlockSpec((1, 8, 128), lambda b, g: (b, 0, 0))],
        out_shape=[jax.ShapeDtypeStruct((n, hv), F32),
                   jax.ShapeDtypeStruct((bn, N_HEADS, dqk, dv), F32),
                   jax.ShapeDtypeStruct((bn, N_HEADS, dqk, 1), F32),
                   jax.ShapeDtypeStruct((bn, 8, 128), F32)],
        scratch_shapes=[pltpu.VMEM((tg, 128), F32), pltpu.VMEM((tg, 128), F32),
                        pltpu.VMEM((N_HEADS, dqk, dv + 128), F32), pltpu.VMEM((8, 128), F32)],
        compiler_params=_cparams(("parallel", "arbitrary")), name="mlstm")(
            q, k, v, z, og, gates, c0, n0, m0, hp, nw.reshape(1, dv))


def _lambda(lq, lam_init):
    a = jnp.sum(lq[0:1, :] * lq[1:2, :], axis=-1, keepdims=True)
    b = jnp.sum(lq[2:3, :] * lq[3:4, :], axis=-1, keepdims=True)
    return jnp.exp(a) - jnp.exp(b) + lam_init


def _attn_kernel(wq_ref, k_ref, vt_ref, z_ref, lq_ref, nw_ref, o_ref, acc_scr, *, t, hpb, lam_init):
    qi = pl.program_id(2)
    dh = k_ref.shape[1] // hpb
    nl = 4 * t
    acc_scr[...] = jnp.zeros(acc_scr.shape, F32)

    def step(jp, carry, masked):
        rows = [pl.multiple_of((2 * jp + i) * t, t) for i in range(2)]
        wqs = [jnp.concatenate([wq_ref[0, a], wq_ref[1, a]], axis=1) for a in range(hpb)]
        scores = [[_dot(k_ref[pl.ds(rows[i], t), a * dh:(a + 1) * dh], wqs[a]) for i in range(2)]
                  for a in range(hpb)]
        out = []
        for a in range(hpb):
            m_prev, l_prev = carry[a]
            ss = scores[a]
            if masked:
                lane = _iota((t, nl), 1)
                qpos = (lane // (2 * t)) * t + lane % t
                kpos = _iota((t, nl), 0)
                ss = [jnp.where(kpos + i * t <= qpos, ss[i], NEG) for i in range(2)]
            m_new = jnp.maximum(m_prev, jnp.maximum(jnp.max(ss[0], axis=0, keepdims=True),
                                                    jnp.max(ss[1], axis=0, keepdims=True)))
            alpha = jnp.exp2(m_prev - m_new)
            ps = [jnp.exp2(s - m_new) for s in ss]
            l_new = alpha * l_prev + (jnp.sum(ps[0], axis=0, keepdims=True) + jnp.sum(ps[1], axis=0, keepdims=True))
            pv = (_dot(vt_ref[2 * jp, a * dh:(a + 1) * dh, :], ps[0].astype(BF16))
                  + _dot(vt_ref[2 * jp + 1, a * dh:(a + 1) * dh, :], ps[1].astype(BF16)))
            acc_scr[a] = alpha * acc_scr[a] + pv
            out.append((m_new, l_new))
        return tuple(out)

    init = tuple((jnp.full((1, nl), NEG, F32), jnp.zeros((1, nl), F32)) for _ in range(hpb))
    carry = lax.fori_loop(0, qi, lambda j, c: step(j, c, False), init)
    carry = step(qi, carry, True)

    lam = _lambda(lq_ref[...], lam_init)
    nw = nw_ref[...]
    for a in range(hpb):
        acc = acc_scr[a] * (1.0 / carry[a][1])
        for qb in range(2):
            b0 = 2 * t * qb
            o = (acc[:, b0:b0 + t] - lam * acc[:, b0 + t:b0 + 2 * t]).T
            zz = z_ref[qb * t:(qb + 1) * t, a * dh:(a + 1) * dh]
            o_ref[qb * t:(qb + 1) * t, a * dh:(a + 1) * dh] = _rms(o, nw) * (1.0 - lam_init) * _silu(zz)


def _attn_prompt(wq, k16, vt, z, lq, nw, *, bn, seq, lam_init):
    n = k16.shape[0]
    t = vt.shape[2]
    dh = wq.shape[2]
    hpb = 2
    nq = seq // t
    nq2 = nq // 2
    kern = functools.partial(_attn_kernel, t=t, hpb=hpb, lam_init=lam_init)
    return pl.pallas_call(
        kern, grid=(bn, N_HEADS // hpb, nq2),
        in_specs=[pl.BlockSpec((2, hpb, dh, 2 * t), lambda b, h, i: (b * nq2 + i, h, 0, 0)),
                  pl.BlockSpec((seq, hpb * dh), lambda b, h, i: (b, h)),
                  pl.BlockSpec((nq, hpb * dh, t), lambda b, h, i: (b, h, 0)),
                  pl.BlockSpec((2 * t, hpb * dh), lambda b, h, i: (b * nq2 + i, h)),
                  pl.BlockSpec(lq.shape, lambda b, h, i: (0, 0)),
                  pl.BlockSpec((1, dh), lambda b, h, i: (0, 0))],
        out_specs=pl.BlockSpec((2 * t, hpb * dh), lambda b, h, i: (b * nq2 + i, h)),
        out_shape=jax.ShapeDtypeStruct((n, N_HEADS * dh), F32),
        scratch_shapes=[pltpu.VMEM((hpb, dh, 4 * t), F32)],
        compiler_params=_cparams(("parallel", "parallel", "arbitrary")), name="diff_attn_prompt")(
            wq, k16, vt, z, lq, nw.reshape(1, dh))


def _attn_s_kernel(pt_ref, q_ref, kn_ref, vn_ref, z_ref, lq_ref, nw_ref, *rest, pg, tv, lam_init):
    k_pages = rest[:pg]
    v_pages = rest[pg:2 * pg]
    o_ref, qbd, bias, m_scr, l_scr, acc_scr = rest[2 * pg:]
    j = pl.program_id(1)
    tp, hd = q_ref.shape
    dh = hd // N_HEADS
    dqk = dh // 2
    prow = k_pages[0].shape[0]
    rph = 2 * tp

    @pl.when(j == 0)
    def _():
        q = q_ref[...] * (dqk ** -0.5)
        lane = _iota((tp, dh), 1)
        for h in range(N_HEADS):
            qh = q[:, h * dh:(h + 1) * dh]
            qbd[h * rph:h * rph + tp, :] = jnp.where(lane < dqk, qh, 0.0)
            qbd[h * rph + tp:(h + 1) * rph, :] = jnp.where(lane >= dqk, qh, 0.0)
        rhead = _iota(bias.shape, 0) // rph
        chead = _iota(bias.shape, 1) % N_HEADS
        bias[...] = jnp.where(rhead == chead, 0.0, NEG)
        m_scr[...] = jnp.full(m_scr.shape, NEG, F32)
        l_scr[...] = jnp.zeros(l_scr.shape, F32)
        acc_scr[...] = jnp.zeros(acc_scr.shape, F32)

    def update(s, pv):
        m_prev = m_scr[...]
        m_new = jnp.maximum(m_prev, jnp.max(s, axis=-1, keepdims=True))
        alpha = jnp.exp(m_prev - m_new)
        p = jnp.exp(s - m_new[:, 0:1])
        l_scr[...] = alpha * l_scr[...] + jnp.sum(p, axis=-1, keepdims=True)
        acc_scr[...] = alpha * acc_scr[...] + pv(p)
        m_scr[...] = m_new

    qb = qbd[...].astype(BF16)
    bs_ = bias[...]
    s = jnp.concatenate([_dot_nt(qb, kp[...].astype(BF16)) + bs_ for kp in k_pages], axis=1)

    def pv_pages(p):
        pb = p.astype(BF16)
        acc = _dot(pb[:, 0:prow], v_pages[0][...].astype(BF16))
        for i in range(1, pg):
            acc = acc + _dot(pb[:, i * prow:(i + 1) * prow], v_pages[i][...].astype(BF16))
        return acc

    update(s, pv_pages)

    @pl.when(j == pl.num_programs(1) - 1)
    def _():
        qf = qbd[...]
        kn = kn_ref[...]
        vn = vn_ref[...]
        sn = jnp.concatenate([_dot_nt(qf[h * rph:(h + 1) * rph], kn[:, h * dh:(h + 1) * dh])
                              for h in range(N_HEADS)], axis=0)
        tq = _iota(sn.shape, 0) % tp
        tk = _iota(sn.shape, 1)
        sn = jnp.where(tk <= tq, jnp.where(tk < tv, sn, NEG), NEG)
        update(sn, lambda p: jnp.concatenate(
            [_dot(p[h * rph:(h + 1) * rph], vn[:, h * dh:(h + 1) * dh]) for h in range(N_HEADS)], axis=0))
        acc = acc_scr[...] / l_scr[...]
        lam = _lambda(lq_ref[...], lam_init)
        nw = nw_ref[...]
        normed = jnp.concatenate(
            [_rms(acc[h * rph:h * rph + tp] - lam * acc[h * rph + tp:(h + 1) * rph], nw) for h in range(N_HEADS)],
            axis=1)
        o_ref[...] = normed * (1.0 - lam_init) * _silu(z_ref[...])


def _attn_sample(page_table, q, kn, vn, z, lq, nw, cache_k, cache_v, *, e, tv, lam_init):
    bs, npages = page_table.shape
    n, hd = q.shape
    tp = n // bs
    dh = hd // N_HEADS
    pg = PAGES_PER_STEP
    prow = cache_k.shape[2]
    steps = npages // pg
    rows = 2 * N_HEADS * tp
    tok = lambda b, j, pt: (b, 0)
    page_specs = [pl.BlockSpec((None, None, prow, dh),
                               lambda b, j, pt, i=i: (e, pt[b, j * pg + i], 0, 0)) for i in range(pg)]
    kern = functools.partial(_attn_s_kernel, pg=pg, tv=tv, lam_init=lam_init)
    grid_spec = pltpu.PrefetchScalarGridSpec(
        num_scalar_prefetch=1, grid=(bs, steps),
        in_specs=[pl.BlockSpec((tp, hd), tok), pl.BlockSpec((tp, hd), tok), pl.BlockSpec((tp, hd), tok),
                  pl.BlockSpec((tp, hd), tok),
                  pl.BlockSpec(lq.shape, lambda b, j, pt: (0, 0)),
                  pl.BlockSpec((1, dh), lambda b, j, pt: (0, 0))] + page_specs + page_specs,
        out_specs=pl.BlockSpec((tp, hd), tok),
        scratch_shapes=[pltpu.VMEM((rows, dh), F32), pltpu.VMEM((rows, prow), F32), pltpu.VMEM((rows, 128), F32),
                        pltpu.VMEM((rows, 128), F32), pltpu.VMEM((rows, dh), F32)])
    return pl.pallas_call(
        kern, grid_spec=grid_spec, out_shape=jax.ShapeDtypeStruct((n, hd), F32),
        compiler_params=_cparams(("parallel", "arbitrary")), name="diff_attn_sample")(
            page_table, q, kn, vn, z, lq, nw.reshape(1, dh), *([cache_k] * pg), *([cache_v] * pg))


def _rope_tables(pos, dqk):
    rope_dim = dqk // 4
    half = rope_dim // 2
    inv = jnp.power(ROPE_THETA, -jnp.arange(half, dtype=F32) / half)
    ang = pos.astype(F32)[:, None] * inv[None, :]
    lane = jnp.arange(128) % dqk
    idx = lane % half
    cos = jnp.where(lane < rope_dim, jnp.cos(ang)[:, idx], 1.0)
    sin = jnp.sin(ang)[:, idx]
    sa = jnp.where((lane >= half) & (lane < rope_dim), sin, 0.0)
    sb = jnp.where(lane < half, -sin, 0.0)
    return cos.astype(F32), sa.astype(F32), sb.astype(F32)


def _lane_row(vals, offset):
    out = jnp.zeros((128,), F32)
    return out.at[offset:offset + vals.shape[0]].set(vals.astype(F32))


def kernel(x_prompt, x_sample, cache_k, cache_v, page_table, state_gdn_conv, state_gdn_s, state_mlstm_c,
           state_mlstm_n, state_mlstm_m, norm_w, final_norm_w, w_in_even, w_out_even, conv_w, a_log, dt_bias,
           gdn_norm_w, lambda_qk, diff_norm_w, w_in_odd, w_out_odd, b_i, b_f, mlstm_norm_w):
    bp, seq, d = x_prompt.shape
    bs, ts, _ = x_sample.shape
    depth = norm_w.shape[0]
    tp = SAMPLE_PAD
    dk_a = state_gdn_s.shape[-1]
    conv_ch = state_gdn_conv.shape[-1]
    w_a = N_HEADS * dk_a
    dh_b = cache_v.shape[-1]
    dqk_b = dh_b // 2
    w_b = N_HEADS * dh_b
    dqk_c, dv_c = state_mlstm_c.shape[-2], state_mlstm_c.shape[-1]
    hq_c, w_c = N_HEADS * dqk_c, N_HEADS * dv_c
    past = page_table.shape[1] * cache_k.shape[2]
    n_pool, page = cache_k.shape[1], cache_k.shape[2]

    xp = x_prompt.reshape(bp * seq, d)
    xs = jnp.pad(x_sample, ((0, 0), (0, tp - ts), (0, 0))).reshape(bs * tp, d)
    tm_p = math.gcd(seq, 256)
    tm_s = math.gcd(bs * tp, 256)
    nch_p = math.gcd(seq // CHUNK, 4)

    rope_p = _rope_tables(jnp.arange(seq), dqk_b)
    rope_s = _rope_tables(past + (jnp.arange(tm_s) % tp), dqk_b)
    half = dqk_b // 8
    ang_t = (jnp.power(ROPE_THETA, -jnp.arange(half, dtype=F32) / half)[:, None]
             * jnp.arange(seq).astype(F32)[None, :])
    cos_t, sin_t = jnp.cos(ang_t), jnp.sin(ang_t)

    even_w = conv_ch + w_a
    seg_even = [(0, even_w), (even_w, w_b), (even_w + w_b, w_b), (even_w + 2 * w_b, w_b),
                (even_w + 3 * w_b, w_b), (even_w + 4 * w_b, 128)]
    seg_even_p = [seg_even[0]] + seg_even[2:]
    rope_even = tuple(range(even_w // 128, (even_w + 2 * w_b) // 128))
    odd_w = 2 * hq_c + 3 * w_c
    seg_odd = [(0, hq_c), (hq_c, hq_c), (2 * hq_c, w_c), (2 * hq_c + w_c, w_c), (2 * hq_c + 2 * w_c, w_c),
               (odd_w, 128)]

    ck = cache_k.reshape(cache_k.shape[0], n_pool, page * N_HEADS, dh_b)
    cv = cache_v.reshape(cache_v.shape[0], n_pool, page * N_HEADS, dh_b)

    k_rows_p, v_rows_p, k_rows_s, v_rows_s = [], [], [], []
    conv_p, s_p, conv_s, s_s = [], [], [], []
    c_p, n_p, m_p, c_s, n_s, m_s = [], [], [], [], [], []

    for layer in range(depth):
        last = layer == depth - 1
        if layer % 2 == 0:
            e = layer // 2
            lam_init = 0.8 - 0.6 * math.exp(-0.3 * layer)
            w = w_in_even[e]
            g0 = even_w
            w_r = jnp.concatenate([w[:, :g0], w[:, g0 + 2 * N_HEADS:], w[:, g0:g0 + 2 * N_HEADS],
                                   jnp.zeros((d, 128 - 2 * N_HEADS), F32)], axis=1).astype(BF16)
            w_o = w_out_even[e].astype(BF16)
            hp = jnp.zeros((8, 128), F32).at[0].set(_lane_row(a_log[e], N_HEADS)).at[1].set(
                _lane_row(dt_bias[e], N_HEADS))
            lq = lambda_qk[e]

            wq_t = w[:, g0 + 2 * N_HEADS:g0 + 2 * N_HEADS + w_b].T.astype(BF16)
            wv_t = w[:, g0 + 2 * N_HEADS + 2 * w_b:g0 + 2 * N_HEADS + 3 * w_b].T.astype(BF16)
            az, kb, vb, zb, gt, k16, wq, vt = _proj(xp, norm_w[layer], w_r, seg_even_p, (rope_even, rope_p), tm_p,
                                                    k16_seg=1, feat=(wq_t, wv_t, cos_t, sin_t))
            oa, cst, sst = _gdn(az, gt, jnp.zeros((bp, 8, conv_ch), F32),
                                jnp.zeros((1, bp, N_HEADS, dk_a, dk_a), F32), 0,
                                conv_w[e], hp, gdn_norm_w[e], bn=bp, c=CHUNK, nch=nch_p, tv=CHUNK)
            ob = _attn_prompt(wq, k16, vt, zb, lq, diff_norm_w[e], bn=bp, seq=seq, lam_init=lam_init)
            xp = _out_proj(oa, ob, w_a, 0, w_o, xp, final_norm_w, last, tm_p)
            k_rows_p.append(kb.reshape(bp, seq, N_HEADS, dh_b))
            v_rows_p.append(vb.reshape(bp, seq, N_HEADS, dh_b))
            conv_p.append(cst[:, 8 - (CONV_W - 1):])
            s_p.append(sst)

            az, qb, kb, vb, zb, gt = _proj(xs, norm_w[layer], w_r, seg_even, (rope_even, rope_s), tm_s)
            conv0 = jnp.pad(state_gdn_conv[e], ((0, 0), (8 - (CONV_W - 1), 0), (0, 0)))
            oa, cst, sst = _gdn(az, gt, conv0, state_gdn_s, e, conv_w[e], hp, gdn_norm_w[e],
                                bn=bs, c=tp, nch=1, tv=ts)
            ob = _attn_sample(page_table, qb, kb, vb, zb, lq, diff_norm_w[e], ck, cv, e=e, tv=ts,
                              lam_init=lam_init)
            xs = _out_proj(oa, ob, w_a, 0, w_o, xs, final_norm_w, last, tm_s)
            k_rows_s.append(kb.reshape(bs, tp, N_HEADS, dh_b)[:, :ts])
            v_rows_s.append(vb.reshape(bs, tp, N_HEADS, dh_b)[:, :ts])
            conv_s.append(cst[:, 8 - (CONV_W - 1):])
            s_s.append(sst)
        else:
            o = layer // 2
            w_r = jnp.concatenate([w_in_odd[o], jnp.zeros((d, 128 - 2 * N_HEADS), F32)], axis=1).astype(BF16)
            w_o = w_out_odd[o].astype(BF16)
            hp = jnp.zeros((8, 128), F32).at[0].set(_lane_row(b_i[o], 0)).at[1].set(_lane_row(b_f[o], N_HEADS))

            q, k, v, z, og, gt = _proj(xp, norm_w[layer], w_r, seg_odd, None, tm_p)
            hm, cf, nf, mf = _mlstm(q, k, v, z, og, gt, jnp.zeros((1, bp, N_HEADS, dqk_c, dv_c), F32), 0,
                                    jnp.zeros((bp, N_HEADS, dqk_c, 1), F32), jnp.zeros((bp, 8, 128), F32),
                                    hp, mlstm_norm_w[o], bn=bp, c=CHUNK, nch=nch_p, tv=CHUNK)
            xp = _out_proj(hm, hm, w_c // 2, 1, w_o, xp, final_norm_w, last, tm_p)
            c_p.append(cf)
            n_p.append(nf[..., 0])
            m_p.append(mf[:, 0, :N_HEADS])

            q, k, v, z, og, gt = _proj(xs, norm_w[layer], w_r, seg_odd, None, tm_s)
            m0 = jnp.zeros((bs, 8, 128), F32).at[:, 0, :N_HEADS].set(state_mlstm_m[o])
            hm, cf, nf, mf = _mlstm(q, k, v, z, og, gt, state_mlstm_c, o, state_mlstm_n[o][..., None], m0,
                                    hp, mlstm_norm_w[o], bn=bs, c=tp, nch=1, tv=ts)
            xs = _out_proj(hm, hm, w_c // 2, 1, w_o, xs, final_norm_w, last, tm_s)
            c_s.append(cf)
            n_s.append(nf[..., 0])
            m_s.append(mf[:, 0, :N_HEADS])

    y_prompt = xp.reshape(bp, seq, d)
    y_sample = xs.reshape(bs, tp, d)[:, :ts]
    st = jnp.stack
    return (y_prompt, y_sample, st(k_rows_p), st(v_rows_p), st(k_rows_s), st(v_rows_s),
            st(conv_p), st(s_p), st(conv_s), st(s_s),
            st(c_p), st(n_p), st(m_p), st(c_s), st(n_s), st(m_s))
```

```python
import functools
import math

import jax
import jax.numpy as jnp
from jax import lax
from jax.experimental import pallas as pl
from jax.experimental.pallas import tpu as pltpu

F32 = jnp.float32
BF16 = jnp.bfloat16

EPS = 1e-6
NEG = -1e30
N_HEADS = 4
CONV_W = 4
ROPE_THETA = 500000.0
CHUNK = 64
SAMPLE_PAD = 8
PAGES_PER_STEP = 16
V7X_VMEM_LIMIT = 56 * 1024 * 1024


def _cparams(sem):
    return pltpu.CompilerParams(dimension_semantics=sem, vmem_limit_bytes=V7X_VMEM_LIMIT)


def _dot(a, b):
    return jnp.dot(a, b, preferred_element_type=F32)


def _dot_nt(a, b):
    return lax.dot_general(a, b, (((1,), (1,)), ((), ())), preferred_element_type=F32)


def _dot_tn(a, b):
    return lax.dot_general(a, b, (((0,), (0,)), ((), ())), preferred_element_type=F32)


def _split3(x):
    hi = x.astype(BF16)
    r = x - hi.astype(F32)
    mid = r.astype(BF16)
    lo = (r - mid.astype(F32)).astype(BF16)
    return hi, mid, lo


def _exact_dot(a_bf16, x):
    hi, mid, lo = _split3(x)
    return _dot(a_bf16, hi) + (_dot(a_bf16, mid) + _dot(a_bf16, lo))


def _exact_dot_nt(a_bf16, x):
    hi, mid, lo = _split3(x)
    return _dot_nt(a_bf16, hi) + (_dot_nt(a_bf16, mid) + _dot_nt(a_bf16, lo))


def _sigmoid(x):
    return 1.0 / (1.0 + jnp.exp(-x))


def _silu(x):
    return x * _sigmoid(x)


def _softplus(x):
    return jnp.maximum(x, 0.0) + jnp.log(1.0 + jnp.exp(-jnp.abs(x)))


def _rms(x, w):
    return x * lax.rsqrt(jnp.mean(x * x, axis=-1, keepdims=True) + EPS) * w


def _iota(shape, dim):
    return lax.broadcasted_iota(jnp.int32, shape, dim)


def _proj_kernel(*refs, segs, rope_blocks, k16_seg, feat_major, row_segs):
    n_out = len(segs)
    pos = 3
    x_ref, nw_ref, w_ref = refs[:3]
    if rope_blocks:
        cos_ref, sa_ref, sb_ref = refs[pos:pos + 3]
        pos += 3
    if feat_major:
        wqt_ref, wvt_ref, cost_ref, sint_ref = refs[pos:pos + 4]
        pos += 4
    pos += len(row_segs)
    outs = refs[pos:pos + n_out]
    pos += n_out
    x = x_ref[...]
    h = _rms(x, nw_ref[...]).astype(BF16)
    for si, ((c0, width), o_ref) in enumerate(zip(segs, outs)):
        for s0 in range(0, width, 512):
            sw = min(512, width - s0)
            acc = _dot(h, w_ref[:, c0 + s0:c0 + s0 + sw])
            if rope_blocks and (c0 + s0) // 128 in rope_blocks:
                cos = cos_ref[...]
                sa = sa_ref[...]
                sb = sb_ref[...]
                parts = []
                for t0 in range(0, sw, 128):
                    a = acc[:, t0:t0 + 128]
                    parts.append(a * cos + pltpu.roll(a, 8, 1) * sa + pltpu.roll(a, 120, 1) * sb)
                acc = jnp.concatenate(parts, axis=1)
            if si in row_segs:
                dh = sw // N_HEADS
                for hd in range(N_HEADS):
                    o_ref[pl.ds(hd, acc.shape[0], stride=N_HEADS), :] = acc[:, hd * dh:(hd + 1) * dh]
            else:
                o_ref[:, s0:s0 + sw] = acc
            if si == k16_seg:
                refs[pos][:, s0:s0 + sw] = acc.astype(BF16)
    if feat_major:
        wq_ref, vt_ref = refs[pos + 1], refs[pos + 2]
        tm = x.shape[0]
        vt_ref[0] = _dot_nt(wvt_ref[...], h).astype(BF16)
        qt = _dot_nt(wqt_ref[...], h)
        dqk = qt.shape[0] // (2 * N_HEADS)
        half = dqk // 8
        cos = cost_ref[...]
        sin = sint_ref[...]
        scale = (dqk ** -0.5) * math.log2(math.e)
        zero = jnp.zeros((dqk, tm), F32)
        for hd in range(N_HEADS):
            maps = []
            for m in range(2):
                r0 = (2 * hd + m) * dqk
                x1 = qt[r0:r0 + half]
                x2 = qt[r0 + half:r0 + 2 * half]
                maps.append(jnp.concatenate([x1 * cos - x2 * sin, x2 * cos + x1 * sin,
                                             qt[r0 + 2 * half:r0 + dqk]], axis=0) * scale)
            top = jnp.concatenate([maps[0], zero], axis=1)
            bot = jnp.concatenate([zero, maps[1]], axis=1)
            wq_ref[0, hd] = jnp.concatenate([top, bot], axis=0).astype(BF16)


def _proj(x, nw, w, segs, rope, tm, k16_seg=None, feat=None, rows=None):
    n, d = x.shape
    ncols = w.shape[1]
    grid = (n // tm,)
    in_specs = [pl.BlockSpec((tm, d), lambda i: (i, 0)),
                pl.BlockSpec((1, d), lambda i: (0, 0)),
                pl.BlockSpec((d, ncols), lambda i: (0, 0))]
    args = [x, nw.reshape(1, d), w]
    rope_blocks = ()
    if rope is not None:
        rope_blocks, tables = rope
        nt = tables[0].shape[0] // tm
        for t in tables:
            in_specs.append(pl.BlockSpec((tm, 128), lambda i, nt=nt: (i % nt, 0)))
            args.append(t)
    out_shape = [jax.ShapeDtypeStruct((n, width), F32) for _, width in segs]
    out_specs = [pl.BlockSpec((tm, width), lambda i: (i, 0)) for _, width in segs]
    if feat is not None:
        wqt, wvt, cost, sint = feat
        ntt = cost.shape[1] // tm
        in_specs += [pl.BlockSpec(wqt.shape, lambda i: (0, 0)), pl.BlockSpec(wvt.shape, lambda i: (0, 0)),
                     pl.BlockSpec((cost.shape[0], tm), lambda i, ntt=ntt: (0, i % ntt)),
                     pl.BlockSpec((sint.shape[0], tm), lambda i, ntt=ntt: (0, i % ntt))]
        args += [wqt, wvt, cost, sint]
    row_segs, aliases = (), {}
    if rows is not None:
        row_segs, layer, stacked = rows
        for si, arr in zip(row_segs, stacked):
            aliases[len(args)] = si
            in_specs.append(pl.BlockSpec(memory_space=pl.ANY))
            args.append(arr)
            out_shape[si] = jax.ShapeDtypeStruct(arr.shape, F32)
            out_specs[si] = pl.BlockSpec((None, tm * N_HEADS, arr.shape[2]), lambda i, layer=layer: (layer, i, 0))
    if k16_seg is not None:
        kw = segs[k16_seg][1]
        out_shape.append(jax.ShapeDtypeStruct((n, kw), BF16))
        out_specs.append(pl.BlockSpec((tm, kw), lambda i: (i, 0)))
    if feat is not None:
        hq, hv = wqt.shape[0], wvt.shape[0]
        dh = hq // N_HEADS
        out_shape += [jax.ShapeDtypeStruct((n // tm, N_HEADS, dh, 2 * tm), BF16),
                      jax.ShapeDtypeStruct((n // tm, hv, tm), BF16)]
        out_specs += [pl.BlockSpec((1, N_HEADS, dh, 2 * tm), lambda i: (i, 0, 0, 0)),
                      pl.BlockSpec((1, hv, tm), lambda i: (i, 0, 0))]
    return pl.pallas_call(
        functools.partial(_proj_kernel, segs=tuple(segs), rope_blocks=tuple(rope_blocks), k16_seg=k16_seg,
                          feat_major=feat is not None, row_segs=tuple(row_segs)),
        grid=grid, in_specs=in_specs, out_specs=out_specs, out_shape=out_shape, input_output_aliases=aliases,
        compiler_params=_cparams(("parallel",)), name="norm_proj")(*args)


def _out_kernel(a_ref, b_ref, w_ref, x_ref, fw_ref, o_ref, *, final):
    ka = a_ref.shape[1]
    y = _dot(a_ref[...].astype(BF16), w_ref[0:ka, :]) + _dot(b_ref[...].astype(BF16), w_ref[ka:, :])
    xn = x_ref[...] + y
    if final:
        xn = _rms(xn, fw_ref[...])
    o_ref[...] = xn


def _out_proj(a, b, ka, b_col, w, x, fw, final, tm):
    n, d = x.shape
    kb = w.shape[0] - ka
    return pl.pallas_call(
        functools.partial(_out_kernel, final=final),
        grid=(n // tm,),
        in_specs=[pl.BlockSpec((tm, ka), lambda i: (i, 0)),
                  pl.BlockSpec((tm, kb), lambda i: (i, b_col)),
                  pl.BlockSpec(w.shape, lambda i: (0, 0)),
                  pl.BlockSpec((tm, d), lambda i: (i, 0)),
                  pl.BlockSpec((1, d), lambda i: (0, 0))],
        out_specs=pl.BlockSpec((tm, d), lambda i: (i, 0)),
        out_shape=jax.ShapeDtypeStruct((n, d), F32),
        compiler_params=_cparams(("parallel",)), name="out_proj")(a, b, w, x, fw.reshape(1, d))


def _neumann(a_list, c):
    eye = jnp.where(_iota((c, c), 0) == _iota((c, c), 1), 1.0, 0.0)
    ts = [eye - a for a in a_list]
    abs_ = [a.astype(BF16) for a in a_list]
    ms = [_dot(ab, ab) for ab in abs_]
    levels = int(math.log2(c)) - 1
    for k in range(levels):
        mbs = [m.astype(BF16) for m in ms]
        ts = [t + _dot(mb, t.astype(BF16)) for mb, t in zip(mbs, ts)]
        if k + 1 < levels:
            ms = [_dot(mb, mb) for mb in mbs]
    return ts


def _gdn_kernel(x_ref, z_ref, gt_ref, conv0_ref, s0_ref, cw_ref, hp_ref, nw_ref,
                o_ref, convo_ref, so_ref,
                xbuf, qkv_scr, g_scr, b_scr, s_scr, *, c, nch, tv, dk, unroll):
    gi = pl.program_id(1)
    tg = c * nch
    hq = N_HEADS * dk

    @pl.when(gi == 0)
    def _():
        xbuf[0:8, :] = conv0_ref[0]
        s_scr[...] = s0_ref[0]

    xbuf[8:8 + tg, :] = x_ref[...]
    cw = cw_ref[...]
    y = ((xbuf[5:5 + tg, :] * cw[0:1, :] + xbuf[6:6 + tg, :] * cw[1:2, :])
         + (xbuf[7:7 + tg, :] * cw[2:3, :] + xbuf[8:8 + tg, :] * cw[3:4, :]))
    qkv_scr[...] = _silu(y)
    last = tg if tv == c else tv
    hist = xbuf[8 + last - 3:8 + last, :]
    xbuf[5:8, :] = hist

    gt = gt_ref[...]
    hp = hp_ref[...]
    lane = _iota((tg, 128), 1)
    beta = _sigmoid(gt)
    g = -jnp.exp(hp[0:1, :]) * _softplus(gt + hp[1:2, :])
    g = jnp.where(lane >= N_HEADS, jnp.where(lane < 2 * N_HEADS, g, 0.0), 0.0)
    if tv < c:
        valid = (_iota((tg, 128), 0) % c) < tv
        g = jnp.where(valid, g, 0.0)
        beta = jnp.where(valid, beta, 0.0)
    g_scr[...] = g
    b_scr[...] = beta

    row = _iota((c, c), 0)
    col = _iota((c, c), 1)
    ige = row >= col
    igt = row > col
    tri = jnp.where(ige, 1.0, 0.0).astype(BF16)
    lane_c = _iota((c, 128), 1)
    nw = nw_ref[...]

    sels = [jnp.where(lane_c == N_HEADS + h, 1.0, 0.0).astype(BF16) for h in range(N_HEADS)]

    def prep(starts):
        items = [(ci, h) for ci in range(len(starts)) for h in range(N_HEADS)]
        gcums = [_exact_dot(tri, g_scr[pl.ds(r0, c), :]) for r0 in starts]
        bchs = [b_scr[pl.ds(r0, c), :] for r0 in starts]
        qns, kns, vhs = [], [], []
        for ci, h in items:
            r0 = starts[ci]
            qh = qkv_scr[pl.ds(r0, c), h * dk:(h + 1) * dk]
            kh = qkv_scr[pl.ds(r0, c), hq + h * dk:hq + (h + 1) * dk]
            vhs.append(qkv_scr[pl.ds(r0, c), 2 * hq + h * dk:2 * hq + (h + 1) * dk])
            qns.append(qh * lax.rsqrt(jnp.sum(qh * qh, axis=-1, keepdims=True) + EPS) * (dk ** -0.5))
            kns.append(kh * lax.rsqrt(jnp.sum(kh * kh, axis=-1, keepdims=True) + EPS))
        qkks = [_dot_nt(jnp.concatenate([qn, kn], axis=0).astype(BF16), kn.astype(BF16))
                for qn, kn in zip(qns, kns)]
        grows = [_exact_dot_nt(sels[h], gcums[ci]) for ci, h in items]
        gcols = [gcums[ci][:, N_HEADS + h:N_HEADS + h + 1] for ci, h in items]
        bcols = [bchs[ci][:, h:h + 1] for ci, h in items]
        decays = [jnp.where(ige, jnp.exp(jnp.where(ige, gcol - grow, 0.0)), 0.0)
                  for gcol, grow in zip(gcols, grows)]
        t_invs = _neumann([jnp.where(igt, qkk[c:] * decay * bcol, 0.0)
                           for qkk, decay, bcol in zip(qkks, decays, bcols)], c)
        egs = [jnp.exp(gcol) for gcol in gcols]
        sols = [_dot(t_inv.astype(BF16),
                     jnp.concatenate([vh * bcol, kn * (bcol * eg)], axis=1).astype(BF16))
                for t_inv, vh, kn, bcol, eg in zip(t_invs, vhs, kns, bcols, egs)]
        glasts = [gcums[ci][c - 1:c, N_HEADS + h:N_HEADS + h + 1] for ci, h in items]
        kws = [(kn * jnp.exp(glast - gcol)).astype(BF16) for kn, glast, gcol in zip(kns, glasts, gcols)]
        aqks = [(qkk[:c] * decay).astype(BF16) for qkk, decay in zip(qkks, decays)]
        sol_bs = [sol.astype(BF16) for sol in sols]
        ktus = [_dot_tn(kw, sol_b) for kw, sol_b in zip(kws, sol_bs)]
        aus = [_dot(aqk, sol_b) for aqk, sol_b in zip(aqks, sol_bs)]
        out = []
        for i in range(len(items)):
            lhs = jnp.concatenate([-ktus[i][:, dk:], qns[i] * egs[i] - aus[i][:, dk:]], axis=0).astype(BF16)
            out.append((lhs, ktus[i][:, :dk], aus[i][:, :dk], jnp.exp(glasts[i])))
        return [out[ci * N_HEADS:(ci + 1) * N_HEADS] for ci in range(len(starts))]

    def body(it, carry):
        starts = [pl.multiple_of((it * unroll + i) * c, c) for i in range(unroll)]
        for r0, heads in zip(starts, prep(starts)):
            s_olds = [s_scr[h] for h in range(N_HEADS)]
            res = [_dot(heads[h][0], s_olds[h].astype(BF16)) for h in range(N_HEADS)]
            for h in range(N_HEADS):
                s_scr[h] = s_olds[h] * heads[h][3] + (heads[h][1] + res[h][:dk])
            for h in range(N_HEADS):
                zz = z_ref[pl.ds(r0, c), h * dk:(h + 1) * dk]
                o = heads[h][2] + res[h][dk:]
                o_ref[pl.ds(r0, c), h * dk:(h + 1) * dk] = _rms(o, nw) * _silu(zz)
        return carry

    lax.fori_loop(0, nch // unroll, body, 0)

    @pl.when(gi == pl.num_programs(1) - 1)
    def _():
        convo_ref[0] = xbuf[0:8, :]
        so_ref[0] = s_scr[...]


def _gdn(qkvz, gates, conv0, s0, s_idx, conv_w, hp, nw, *, bn, c, nch, tv):
    n = qkvz.shape[0]
    dk = s0.shape[-1]
    cc = 3 * N_HEADS * dk
    tg = c * nch
    steps = n // (bn * tg)
    kern = functools.partial(_gdn_kernel, c=c, nch=nch, tv=tv, dk=dk, unroll=math.gcd(nch, 4))
    return pl.pallas_call(
        kern, grid=(bn, steps),
        in_specs=[pl.BlockSpec((tg, cc), lambda b, g: (b * steps + g, 0)),
                  pl.BlockSpec((tg, N_HEADS * dk), lambda b, g: (b * steps + g, 3)),
                  pl.BlockSpec((tg, 128), lambda b, g: (b * steps + g, 0)),
                  pl.BlockSpec((1, 8, cc), lambda b, g: (b, 0, 0)),
                  pl.BlockSpec((None, 1, N_HEADS, dk, dk), lambda b, g: (s_idx, b, 0, 0, 0)),
                  pl.BlockSpec((CONV_W, cc), lambda b, g: (0, 0)),
                  pl.BlockSpec((8, 128), lambda b, g: (0, 0)),
                  pl.BlockSpec((1, dk), lambda b, g: (0, 0))],
        out_specs=[pl.BlockSpec((tg, N_HEADS * dk), lambda b, g: (b * steps + g, 0)),
                   pl.BlockSpec((1, 8, cc), lambda b, g: (b, 0, 0)),
                   pl.BlockSpec((1, N_HEADS, dk, dk), lambda b, g: (b, 0, 0, 0))],
        out_shape=[jax.ShapeDtypeStruct((n, N_HEADS * dk), F32),
                   jax.ShapeDtypeStruct((bn, 8, cc), F32),
                   jax.ShapeDtypeStruct((bn, N_HEADS, dk, dk), F32)],
        scratch_shapes=[pltpu.VMEM((8 + tg, cc), F32), pltpu.VMEM((tg, cc), F32),
                        pltpu.VMEM((tg, 128), F32), pltpu.VMEM((tg, 128), F32),
                        pltpu.VMEM((N_HEADS, dk, dk), F32)],
        compiler_params=_cparams(("parallel", "arbitrary")), name="gated_delta")(
            qkvz, qkvz, gates, conv0, s0, conv_w, hp, nw.reshape(1, dk))


def _mlstm_kernel(q_ref, k_ref, v_ref, z_ref, og_ref, gt_ref, c0_ref, n0_ref, m0_ref, hp_ref, nw_ref,
                  o_ref, co_ref, no_ref, mo_ref,
                  li_scr, lf_scr, cext, m_scr, *, c, nch, tv, dqk, dv, unroll):
    gi = pl.program_id(1)
    tg = c * nch
    lane1 = _iota((dqk, 128), 1)

    @pl.when(gi == 0)
    def _():
        for h in range(N_HEADS):
            cext[h, :, 0:dv] = c0_ref[0, h]
            cext[h, :, dv:dv + 128] = jnp.where(lane1 == 0, n0_ref[0, h], 0.0)
        m_scr[...] = m0_ref[0]

    gt = gt_ref[...]
    hp = hp_ref[...]
    lane = _iota((tg, 128), 1)
    li = gt + hp[0:1, :]
    x = gt + hp[1:2, :]
    lf = jnp.minimum(x, 0.0) - jnp.log(1.0 + jnp.exp(-jnp.abs(x)))
    lf = jnp.where(lane >= N_HEADS, jnp.where(lane < 2 * N_HEADS, lf, 0.0), 0.0)
    if tv < c:
        valid = (_iota((tg, 128), 0) % c) < tv
        lf = jnp.where(valid, lf, 0.0)
        li = jnp.where(valid, li, NEG)
    li_scr[...] = li
    lf_scr[...] = lf

    row = _iota((c, c), 0)
    col = _iota((c, c), 1)
    ige = row >= col
    tri = jnp.where(ige, 1.0, 0.0).astype(BF16)
    ones_b = jnp.ones((c, 128), BF16)
    lane_c = _iota((c, 128), 1)
    one_col = jnp.where(lane_c == 0, 1.0, 0.0)
    nw = nw_ref[...]

    def prep(starts):
        items = [(ci, h) for ci in range(len(starts)) for h in range(N_HEADS)]
        bcums = [_exact_dot(tri, lf_scr[pl.ds(r0, c), :]) for r0 in starts]
        lichs = [li_scr[pl.ds(r0, c), :] for r0 in starts]
        qbs = [(q_ref[pl.ds(starts[ci], c), h * dqk:(h + 1) * dqk] * (dqk ** -0.5)).astype(BF16)
               for ci, h in items]
        ks = [k_ref[pl.ds(starts[ci], c), h * dqk:(h + 1) * dqk] for ci, h in items]
        qk_raws = [_dot_nt(qb, k.astype(BF16)) for qb, k in zip(qbs, ks)]
        bcols = [bcums[ci][:, N_HEADS + h:N_HEADS + h + 1] for ci, h in items]
        licols = [lichs[ci][:, h:h + 1] for ci, h in items]
        rowvs = [_exact_dot_nt(ones_b, jnp.where(lane_c == 0, licol - bcol, 0.0))
                 for licol, bcol in zip(licols, bcols)]
        out = []
        for i, (ci, h) in enumerate(items):
            dmat = jnp.where(ige, bcols[i] + rowvs[i], NEG)
            v = v_ref[pl.ds(starts[ci], c), h * dv:(h + 1) * dv]
            blast = bcums[ci][c - 1:c, N_HEADS + h:N_HEADS + h + 1]
            out.append((qbs[i], ks[i], jnp.concatenate([v, one_col], axis=1).astype(BF16), bcols[i], dmat,
                        jnp.max(dmat, axis=-1, keepdims=True), qk_raws[i], blast, blast - bcols[i] + licols[i]))
        return [out[ci * N_HEADS:(ci + 1) * N_HEADS] for ci in range(len(starts))]

    def body(it, carry):
        starts = [pl.multiple_of((it * unroll + i) * c, c) for i in range(unroll)]
        hs = range(N_HEADS)
        for r0, heads in zip(starts, prep(starts)):
            mprevs = [m_scr[0:1, h:h + 1] for h in hs]
            c_olds = [cext[h] for h in hs]
            qcs = [_dot(heads[h][0], c_olds[h].astype(BF16)) for h in hs]
            inters = [heads[h][3] + mprevs[h] for h in hs]
            mts = [jnp.maximum(inters[h], heads[h][5]) for h in hs]
            mnews = [mts[h][c - 1:c, :] for h in hs]
            for h in hs:
                k, vext, blast, wk_log = heads[h][1], heads[h][2], heads[h][7], heads[h][8]
                cext[h] = (c_olds[h] * jnp.exp(blast + mprevs[h] - mnews[h])
                           + _dot_tn((k * jnp.exp(wk_log - mnews[h])).astype(BF16), vext))
                m_scr[0:1, h:h + 1] = mnews[h]
            for h in hs:
                qk = heads[h][6] * jnp.exp(heads[h][4] - mts[h])
                tot = jnp.exp(inters[h] - mts[h]) * qcs[h] + _dot(qk.astype(BF16), heads[h][2])
                hh = tot[:, :dv] / jnp.maximum(jnp.abs(tot[:, dv:dv + 1]), jnp.exp(-mts[h]))
                og = og_ref[pl.ds(r0, c), h * dv:(h + 1) * dv]
                zz = z_ref[pl.ds(r0, c), h * dv:(h + 1) * dv]
                o_ref[pl.ds(r0, c), h * dv:(h + 1) * dv] = _rms(_sigmoid(og) * hh, nw) * _silu(zz)
        return carry

    lax.fori_loop(0, nch // unroll, body, 0)

    @pl.when(gi == pl.num_programs(1) - 1)
    def _():
        for h in range(N_HEADS):
            co_ref[0, h] = cext[h, :, 0:dv]
            no_ref[0, h] = cext[h, :, dv:dv + 1]
        mo_ref[0] = m_scr[...]


def _mlstm(q, k, v, z, og, gates, c0, c_idx, n0, m0, hp, nw, *, bn, c, nch, tv):
    n = q.shape[0]
    dqk, dv = c0.shape[-2], c0.shape[-1]
    tg = c * nch
    steps = n // (bn * tg)
    hq, hv = N_HEADS * dqk, N_HEADS * dv
    row = lambda b, g: (b * steps + g, 0)
    kern = functools.partial(_mlstm_kernel, c=c, nch=nch, tv=tv, dqk=dqk, dv=dv, unroll=math.gcd(nch, 2))
    return pl.pallas_call(
        kern, grid=(bn, steps),
        in_specs=[pl.BlockSpec((tg, hq), row), pl.BlockSpec((tg, hq), row),
                  pl.BlockSpec((tg, hv), row), pl.BlockSpec((tg, hv), row),
                  pl.BlockSpec((tg, hv), row), pl.BlockSpec((tg, 128), row),
                  pl.BlockSpec((None, 1, N_HEADS, dqk, dv), lambda b, g: (c_idx, b, 0, 0, 0)),
                  pl.BlockSpec((1, N_HEADS, dqk, 1), lambda b, g: (b, 0, 0, 0)),
                  pl.BlockSpec((1, 8, 128), lambda b, g: (b, 0, 0)),
                  pl.BlockSpec((8, 128), lambda b, g: (0, 0)),
                  pl.BlockSpec((1, dv), lambda b, g: (0, 0))],
        out_specs=[pl.BlockSpec((tg, hv), row),
                   pl.BlockSpec((1, N_HEADS, dqk, dv), lambda b, g: (b, 0, 0, 0)),
                   pl.BlockSpec((1, N_HEADS, dqk, 1), lambda b, g: (b, 0, 0, 0)),
                   pl.BlockSpec((1, 8, 128), lambda b, g: (b, 0, 0))],
        out_shape=[jax.ShapeDtypeStruct((n, hv), F32),
                   jax.ShapeDtypeStruct((bn, N_HEADS, dqk, dv), F32),
                   jax.ShapeDtypeStruct((bn, N_HEADS, dqk, 1), F32),
                   jax.ShapeDtypeStruct((bn, 8, 128), F32)],
        scratch_shapes=[pltpu.VMEM((tg, 128), F32), pltpu.VMEM((tg, 128), F32),
                        pltpu.VMEM((N_HEADS, dqk, dv + 128), F32), pltpu.VMEM((8, 128), F32)],
        compiler_params=_cparams(("parallel", "arbitrary")), name="mlstm")(
            q, k, v, z, og, gates, c0, n0, m0, hp, nw.reshape(1, dv))


def _lambda(lq, lam_init):
    a = jnp.sum(lq[0:1, :] * lq[1:2, :], axis=-1, keepdims=True)
    b = jnp.sum(lq[2:3, :] * lq[3:4, :], axis=-1, keepdims=True)
    return jnp.exp(a) - jnp.exp(b) + lam_init


def _attn_kernel(wq_ref, k_ref, vt_ref, z_ref, lq_ref, nw_ref, o_ref, acc_scr, *, t, hpb, lam_init):
    qi = pl.program_id(2)
    dh = k_ref.shape[1] // hpb
    nl = 4 * t
    acc_scr[...] = jnp.zeros(acc_scr.shape, F32)

    def step(jp, carry, masked):
        rows = [pl.multiple_of((2 * jp + i) * t, t) for i in range(2)]
        wqs = [jnp.concatenate([wq_ref[0, a], wq_ref[1, a]], axis=1) for a in range(hpb)]
        scores = [[_dot(k_ref[pl.ds(rows[i], t), a * dh:(a + 1) * dh], wqs[a]) for i in range(2)]
                  for a in range(hpb)]
        out = []
        for a in range(hpb):
            m_prev, l_prev = carry[a]
            ss = scores[a]
            if masked:
                lane = _iota((t, nl), 1)
                qpos = (lane // (2 * t)) * t + lane % t
                kpos = _iota((t, nl), 0)
                ss = [jnp.where(kpos + i * t <= qpos, ss[i], NEG) for i in range(2)]
            m_new = jnp.maximum(m_prev, jnp.maximum(jnp.max(ss[0], axis=0, keepdims=True),
                                                    jnp.max(ss[1], axis=0, keepdims=True)))
            alpha = jnp.exp2(m_prev - m_new)
            ps = [jnp.exp2(s - m_new) for s in ss]
            l_new = alpha * l_prev + (jnp.sum(ps[0], axis=0, keepdims=True) + jnp.sum(ps[1], axis=0, keepdims=True))
            pv = (_dot(vt_ref[2 * jp, a * dh:(a + 1) * dh, :], ps[0].astype(BF16))
                  + _dot(vt_ref[2 * jp + 1, a * dh:(a + 1) * dh, :], ps[1].astype(BF16)))
            acc_scr[a] = alpha * acc_scr[a] + pv
            out.append((m_new, l_new))
        return tuple(out)

    init = tuple((jnp.full((1, nl), NEG, F32), jnp.zeros((1, nl), F32)) for _ in range(hpb))
    carry = lax.fori_loop(0, qi, lambda j, c: step(j, c, False), init)
    carry = step(qi, carry, True)

    lam = _lambda(lq_ref[...], lam_init)
    nw = nw_ref[...]
    for a in range(hpb):
        acc = acc_scr[a] * (1.0 / carry[a][1])
        for qb in range(2):
            b0 = 2 * t * qb
            o = (acc[:, b0:b0 + t] - lam * acc[:, b0 + t:b0 + 2 * t]).T
            zz = z_ref[qb * t:(qb + 1) * t, a * dh:(a + 1) * dh]
            o_ref[qb * t:(qb + 1) * t, a * dh:(a + 1) * dh] = _rms(o, nw) * (1.0 - lam_init) * _silu(zz)


def _attn_prompt(wq, k16, vt, z, lq, nw, *, bn, seq, lam_init):
    n = k16.shape[0]
    t = vt.shape[2]
    dh = wq.shape[2]
    hpb = 2
    nq = seq // t
    nq2 = nq // 2
    kern = functools.partial(_attn_kernel, t=t, hpb=hpb, lam_init=lam_init)
    return pl.pallas_call(
        kern, grid=(bn, N_HEADS // hpb, nq2),
        in_specs=[pl.BlockSpec((2, hpb, dh, 2 * t), lambda b, h, i: (b * nq2 + i, h, 0, 0)),
                  pl.BlockSpec((seq, hpb * dh), lambda b, h, i: (b, h)),
                  pl.BlockSpec((nq, hpb * dh, t), lambda b, h, i: (b, h, 0)),
                  pl.BlockSpec((2 * t, hpb * dh), lambda b, h, i: (b * nq2 + i, h)),
                  pl.BlockSpec(lq.shape, lambda b, h, i: (0, 0)),
                  pl.BlockSpec((1, dh), lambda b, h, i: (0, 0))],
        out_specs=pl.BlockSpec((2 * t, hpb * dh), lambda b, h, i: (b * nq2 + i, h)),
        out_shape=jax.ShapeDtypeStruct((n, N_HEADS * dh), F32),
        scratch_shapes=[pltpu.VMEM((hpb, dh, 4 * t), F32)],
        compiler_params=_cparams(("parallel", "parallel", "arbitrary")), name="diff_attn_prompt")(
            wq, k16, vt, z, lq, nw.reshape(1, dh))


def _attn_s_kernel(pt_ref, q_ref, kn_ref, vn_ref, z_ref, lq_ref, nw_ref, *rest, pg, tv, lam_init):
    k_pages = rest[:pg]
    v_pages = rest[pg:2 * pg]
    o_ref, qbd, bias, m_scr, l_scr, acc_scr = rest[2 * pg:]
    j = pl.program_id(1)
    tp, hd = q_ref.shape
    dh = hd // N_HEADS
    dqk = dh // 2
    prow = k_pages[0].shape[0]
    rph = 2 * tp

    @pl.when(j == 0)
    def _():
        q = q_ref[...] * (dqk ** -0.5)
        lane = _iota((tp, dh), 1)
        for h in range(N_HEADS):
            qh = q[:, h * dh:(h + 1) * dh]
            qbd[h * rph:h * rph + tp, :] = jnp.where(lane < dqk, qh, 0.0)
            qbd[h * rph + tp:(h + 1) * rph, :] = jnp.where(lane >= dqk, qh, 0.0)
        rhead = _iota(bias.shape, 0) // rph
        chead = _iota(bias.shape, 1) % N_HEADS
        bias[...] = jnp.where(rhead == chead, 0.0, NEG)
        m_scr[...] = jnp.full(m_scr.shape, NEG, F32)
        l_scr[...] = jnp.zeros(l_scr.shape, F32)
        acc_scr[...] = jnp.zeros(acc_scr.shape, F32)

    def update(s, pv):
        m_prev = m_scr[...]
        m_new = jnp.maximum(m_prev, jnp.max(s, axis=-1, keepdims=True))
        alpha = jnp.exp(m_prev - m_new)
        p = jnp.exp(s - m_new[:, 0:1])
        l_scr[...] = alpha * l_scr[...] + jnp.sum(p, axis=-1, keepdims=True)
        acc_scr[...] = alpha * acc_scr[...] + pv(p)
        m_scr[...] = m_new

    qb = qbd[...].astype(BF16)
    bs_ = bias[...]
    s = jnp.concatenate([_dot_nt(qb, kp[...].astype(BF16)) + bs_ for kp in k_pages], axis=1)

    def pv_pages(p):
        pb = p.astype(BF16)
        acc = _dot(pb[:, 0:prow], v_pages[0][...].astype(BF16))
        for i in range(1, pg):
            acc = acc + _dot(pb[:, i * prow:(i + 1) * prow], v_pages[i][...].astype(BF16))
        return acc

    update(s, pv_pages)

    @pl.when(j == pl.num_programs(1) - 1)
    def _():
        qf = qbd[...]
        kn = kn_ref[...]
        vn = vn_ref[...]
        sn = jnp.concatenate([_dot_nt(qf[h * rph:(h + 1) * rph], kn[:, h * dh:(h + 1) * dh])
                              for h in range(N_HEADS)], axis=0)
        tq = _iota(sn.shape, 0) % tp
        tk = _iota(sn.shape, 1)
        sn = jnp.where(tk <= tq, jnp.where(tk < tv, sn, NEG), NEG)
        update(sn, lambda p: jnp.concatenate(
            [_dot(p[h * rph:(h + 1) * rph], vn[:, h * dh:(h + 1) * dh]) for h in range(N_HEADS)], axis=0))
        acc = acc_scr[...] / l_scr[...]
        lam = _lambda(lq_ref[...], lam_init)
        nw = nw_ref[...]
        normed = jnp.concatenate(
            [_rms(acc[h * rph:h * rph + tp] - lam * acc[h * rph + tp:(h + 1) * rph], nw) for h in range(N_HEADS)],
            axis=1)
        o_ref[...] = normed * (1.0 - lam_init) * _silu(z_ref[...])


def _attn_sample(page_table, q, kn, vn, z, lq, nw, cache_k, cache_v, *, e, tv, lam_init):
    bs, npages = page_table.shape
    n, hd = q.shape
    tp = n // bs
    dh = hd // N_HEADS
    pg = math.gcd(npages, PAGES_PER_STEP)
    prow = cache_k.shape[2]
    steps = npages // pg
    rows = 2 * N_HEADS * tp
    tok = lambda b, j, pt: (b, 0)
    page_specs = [pl.BlockSpec((None, None, prow, dh),
                               lambda b, j, pt, i=i: (e, pt[b, j * pg + i], 0, 0)) for i in range(pg)]
    kern = functools.partial(_attn_s_kernel, pg=pg, tv=tv, lam_init=lam_init)
    grid_spec = pltpu.PrefetchScalarGridSpec(
        num_scalar_prefetch=1, grid=(bs, steps),
        in_specs=[pl.BlockSpec((tp, hd), tok), pl.BlockSpec((tp, hd), tok), pl.BlockSpec((tp, hd), tok),
                  pl.BlockSpec((tp, hd), tok),
                  pl.BlockSpec(lq.shape, lambda b, j, pt: (0, 0)),
                  pl.BlockSpec((1, dh), lambda b, j, pt: (0, 0))] + page_specs + page_specs,
        out_specs=pl.BlockSpec((tp, hd), tok),
        scratch_shapes=[pltpu.VMEM((rows, dh), F32), pltpu.VMEM((rows, prow), F32), pltpu.VMEM((rows, 128), F32),
                        pltpu.VMEM((rows, 128), F32), pltpu.VMEM((rows, dh), F32)])
    return pl.pallas_call(
        kern, grid_spec=grid_spec, out_shape=jax.ShapeDtypeStruct((n, hd), F32),
        compiler_params=_cparams(("parallel", "arbitrary")), name="diff_attn_sample")(
            page_table, q, kn, vn, z, lq, nw.reshape(1, dh), *([cache_k] * pg), *([cache_v] * pg))


def _rope_tables(pos, dqk):
    rope_dim = dqk // 4
    half = rope_dim // 2
    inv = jnp.power(ROPE_THETA, -jnp.arange(half, dtype=F32) / half)
    ang = pos.astype(F32)[:, None] * inv[None, :]
    lane = jnp.arange(128) % dqk
    idx = lane % half
    cos = jnp.where(lane < rope_dim, jnp.cos(ang)[:, idx], 1.0)
    sin = jnp.sin(ang)[:, idx]
    sa = jnp.where((lane >= half) & (lane < rope_dim), sin, 0.0)
    sb = jnp.where(lane < half, -sin, 0.0)
    return cos.astype(F32), sa.astype(F32), sb.astype(F32)


def _lane_row(vals, offset):
    out = jnp.zeros((128,), F32)
    return out.at[offset:offset + vals.shape[0]].set(vals.astype(F32))


def kernel(x_prompt, x_sample, cache_k, cache_v, page_table, state_gdn_conv, state_gdn_s, state_mlstm_c,
           state_mlstm_n, state_mlstm_m, norm_w, final_norm_w, w_in_even, w_out_even, conv_w, a_log, dt_bias,
           gdn_norm_w, lambda_qk, diff_norm_w, w_in_odd, w_out_odd, b_i, b_f, mlstm_norm_w):
    bp, seq, d = x_prompt.shape
    bs, ts, _ = x_sample.shape
    depth = norm_w.shape[0]
    tp = SAMPLE_PAD
    dk_a = state_gdn_s.shape[-1]
    conv_ch = state_gdn_conv.shape[-1]
    w_a = N_HEADS * dk_a
    dh_b = cache_v.shape[-1]
    dqk_b = dh_b // 2
    w_b = N_HEADS * dh_b
    dqk_c, dv_c = state_mlstm_c.shape[-2], state_mlstm_c.shape[-1]
    hq_c, w_c = N_HEADS * dqk_c, N_HEADS * dv_c
    past = page_table.shape[1] * cache_k.shape[2]
    n_pool, page = cache_k.shape[1], cache_k.shape[2]

    xp = x_prompt.reshape(bp * seq, d)
    xs = jnp.pad(x_sample, ((0, 0), (0, tp - ts), (0, 0))).reshape(bs * tp, d)
    tm_p = math.gcd(seq, 256)
    tm_s = math.gcd(bs * tp, 256)
    nch_p = math.gcd(seq // CHUNK, 4)

    rope_p = _rope_tables(jnp.arange(seq), dqk_b)
    rope_s = _rope_tables(past + (jnp.arange(tm_s) % tp), dqk_b)
    half = dqk_b // 8
    ang_t = (jnp.power(ROPE_THETA, -jnp.arange(half, dtype=F32) / half)[:, None]
             * jnp.arange(seq).astype(F32)[None, :])
    cos_t, sin_t = jnp.cos(ang_t), jnp.sin(ang_t)

    even_w = conv_ch + w_a
    seg_even = [(0, even_w), (even_w, w_b), (even_w + w_b, w_b), (even_w + 2 * w_b, w_b),
                (even_w + 3 * w_b, w_b), (even_w + 4 * w_b, 128)]
    seg_even_p = [seg_even[0]] + seg_even[2:]
    rope_even = tuple(range(even_w // 128, (even_w + 2 * w_b) // 128))
    odd_w = 2 * hq_c + 3 * w_c
    seg_odd = [(0, hq_c), (hq_c, hq_c), (2 * hq_c, w_c), (2 * hq_c + w_c, w_c), (2 * hq_c + 2 * w_c, w_c),
               (odd_w, 128)]

    ck = cache_k.reshape(cache_k.shape[0], n_pool, page * N_HEADS, dh_b)
    cv = cache_v.reshape(cache_v.shape[0], n_pool, page * N_HEADS, dh_b)

    n_even = (depth + 1) // 2
    krows_p = jnp.zeros((n_even, bp * seq * N_HEADS, dh_b), F32)
    vrows_p = jnp.zeros((n_even, bp * seq * N_HEADS, dh_b), F32)
    k_rows_s, v_rows_s = [], []
    conv_p, s_p, conv_s, s_s = [], [], [], []
    c_p, n_p, m_p, c_s, n_s, m_s = [], [], [], [], [], []

    for layer in range(depth):
        last = layer == depth - 1
        if layer % 2 == 0:
            e = layer // 2
            lam_init = 0.8 - 0.6 * math.exp(-0.3 * layer)
            w = w_in_even[e]
            g0 = even_w
            w_r = jnp.concatenate([w[:, :g0], w[:, g0 + 2 * N_HEADS:], w[:, g0:g0 + 2 * N_HEADS],
                                   jnp.zeros((d, 128 - 2 * N_HEADS), F32)], axis=1).astype(BF16)
            w_o = w_out_even[e].astype(BF16)
            hp = jnp.zeros((8, 128), F32).at[0].set(_lane_row(a_log[e], N_HEADS)).at[1].set(
                _lane_row(dt_bias[e], N_HEADS))
            lq = lambda_qk[e]

            wq_t = w[:, g0 + 2 * N_HEADS:g0 + 2 * N_HEADS + w_b].T.astype(BF16)
            wv_t = w[:, g0 + 2 * N_HEADS + 2 * w_b:g0 + 2 * N_HEADS + 3 * w_b].T.astype(BF16)
            az, krows_p, vrows_p, zb, gt, k16, wq, vt = _proj(
                xp, norm_w[layer], w_r, seg_even_p, (rope_even, rope_p), tm_p, k16_seg=1,
                feat=(wq_t, wv_t, cos_t, sin_t), rows=((1, 2), e, (krows_p, vrows_p)))
            oa, cst, sst = _gdn(az, gt, jnp.zeros((bp, 8, conv_ch), F32),
                                jnp.zeros((1, bp, N_HEADS, dk_a, dk_a), F32), 0,
                                conv_w[e], hp, gdn_norm_w[e], bn=bp, c=CHUNK, nch=nch_p, tv=CHUNK)
            ob = _attn_prompt(wq, k16, vt, zb, lq, diff_norm_w[e], bn=bp, seq=seq, lam_init=lam_init)
            xp = _out_proj(oa, ob, w_a, 0, w_o, xp, final_norm_w, last, tm_p)
            conv_p.append(cst[:, 8 - (CONV_W - 1):])
            s_p.append(sst)

            az, qb, kb, vb, zb, gt = _proj(xs, norm_w[layer], w_r, seg_even, (rope_even, rope_s), tm_s)
            conv0 = jnp.pad(state_gdn_conv[e], ((0, 0), (8 - (CONV_W - 1), 0), (0, 0)))
            oa, cst, sst = _gdn(az, gt, conv0, state_gdn_s, e, conv_w[e], hp, gdn_norm_w[e],
                                bn=bs, c=tp, nch=1, tv=ts)
            ob = _attn_sample(page_table, qb, kb, vb, zb, lq, diff_norm_w[e], ck, cv, e=e, tv=ts,
                              lam_init=lam_init)
            xs = _out_proj(oa, ob, w_a, 0, w_o, xs, final_norm_w, last, tm_s)
            k_rows_s.append(kb.reshape(bs, tp, N_HEADS, dh_b)[:, :ts])
            v_rows_s.append(vb.reshape(bs, tp, N_HEADS, dh_b)[:, :ts])
            conv_s.append(cst[:, 8 - (CONV_W - 1):])
            s_s.append(sst)
        else:
            o = layer // 2
            w_r = jnp.concatenate([w_in_odd[o], jnp.zeros((d, 128 - 2 * N_HEADS), F32)], axis=1).astype(BF16)
            w_o = w_out_odd[o].astype(BF16)
            hp = jnp.zeros((8, 128), F32).at[0].set(_lane_row(b_i[o], 0)).at[1].set(_lane_row(b_f[o], N_HEADS))

            q, k, v, z, og, gt = _proj(xp, norm_w[layer], w_r, seg_odd, None, tm_p)
            hm, cf, nf, mf = _mlstm(q, k, v, z, og, gt, jnp.zeros((1, bp, N_HEADS, dqk_c, dv_c), F32), 0,
                                    jnp.zeros((bp, N_HEADS, dqk_c, 1), F32), jnp.zeros((bp, 8, 128), F32),
                                    hp, mlstm_norm_w[o], bn=bp, c=CHUNK, nch=nch_p, tv=CHUNK)
            xp = _out_proj(hm, hm, w_c // 2, 1, w_o, xp, final_norm_w, last, tm_p)
            c_p.append(cf)
            n_p.append(nf[..., 0])
            m_p.append(mf[:, 0, :N_HEADS])

            q, k, v, z, og, gt = _proj(xs, norm_w[layer], w_r, seg_odd, None, tm_s)
            m0 = jnp.zeros((bs, 8, 128), F32).at[:, 0, :N_HEADS].set(state_mlstm_m[o])
            hm, cf, nf, mf = _mlstm(q, k, v, z, og, gt, state_mlstm_c, o, state_mlstm_n[o][..., None], m0,
                                    hp, mlstm_norm_w[o], bn=bs, c=tp, nch=1, tv=ts)
            xs = _out_proj(hm, hm, w_c // 2, 1, w_o, xs, final_norm_w, last, tm_s)
            c_s.append(cf)
            n_s.append(nf[..., 0])
            m_s.append(mf[:, 0, :N_HEADS])

    y_prompt = xp.reshape(bp, seq, d)
    y_sample = xs.reshape(bs, tp, d)[:, :ts]
    st = jnp.stack
    kv_shape = (n_even, bp, seq, N_HEADS, dh_b)
    return (y_prompt, y_sample, krows_p.reshape(kv_shape), vrows_p.reshape(kv_shape), st(k_rows_s), st(v_rows_s),
            st(conv_p), st(s_p), st(conv_s), st(s_s),
            st(c_p), st(n_p), st(m_p), st(c_s), st(n_s), st(m_s))
```

```python
import functools
import math

import jax
import jax.numpy as jnp
from jax import lax
from jax.experimental import pallas as pl
from jax.experimental.pallas import tpu as pltpu

F32 = jnp.float32
BF16 = jnp.bfloat16

EPS = 1e-6
NEG = -1e30
N_HEADS = 4
CONV_W = 4
ROPE_THETA = 500000.0
CHUNK = 64
SAMPLE_PAD = 8
PAGES_PER_STEP = 16
V7X_VMEM_LIMIT = 56 * 1024 * 1024


def _cparams(sem):
    return pltpu.CompilerParams(dimension_semantics=sem, vmem_limit_bytes=V7X_VMEM_LIMIT)


def _dot(a, b):
    return jnp.dot(a, b, preferred_element_type=F32)


def _dot_nt(a, b):
    return lax.dot_general(a, b, (((1,), (1,)), ((), ())), preferred_element_type=F32)


def _dot_tn(a, b):
    return lax.dot_general(a, b, (((0,), (0,)), ((), ())), preferred_element_type=F32)


def _split3(x):
    hi = x.astype(BF16)
    r = x - hi.astype(F32)
    mid = r.astype(BF16)
    lo = (r - mid.astype(F32)).astype(BF16)
    return hi, mid, lo


def _exact_dot(a_bf16, x):
    hi, mid, lo = _split3(x)
    return _dot(a_bf16, hi) + (_dot(a_bf16, mid) + _dot(a_bf16, lo))


def _exact_dot_nt(a_bf16, x):
    hi, mid, lo = _split3(x)
    return _dot_nt(a_bf16, hi) + (_dot_nt(a_bf16, mid) + _dot_nt(a_bf16, lo))


def _sigmoid(x):
    return 1.0 / (1.0 + jnp.exp(-x))


def _silu(x):
    return x * _sigmoid(x)


def _softplus(x):
    return jnp.maximum(x, 0.0) + jnp.log(1.0 + jnp.exp(-jnp.abs(x)))


def _rms(x, w):
    return x * lax.rsqrt(jnp.mean(x * x, axis=-1, keepdims=True) + EPS) * w


def _iota(shape, dim):
    return lax.broadcasted_iota(jnp.int32, shape, dim)


def _proj_kernel(*refs, segs, rope_blocks, k16_seg, feat_major, row_segs):
    n_out = len(segs)
    pos = 3
    x_ref, nw_ref, w_ref = refs[:3]
    if rope_blocks:
        cos_ref, sa_ref, sb_ref = refs[pos:pos + 3]
        pos += 3
    if feat_major:
        wqt_ref, wvt_ref, cost_ref, sint_ref = refs[pos:pos + 4]
        pos += 4
    pos += len(row_segs)
    outs = refs[pos:pos + n_out]
    pos += n_out
    x = x_ref[...]
    h = _rms(x, nw_ref[...]).astype(BF16)
    for si, ((c0, width), o_ref) in enumerate(zip(segs, outs)):
        for s0 in range(0, width, 512):
            sw = min(512, width - s0)
            acc = _dot(h, w_ref[:, c0 + s0:c0 + s0 + sw])
            if rope_blocks and (c0 + s0) // 128 in rope_blocks:
                cos = cos_ref[...]
                sa = sa_ref[...]
                sb = sb_ref[...]
                parts = []
                for t0 in range(0, sw, 128):
                    a = acc[:, t0:t0 + 128]
                    parts.append(a * cos + pltpu.roll(a, 8, 1) * sa + pltpu.roll(a, 120, 1) * sb)
                acc = jnp.concatenate(parts, axis=1)
            if si in row_segs:
                dh = sw // N_HEADS
                for hd in range(N_HEADS):
                    o_ref[pl.ds(hd, acc.shape[0], stride=N_HEADS), :] = acc[:, hd * dh:(hd + 1) * dh]
            else:
                o_ref[:, s0:s0 + sw] = acc
            if si == k16_seg:
                refs[pos][:, s0:s0 + sw] = acc.astype(BF16)
    if feat_major:
        wq_ref, vt_ref = refs[pos + 1], refs[pos + 2]
        tm = x.shape[0]
        vt_ref[0] = _dot_nt(wvt_ref[...], h).astype(BF16)
        qt = _dot_nt(wqt_ref[...], h)
        dqk = qt.shape[0] // (2 * N_HEADS)
        half = dqk // 8
        cos = cost_ref[...]
        sin = sint_ref[...]
        scale = (dqk ** -0.5) * math.log2(math.e)
        zero = jnp.zeros((dqk, tm), F32)
        for hd in range(N_HEADS):
            maps = []
            for m in range(2):
                r0 = (2 * hd + m) * dqk
                x1 = qt[r0:r0 + half]
                x2 = qt[r0 + half:r0 + 2 * half]
                maps.append(jnp.concatenate([x1 * cos - x2 * sin, x2 * cos + x1 * sin,
                                             qt[r0 + 2 * half:r0 + dqk]], axis=0) * scale)
            top = jnp.concatenate([maps[0], zero], axis=1)
            bot = jnp.concatenate([zero, maps[1]], axis=1)
            wq_ref[0, hd] = jnp.concatenate([top, bot], axis=0).astype(BF16)


def _proj(x, nw, w, segs, rope, tm, k16_seg=None, feat=None, rows=None):
    n, d = x.shape
    ncols = w.shape[1]
    grid = (n // tm,)
    in_specs = [pl.BlockSpec((tm, d), lambda i: (i, 0)),
                pl.BlockSpec((1, d), lambda i: (0, 0)),
                pl.BlockSpec((d, ncols), lambda i: (0, 0))]
    args = [x, nw.reshape(1, d), w]
    rope_blocks = ()
    if rope is not None:
        rope_blocks, tables = rope
        nt = tables[0].shape[0] // tm
        for t in tables:
            in_specs.append(pl.BlockSpec((tm, 128), lambda i, nt=nt: (i % nt, 0)))
            args.append(t)
    out_shape = [jax.ShapeDtypeStruct((n, width), F32) for _, width in segs]
    out_specs = [pl.BlockSpec((tm, width), lambda i: (i, 0)) for _, width in segs]
    if feat is not None:
        wqt, wvt, cost, sint = feat
        ntt = cost.shape[1] // tm
        in_specs += [pl.BlockSpec(wqt.shape, lambda i: (0, 0)), pl.BlockSpec(wvt.shape, lambda i: (0, 0)),
                     pl.BlockSpec((cost.shape[0], tm), lambda i, ntt=ntt: (0, i % ntt)),
                     pl.BlockSpec((sint.shape[0], tm), lambda i, ntt=ntt: (0, i % ntt))]
        args += [wqt, wvt, cost, sint]
    row_segs, aliases = (), {}
    if rows is not None:
        row_segs, layer, stacked = rows
        for si, arr in zip(row_segs, stacked):
            aliases[len(args)] = si
            in_specs.append(pl.BlockSpec(memory_space=pl.ANY))
            args.append(arr)
            out_shape[si] = jax.ShapeDtypeStruct(arr.shape, F32)
            out_specs[si] = pl.BlockSpec((None, tm * N_HEADS, arr.shape[2]), lambda i, layer=layer: (layer, i, 0))
    if k16_seg is not None:
        kw = segs[k16_seg][1]
        out_shape.append(jax.ShapeDtypeStruct((n, kw), BF16))
        out_specs.append(pl.BlockSpec((tm, kw), lambda i: (i, 0)))
    if feat is not None:
        hq, hv = wqt.shape[0], wvt.shape[0]
        dh = hq // N_HEADS
        out_shape += [jax.ShapeDtypeStruct((n // tm, N_HEADS, dh, 2 * tm), BF16),
                      jax.ShapeDtypeStruct((n // tm, hv, tm), BF16)]
        out_specs += [pl.BlockSpec((1, N_HEADS, dh, 2 * tm), lambda i: (i, 0, 0, 0)),
                      pl.BlockSpec((1, hv, tm), lambda i: (i, 0, 0))]
    return pl.pallas_call(
        functools.partial(_proj_kernel, segs=tuple(segs), rope_blocks=tuple(rope_blocks), k16_seg=k16_seg,
                          feat_major=feat is not None, row_segs=tuple(row_segs)),
        grid=grid, in_specs=in_specs, out_specs=out_specs, out_shape=out_shape, input_output_aliases=aliases,
        compiler_params=_cparams(("parallel",)), name="norm_proj")(*args)


def _out_kernel(a_ref, b_ref, w_ref, x_ref, fw_ref, o_ref, *, final):
    ka = a_ref.shape[1]
    y = _dot(a_ref[...].astype(BF16), w_ref[0:ka, :]) + _dot(b_ref[...].astype(BF16), w_ref[ka:, :])
    xn = x_ref[...] + y
    if final:
        xn = _rms(xn, fw_ref[...])
    o_ref[...] = xn


def _out_proj(a, b, ka, b_col, w, x, fw, final, tm):
    n, d = x.shape
    kb = w.shape[0] - ka
    return pl.pallas_call(
        functools.partial(_out_kernel, final=final),
        grid=(n // tm,),
        in_specs=[pl.BlockSpec((tm, ka), lambda i: (i, 0)),
                  pl.BlockSpec((tm, kb), lambda i: (i, b_col)),
                  pl.BlockSpec(w.shape, lambda i: (0, 0)),
                  pl.BlockSpec((tm, d), lambda i: (i, 0)),
                  pl.BlockSpec((1, d), lambda i: (0, 0))],
        out_specs=pl.BlockSpec((tm, d), lambda i: (i, 0)),
        out_shape=jax.ShapeDtypeStruct((n, d), F32),
        compiler_params=_cparams(("parallel",)), name="out_proj")(a, b, w, x, fw.reshape(1, d))


def _neumann(a_list, c):
    eye = jnp.where(_iota((c, c), 0) == _iota((c, c), 1), 1.0, 0.0)
    ts = [eye - a for a in a_list]
    abs_ = [a.astype(BF16) for a in a_list]
    ms = [_dot(ab, ab) for ab in abs_]
    levels = int(math.log2(c)) - 1
    for k in range(levels):
        mbs = [m.astype(BF16) for m in ms]
        ts = [t + _dot(mb, t.astype(BF16)) for mb, t in zip(mbs, ts)]
        if k + 1 < levels:
            ms = [_dot(mb, mb) for mb in mbs]
    return ts


def _gdn_kernel(x_ref, z_ref, gt_ref, conv0_ref, s0_ref, cw_ref, hp_ref, nw_ref,
                o_ref, convo_ref, so_ref,
                xbuf, qkv_scr, g_scr, b_scr, s_scr, *, c, nch, tv, dk, unroll, per_seq):
    gi = pl.program_id(1)
    tg = c * nch
    hq = N_HEADS * dk
    cw = cw_ref[...]

    def conv(base, rows):
        return _silu((xbuf[base + 5:base + 5 + rows, :] * cw[0:1, :] + xbuf[base + 6:base + 6 + rows, :] * cw[1:2, :])
                     + (xbuf[base + 7:base + 7 + rows, :] * cw[2:3, :] + xbuf[base + 8:base + 8 + rows, :] * cw[3:4, :]))

    if per_seq:
        for s in range(nch):
            base = s * (8 + c)
            xbuf[base:base + 8, :] = conv0_ref[s]
            xbuf[base + 8:base + 8 + c, :] = x_ref[s * c:(s + 1) * c, :]
            qkv_scr[s * c:(s + 1) * c, :] = conv(base, c)
            convo_ref[s, 0:8 - (CONV_W - 1), :] = jnp.zeros((8 - (CONV_W - 1), xbuf.shape[1]), F32)
            convo_ref[s, 8 - (CONV_W - 1):8, :] = xbuf[base + 8 + tv - (CONV_W - 1):base + 8 + tv, :]
    else:
        @pl.when(gi == 0)
        def _():
            xbuf[0:8, :] = conv0_ref[0]
            s_scr[...] = s0_ref[0]

        xbuf[8:8 + tg, :] = x_ref[...]
        qkv_scr[...] = conv(0, tg)
        xbuf[5:8, :] = xbuf[8 + tg - 3:8 + tg, :]

    gt = gt_ref[...]
    hp = hp_ref[...]
    lane = _iota((tg, 128), 1)
    beta = _sigmoid(gt)
    g = -jnp.exp(hp[0:1, :]) * _softplus(gt + hp[1:2, :])
    g = jnp.where(lane >= N_HEADS, jnp.where(lane < 2 * N_HEADS, g, 0.0), 0.0)
    if tv < c:
        valid = (_iota((tg, 128), 0) % c) < tv
        g = jnp.where(valid, g, 0.0)
        beta = jnp.where(valid, beta, 0.0)
    g_scr[...] = g
    b_scr[...] = beta

    row = _iota((c, c), 0)
    col = _iota((c, c), 1)
    ige = row >= col
    igt = row > col
    tri = jnp.where(ige, 1.0, 0.0).astype(BF16)
    lane_c = _iota((c, 128), 1)
    nw = nw_ref[...]

    sels = [jnp.where(lane_c == N_HEADS + h, 1.0, 0.0).astype(BF16) for h in range(N_HEADS)]

    def prep(starts):
        items = [(ci, h) for ci in range(len(starts)) for h in range(N_HEADS)]
        gcums = [_exact_dot(tri, g_scr[pl.ds(r0, c), :]) for r0 in starts]
        bchs = [b_scr[pl.ds(r0, c), :] for r0 in starts]
        qns, kns, vhs = [], [], []
        for ci, h in items:
            r0 = starts[ci]
            qh = qkv_scr[pl.ds(r0, c), h * dk:(h + 1) * dk]
            kh = qkv_scr[pl.ds(r0, c), hq + h * dk:hq + (h + 1) * dk]
            vhs.append(qkv_scr[pl.ds(r0, c), 2 * hq + h * dk:2 * hq + (h + 1) * dk])
            qns.append(qh * lax.rsqrt(jnp.sum(qh * qh, axis=-1, keepdims=True) + EPS) * (dk ** -0.5))
            kns.append(kh * lax.rsqrt(jnp.sum(kh * kh, axis=-1, keepdims=True) + EPS))
        qkks = [_dot_nt(jnp.concatenate([qn, kn], axis=0).astype(BF16), kn.astype(BF16))
                for qn, kn in zip(qns, kns)]
        grows = [_exact_dot_nt(sels[h], gcums[ci]) for ci, h in items]
        gcols = [gcums[ci][:, N_HEADS + h:N_HEADS + h + 1] for ci, h in items]
        bcols = [bchs[ci][:, h:h + 1] for ci, h in items]
        decays = [jnp.where(ige, jnp.exp(jnp.where(ige, gcol - grow, 0.0)), 0.0)
                  for gcol, grow in zip(gcols, grows)]
        t_invs = _neumann([jnp.where(igt, qkk[c:] * decay * bcol, 0.0)
                           for qkk, decay, bcol in zip(qkks, decays, bcols)], c)
        egs = [jnp.exp(gcol) for gcol in gcols]
        sols = [_dot(t_inv.astype(BF16),
                     jnp.concatenate([vh * bcol, kn * (bcol * eg)], axis=1).astype(BF16))
                for t_inv, vh, kn, bcol, eg in zip(t_invs, vhs, kns, bcols, egs)]
        glasts = [gcums[ci][c - 1:c, N_HEADS + h:N_HEADS + h + 1] for ci, h in items]
        kws = [(kn * jnp.exp(glast - gcol)).astype(BF16) for kn, glast, gcol in zip(kns, glasts, gcols)]
        aqks = [(qkk[:c] * decay).astype(BF16) for qkk, decay in zip(qkks, decays)]
        sol_bs = [sol.astype(BF16) for sol in sols]
        ktus = [_dot_tn(kw, sol_b) for kw, sol_b in zip(kws, sol_bs)]
        aus = [_dot(aqk, sol_b) for aqk, sol_b in zip(aqks, sol_bs)]
        out = []
        for i in range(len(items)):
            lhs = jnp.concatenate([-ktus[i][:, dk:], qns[i] * egs[i] - aus[i][:, dk:]], axis=0).astype(BF16)
            out.append((lhs, ktus[i][:, :dk], aus[i][:, :dk], jnp.exp(glasts[i])))
        return [out[ci * N_HEADS:(ci + 1) * N_HEADS] for ci in range(len(starts))]

    def chunks(starts, seq_ids):
        preps = prep(starts)
        for r0, sid, heads in zip(starts, seq_ids, preps):
            s_olds = [s_scr[h] if sid is None else s0_ref[sid, h] for h in range(N_HEADS)]
            res = [_dot(heads[h][0], s_olds[h].astype(BF16)) for h in range(N_HEADS)]
            for h in range(N_HEADS):
                s_new = s_olds[h] * heads[h][3] + (heads[h][1] + res[h][:dk])
                if sid is None:
                    s_scr[h] = s_new
                else:
                    so_ref[sid, h] = s_new
            for h in range(N_HEADS):
                zz = z_ref[pl.ds(r0, c), h * dk:(h + 1) * dk]
                o = heads[h][2] + res[h][dk:]
                o_ref[pl.ds(r0, c), h * dk:(h + 1) * dk] = _rms(o, nw) * _silu(zz)

    if per_seq:
        chunks([s * c for s in range(nch)], list(range(nch)))
    else:
        def body(it, carry):
            chunks([pl.multiple_of((it * unroll + i) * c, c) for i in range(unroll)], [None] * unroll)
            return carry

        lax.fori_loop(0, nch // unroll, body, 0)

        @pl.when(gi == pl.num_programs(1) - 1)
        def _():
            convo_ref[0] = xbuf[0:8, :]
            so_ref[0] = s_scr[...]


def _gdn(qkvz, gates, conv0, s0, s_idx, conv_w, hp, nw, *, bn, c, nch, tv, per_seq=False):
    n = qkvz.shape[0]
    dk = s0.shape[-1]
    cc = 3 * N_HEADS * dk
    tg = c * nch
    sb = nch if per_seq else 1
    steps = 1 if per_seq else n // (bn * tg)
    kern = functools.partial(_gdn_kernel, c=c, nch=nch, tv=tv, dk=dk, unroll=math.gcd(nch, 4), per_seq=per_seq)
    return pl.pallas_call(
        kern, grid=(bn // sb, steps),
        in_specs=[pl.BlockSpec((tg, cc), lambda b, g: (b * steps + g, 0)),
                  pl.BlockSpec((tg, N_HEADS * dk), lambda b, g: (b * steps + g, 3)),
                  pl.BlockSpec((tg, 128), lambda b, g: (b * steps + g, 0)),
                  pl.BlockSpec((sb, 8, cc), lambda b, g: (b, 0, 0)),
                  pl.BlockSpec((None, sb, N_HEADS, dk, dk), lambda b, g: (s_idx, b, 0, 0, 0)),
                  pl.BlockSpec((CONV_W, cc), lambda b, g: (0, 0)),
                  pl.BlockSpec((8, 128), lambda b, g: (0, 0)),
                  pl.BlockSpec((1, dk), lambda b, g: (0, 0))],
        out_specs=[pl.BlockSpec((tg, N_HEADS * dk), lambda b, g: (b * steps + g, 0)),
                   pl.BlockSpec((sb, 8, cc), lambda b, g: (b, 0, 0)),
                   pl.BlockSpec((sb, N_HEADS, dk, dk), lambda b, g: (b, 0, 0, 0))],
        out_shape=[jax.ShapeDtypeStruct((n, N_HEADS * dk), F32),
                   jax.ShapeDtypeStruct((bn, 8, cc), F32),
                   jax.ShapeDtypeStruct((bn, N_HEADS, dk, dk), F32)],
        scratch_shapes=[pltpu.VMEM((sb * 8 + tg, cc), F32), pltpu.VMEM((tg, cc), F32),
                        pltpu.VMEM((tg, 128), F32), pltpu.VMEM((tg, 128), F32),
                        pltpu.VMEM((N_HEADS, dk, dk), F32)],
        compiler_params=_cparams(("parallel", "arbitrary")), name="gated_delta")(
            qkvz, qkvz, gates, conv0, s0, conv_w, hp, nw.reshape(1, dk))


def _mlstm_kernel(q_ref, k_ref, v_ref, z_ref, og_ref, gt_ref, c0_ref, n0_ref, m0_ref, hp_ref, nw_ref,
                  o_ref, co_ref, no_ref, mo_ref,
                  li_scr, lf_scr, cext, m_scr, *, c, nch, sb, tv, dqk, dv):
    gi = pl.program_id(1)
    tg = c * nch
    lane1 = _iota((dqk, 128), 1)

    @pl.when(gi == 0)
    def _():
        for s in range(sb):
            for h in range(N_HEADS):
                cext[s * N_HEADS + h, :, 0:dv] = c0_ref[s, h]
                cext[s * N_HEADS + h, :, dv:dv + 128] = jnp.where(lane1 == 0, n0_ref[s, h], 0.0)
            m_scr[s * 8:(s + 1) * 8, :] = m0_ref[s]

    hp = hp_ref[...]
    lane = _iota((tg, 128), 1)
    for s in range(sb):
        gt = gt_ref[s]
        li = gt + hp[0:1, :]
        x = gt + hp[1:2, :]
        lf = jnp.minimum(x, 0.0) - jnp.log(1.0 + jnp.exp(-jnp.abs(x)))
        lf = jnp.where(lane >= N_HEADS, jnp.where(lane < 2 * N_HEADS, lf, 0.0), 0.0)
        if tv < c:
            valid = (_iota((tg, 128), 0) % c) < tv
            lf = jnp.where(valid, lf, 0.0)
            li = jnp.where(valid, li, NEG)
        li_scr[s] = li
        lf_scr[s] = lf

    row = _iota((c, c), 0)
    col = _iota((c, c), 1)
    ige = row >= col
    tri = jnp.where(ige, 1.0, 0.0).astype(BF16)
    ones_b = jnp.ones((c, 128), BF16)
    lane_c = _iota((c, 128), 1)
    one_col = jnp.where(lane_c == 0, 1.0, 0.0)
    nw = nw_ref[...]

    def prep(clist):
        items = [(ci, h) for ci in range(len(clist)) for h in range(N_HEADS)]
        bcums = [_exact_dot(tri, lf_scr[s, pl.ds(r0, c), :]) for s, r0 in clist]
        lichs = [li_scr[s, pl.ds(r0, c), :] for s, r0 in clist]
        qbs = [(q_ref[clist[ci][0], pl.ds(clist[ci][1], c), h * dqk:(h + 1) * dqk] * (dqk ** -0.5)).astype(BF16)
               for ci, h in items]
        ks = [k_ref[clist[ci][0], pl.ds(clist[ci][1], c), h * dqk:(h + 1) * dqk] for ci, h in items]
        qk_raws = [_dot_nt(qb, k.astype(BF16)) for qb, k in zip(qbs, ks)]
        bcols = [bcums[ci][:, N_HEADS + h:N_HEADS + h + 1] for ci, h in items]
        licols = [lichs[ci][:, h:h + 1] for ci, h in items]
        rowvs = [_exact_dot_nt(ones_b, jnp.where(lane_c == 0, licol - bcol, 0.0))
                 for licol, bcol in zip(licols, bcols)]
        out = []
        for i, (ci, h) in enumerate(items):
            dmat = jnp.where(ige, bcols[i] + rowvs[i], NEG)
            v = v_ref[clist[ci][0], pl.ds(clist[ci][1], c), h * dv:(h + 1) * dv]
            blast = bcums[ci][c - 1:c, N_HEADS + h:N_HEADS + h + 1]
            out.append((qbs[i], ks[i], jnp.concatenate([v, one_col], axis=1).astype(BF16), bcols[i], dmat,
                        jnp.max(dmat, axis=-1, keepdims=True), qk_raws[i], blast, blast - bcols[i] + licols[i]))
        return [out[ci * N_HEADS:(ci + 1) * N_HEADS] for ci in range(len(clist))]

    def chunks(clist):
        hs = range(N_HEADS)
        for (s, r0), heads in zip(clist, prep(clist)):
            mprevs = [m_scr[s * 8:s * 8 + 1, h:h + 1] for h in hs]
            c_olds = [cext[s * N_HEADS + h] for h in hs]
            qcs = [_dot(heads[h][0], c_olds[h].astype(BF16)) for h in hs]
            inters = [heads[h][3] + mprevs[h] for h in hs]
            mts = [jnp.maximum(inters[h], heads[h][5]) for h in hs]
            mnews = [mts[h][c - 1:c, :] for h in hs]
            for h in hs:
                k, vext, blast, wk_log = heads[h][1], heads[h][2], heads[h][7], heads[h][8]
                cext[s * N_HEADS + h] = (c_olds[h] * jnp.exp(blast + mprevs[h] - mnews[h])
                                         + _dot_tn((k * jnp.exp(wk_log - mnews[h])).astype(BF16), vext))
                m_scr[s * 8:s * 8 + 1, h:h + 1] = mnews[h]
            for h in hs:
                qk = heads[h][6] * jnp.exp(heads[h][4] - mts[h])
                tot = jnp.exp(inters[h] - mts[h]) * qcs[h] + _dot(qk.astype(BF16), heads[h][2])
                hh = tot[:, :dv] / jnp.maximum(jnp.abs(tot[:, dv:dv + 1]), jnp.exp(-mts[h]))
                og = og_ref[s, pl.ds(r0, c), h * dv:(h + 1) * dv]
                zz = z_ref[s, pl.ds(r0, c), h * dv:(h + 1) * dv]
                o_ref[s, pl.ds(r0, c), h * dv:(h + 1) * dv] = _rms(_sigmoid(og) * hh, nw) * _silu(zz)

    chunks([(s, i * c) for i in range(nch) for s in range(sb)])

    @pl.when(gi == pl.num_programs(1) - 1)
    def _():
        for s in range(sb):
            for h in range(N_HEADS):
                co_ref[s, h] = cext[s * N_HEADS + h, :, 0:dv]
                no_ref[s, h] = cext[s * N_HEADS + h, :, dv:dv + 1]
            mo_ref[s] = m_scr[s * 8:(s + 1) * 8, :]


def _mlstm(q, k, v, z, og, gates, c0, c_idx, n0, m0, hp, nw, *, bn, c, nch, sb, tv):
    n = q.shape[0]
    dqk, dv = c0.shape[-2], c0.shape[-1]
    tg = c * nch
    rows = n // bn
    steps = rows // tg
    hq, hv = N_HEADS * dqk, N_HEADS * dv
    row = lambda b, g: (b, g, 0)
    q, k, v, z, og, gates = (a.reshape(bn, rows, a.shape[1]) for a in (q, k, v, z, og, gates))
    kern = functools.partial(_mlstm_kernel, c=c, nch=nch, sb=sb, tv=tv, dqk=dqk, dv=dv)
    out = pl.pallas_call(
        kern, grid=(bn // sb, steps),
        in_specs=[pl.BlockSpec((sb, tg, hq), row), pl.BlockSpec((sb, tg, hq), row),
                  pl.BlockSpec((sb, tg, hv), row), pl.BlockSpec((sb, tg, hv), row),
                  pl.BlockSpec((sb, tg, hv), row), pl.BlockSpec((sb, tg, 128), row),
                  pl.BlockSpec((None, sb, N_HEADS, dqk, dv), lambda b, g: (c_idx, b, 0, 0, 0)),
                  pl.BlockSpec((sb, N_HEADS, dqk, 1), lambda b, g: (b, 0, 0, 0)),
                  pl.BlockSpec((sb, 8, 128), lambda b, g: (b, 0, 0)),
                  pl.BlockSpec((8, 128), lambda b, g: (0, 0)),
                  pl.BlockSpec((1, dv), lambda b, g: (0, 0))],
        out_specs=[pl.BlockSpec((sb, tg, hv), row),
                   pl.BlockSpec((sb, N_HEADS, dqk, dv), lambda b, g: (b, 0, 0, 0)),
                   pl.BlockSpec((sb, N_HEADS, dqk, 1), lambda b, g: (b, 0, 0, 0)),
                   pl.BlockSpec((sb, 8, 128), lambda b, g: (b, 0, 0))],
        out_shape=[jax.ShapeDtypeStruct((bn, rows, hv), F32),
                   jax.ShapeDtypeStruct((bn, N_HEADS, dqk, dv), F32),
                   jax.ShapeDtypeStruct((bn, N_HEADS, dqk, 1), F32),
                   jax.ShapeDtypeStruct((bn, 8, 128), F32)],
        scratch_shapes=[pltpu.VMEM((sb, tg, 128), F32), pltpu.VMEM((sb, tg, 128), F32),
                        pltpu.VMEM((sb * N_HEADS, dqk, dv + 128), F32), pltpu.VMEM((sb * 8, 128), F32)],
        compiler_params=_cparams(("parallel", "arbitrary")), name="mlstm")(
            q, k, v, z, og, gates, c0, n0, m0, hp, nw.reshape(1, dv))
    return (out[0].reshape(n, hv),) + tuple(out[1:])


def _lambda(lq, lam_init):
    a = jnp.sum(lq[0:1, :] * lq[1:2, :], axis=-1, keepdims=True)
    b = jnp.sum(lq[2:3, :] * lq[3:4, :], axis=-1, keepdims=True)
    return jnp.exp(a) - jnp.exp(b) + lam_init


def _attn_kernel(wq_ref, k_ref, vt_ref, z_ref, lq_ref, nw_ref, o_ref, acc_scr, *, t, hpb, lam_init):
    qi = pl.program_id(2)
    dh = k_ref.shape[1] // hpb
    nl = 4 * t
    acc_scr[...] = jnp.zeros(acc_scr.shape, F32)

    def step(jp, carry, masked):
        rows = [pl.multiple_of((2 * jp + i) * t, t) for i in range(2)]
        wqs = [jnp.concatenate([wq_ref[0, a], wq_ref[1, a]], axis=1) for a in range(hpb)]
        scores = [[_dot(k_ref[pl.ds(rows[i], t), a * dh:(a + 1) * dh], wqs[a]) for i in range(2)]
                  for a in range(hpb)]
        out = []
        for a in range(hpb):
            m_prev, l_prev = carry[a]
            ss = scores[a]
            if masked:
                lane = _iota((t, nl), 1)
                qpos = (lane // (2 * t)) * t + lane % t
                kpos = _iota((t, nl), 0)
                ss = [jnp.where(kpos + i * t <= qpos, ss[i], NEG) for i in range(2)]
            m_new = jnp.maximum(m_prev, jnp.maximum(jnp.max(ss[0], axis=0, keepdims=True),
                                                    jnp.max(ss[1], axis=0, keepdims=True)))
            alpha = jnp.exp2(m_prev - m_new)
            ps = [jnp.exp2(s - m_new) for s in ss]
            l_new = alpha * l_prev + (jnp.sum(ps[0], axis=0, keepdims=True) + jnp.sum(ps[1], axis=0, keepdims=True))
            pv = (_dot(vt_ref[2 * jp, a * dh:(a + 1) * dh, :], ps[0].astype(BF16))
                  + _dot(vt_ref[2 * jp + 1, a * dh:(a + 1) * dh, :], ps[1].astype(BF16)))
            acc_scr[a] = alpha * acc_scr[a] + pv
            out.append((m_new, l_new))
        return tuple(out)

    init = tuple((jnp.full((1, nl), NEG, F32), jnp.zeros((1, nl), F32)) for _ in range(hpb))
    carry = lax.fori_loop(0, qi, lambda j, c: step(j, c, False), init)
    carry = step(qi, carry, True)

    lam = _lambda(lq_ref[...], lam_init)
    nw = nw_ref[...]
    for a in range(hpb):
        acc = acc_scr[a] * (1.0 / carry[a][1])
        for qb in range(2):
            b0 = 2 * t * qb
            o = (acc[:, b0:b0 + t] - lam * acc[:, b0 + t:b0 + 2 * t]).T
            zz = z_ref[qb * t:(qb + 1) * t, a * dh:(a + 1) * dh]
            o_ref[qb * t:(qb + 1) * t, a * dh:(a + 1) * dh] = _rms(o, nw) * (1.0 - lam_init) * _silu(zz)


def _attn_prompt(wq, k16, vt, z, lq, nw, *, bn, seq, lam_init):
    n = k16.shape[0]
    t = vt.shape[2]
    dh = wq.shape[2]
    hpb = 2
    nq = seq // t
    nq2 = nq // 2
    kern = functools.partial(_attn_kernel, t=t, hpb=hpb, lam_init=lam_init)
    return pl.pallas_call(
        kern, grid=(bn, N_HEADS // hpb, nq2),
        in_specs=[pl.BlockSpec((2, hpb, dh, 2 * t), lambda b, h, i: (b * nq2 + i, h, 0, 0)),
                  pl.BlockSpec((seq, hpb * dh), lambda b, h, i: (b, h)),
                  pl.BlockSpec((nq, hpb * dh, t), lambda b, h, i: (b, h, 0)),
                  pl.BlockSpec((2 * t, hpb * dh), lambda b, h, i: (b * nq2 + i, h)),
                  pl.BlockSpec(lq.shape, lambda b, h, i: (0, 0)),
                  pl.BlockSpec((1, dh), lambda b, h, i: (0, 0))],
        out_specs=pl.BlockSpec((2 * t, hpb * dh), lambda b, h, i: (b * nq2 + i, h)),
        out_shape=jax.ShapeDtypeStruct((n, N_HEADS * dh), F32),
        scratch_shapes=[pltpu.VMEM((hpb, dh, 4 * t), F32)],
        compiler_params=_cparams(("parallel", "parallel", "arbitrary")), name="diff_attn_prompt")(
            wq, k16, vt, z, lq, nw.reshape(1, dh))


def _attn_s_kernel(pt_ref, q_ref, kn_ref, vn_ref, z_ref, lq_ref, nw_ref, *rest, pg, tv, lam_init):
    k_pages = rest[:pg]
    v_pages = rest[pg:2 * pg]
    o_ref, qbd, bias, m_scr, l_scr, acc_scr = rest[2 * pg:]
    j = pl.program_id(1)
    tp, hd = q_ref.shape
    dh = hd // N_HEADS
    dqk = dh // 2
    prow = k_pages[0].shape[0]
    rph = 2 * tp

    @pl.when(j == 0)
    def _():
        q = q_ref[...] * (dqk ** -0.5)
        lane = _iota((tp, dh), 1)
        for h in range(N_HEADS):
            qh = q[:, h * dh:(h + 1) * dh]
            qbd[h * rph:h * rph + tp, :] = jnp.where(lane < dqk, qh, 0.0)
            qbd[h * rph + tp:(h + 1) * rph, :] = jnp.where(lane >= dqk, qh, 0.0)
        rhead = _iota(bias.shape, 0) // rph
        chead = _iota(bias.shape, 1) % N_HEADS
        bias[...] = jnp.where(rhead == chead, 0.0, NEG)
        m_scr[...] = jnp.full(m_scr.shape, NEG, F32)
        l_scr[...] = jnp.zeros(l_scr.shape, F32)
        acc_scr[...] = jnp.zeros(acc_scr.shape, F32)

    def update(s, pv):
        m_prev = m_scr[...]
        m_new = jnp.maximum(m_prev, jnp.max(s, axis=-1, keepdims=True))
        alpha = jnp.exp(m_prev - m_new)
        p = jnp.exp(s - m_new[:, 0:1])
        l_scr[...] = alpha * l_scr[...] + jnp.sum(p, axis=-1, keepdims=True)
        acc_scr[...] = alpha * acc_scr[...] + pv(p)
        m_scr[...] = m_new

    qb = qbd[...].astype(BF16)
    bs_ = bias[...]
    s = jnp.concatenate([_dot_nt(qb, kp[...].astype(BF16)) + bs_ for kp in k_pages], axis=1)

    def pv_pages(p):
        pb = p.astype(BF16)
        acc = _dot(pb[:, 0:prow], v_pages[0][...].astype(BF16))
        for i in range(1, pg):
            acc = acc + _dot(pb[:, i * prow:(i + 1) * prow], v_pages[i][...].astype(BF16))
        return acc

    update(s, pv_pages)

    @pl.when(j == pl.num_programs(1) - 1)
    def _():
        qf = qbd[...]
        kn = kn_ref[...]
        vn = vn_ref[...]
        sn = jnp.concatenate([_dot_nt(qf[h * rph:(h + 1) * rph], kn[:, h * dh:(h + 1) * dh])
                              for h in range(N_HEADS)], axis=0)
        tq = _iota(sn.shape, 0) % tp
        tk = _iota(sn.shape, 1)
        sn = jnp.where(tk <= tq, jnp.where(tk < tv, sn, NEG), NEG)
        update(sn, lambda p: jnp.concatenate(
            [_dot(p[h * rph:(h + 1) * rph], vn[:, h * dh:(h + 1) * dh]) for h in range(N_HEADS)], axis=0))
        acc = acc_scr[...] / l_scr[...]
        lam = _lambda(lq_ref[...], lam_init)
        nw = nw_ref[...]
        normed = jnp.concatenate(
            [_rms(acc[h * rph:h * rph + tp] - lam * acc[h * rph + tp:(h + 1) * rph], nw) for h in range(N_HEADS)],
            axis=1)
        o_ref[...] = normed * (1.0 - lam_init) * _silu(z_ref[...])


def _attn_sample(page_table, q, kn, vn, z, lq, nw, cache_k, cache_v, *, e, tv, lam_init):
    bs, npages = page_table.shape
    n, hd = q.shape
    tp = n // bs
    dh = hd // N_HEADS
    pg = math.gcd(npages, PAGES_PER_STEP)
    prow = cache_k.shape[2]
    steps = npages // pg
    rows = 2 * N_HEADS * tp
    tok = lambda b, j, pt: (b, 0)
    page_specs = [pl.BlockSpec((None, None, prow, dh),
                               lambda b, j, pt, i=i: (e, pt[b, j * pg + i], 0, 0)) for i in range(pg)]
    kern = functools.partial(_attn_s_kernel, pg=pg, tv=tv, lam_init=lam_init)
    grid_spec = pltpu.PrefetchScalarGridSpec(
        num_scalar_prefetch=1, grid=(bs, steps),
        in_specs=[pl.BlockSpec((tp, hd), tok), pl.BlockSpec((tp, hd), tok), pl.BlockSpec((tp, hd), tok),
                  pl.BlockSpec((tp, hd), tok),
                  pl.BlockSpec(lq.shape, lambda b, j, pt: (0, 0)),
                  pl.BlockSpec((1, dh), lambda b, j, pt: (0, 0))] + page_specs + page_specs,
        out_specs=pl.BlockSpec((tp, hd), tok),
        scratch_shapes=[pltpu.VMEM((rows, dh), F32), pltpu.VMEM((rows, prow), F32), pltpu.VMEM((rows, 128), F32),
                        pltpu.VMEM((rows, 128), F32), pltpu.VMEM((rows, dh), F32)])
    return pl.pallas_call(
        kern, grid_spec=grid_spec, out_shape=jax.ShapeDtypeStruct((n, hd), F32),
        compiler_params=_cparams(("parallel", "arbitrary")), name="diff_attn_sample")(
            page_table, q, kn, vn, z, lq, nw.reshape(1, dh), *([cache_k] * pg), *([cache_v] * pg))


def _rope_tables(pos, dqk):
    rope_dim = dqk // 4
    half = rope_dim // 2
    inv = jnp.power(ROPE_THETA, -jnp.arange(half, dtype=F32) / half)
    ang = pos.astype(F32)[:, None] * inv[None, :]
    lane = jnp.arange(128) % dqk
    idx = lane % half
    cos = jnp.where(lane < rope_dim, jnp.cos(ang)[:, idx], 1.0)
    sin = jnp.sin(ang)[:, idx]
    sa = jnp.where((lane >= half) & (lane < rope_dim), sin, 0.0)
    sb = jnp.where(lane < half, -sin, 0.0)
    return cos.astype(F32), sa.astype(F32), sb.astype(F32)


def _lane_row(vals, offset):
    out = jnp.zeros((128,), F32)
    return out.at[offset:offset + vals.shape[0]].set(vals.astype(F32))


def kernel(x_prompt, x_sample, cache_k, cache_v, page_table, state_gdn_conv, state_gdn_s, state_mlstm_c,
           state_mlstm_n, state_mlstm_m, norm_w, final_norm_w, w_in_even, w_out_even, conv_w, a_log, dt_bias,
           gdn_norm_w, lambda_qk, diff_norm_w, w_in_odd, w_out_odd, b_i, b_f, mlstm_norm_w):
    bp, seq, d = x_prompt.shape
    bs, ts, _ = x_sample.shape
    depth = norm_w.shape[0]
    tp = SAMPLE_PAD
    dk_a = state_gdn_s.shape[-1]
    conv_ch = state_gdn_conv.shape[-1]
    w_a = N_HEADS * dk_a
    dh_b = cache_v.shape[-1]
    dqk_b = dh_b // 2
    w_b = N_HEADS * dh_b
    dqk_c, dv_c = state_mlstm_c.shape[-2], state_mlstm_c.shape[-1]
    hq_c, w_c = N_HEADS * dqk_c, N_HEADS * dv_c
    past = page_table.shape[1] * cache_k.shape[2]
    n_pool, page = cache_k.shape[1], cache_k.shape[2]

    xp = x_prompt.reshape(bp * seq, d)
    xs = jnp.pad(x_sample, ((0, 0), (0, tp - ts), (0, 0))).reshape(bs * tp, d)
    tm_p = math.gcd(seq, 256)
    tm_s = math.gcd(bs * tp, 256)
    nch_p = math.gcd(seq // CHUNK, 4)
    seq_s = math.gcd(bs, 4)

    rope_p = _rope_tables(jnp.arange(seq), dqk_b)
    rope_s = _rope_tables(past + (jnp.arange(tm_s) % tp), dqk_b)
    half = dqk_b // 8
    ang_t = (jnp.power(ROPE_THETA, -jnp.arange(half, dtype=F32) / half)[:, None]
             * jnp.arange(seq).astype(F32)[None, :])
    cos_t, sin_t = jnp.cos(ang_t), jnp.sin(ang_t)

    even_w = conv_ch + w_a
    seg_even = [(0, even_w), (even_w, w_b), (even_w + w_b, w_b), (even_w + 2 * w_b, w_b),
                (even_w + 3 * w_b, w_b), (even_w + 4 * w_b, 128)]
    seg_even_p = [seg_even[0]] + seg_even[2:]
    rope_even = tuple(range(even_w // 128, (even_w + 2 * w_b) // 128))
    odd_w = 2 * hq_c + 3 * w_c
    seg_odd = [(0, hq_c), (hq_c, hq_c), (2 * hq_c, w_c), (2 * hq_c + w_c, w_c), (2 * hq_c + 2 * w_c, w_c),
               (odd_w, 128)]

    ck = cache_k.reshape(cache_k.shape[0], n_pool, page * N_HEADS, dh_b)
    cv = cache_v.reshape(cache_v.shape[0], n_pool, page * N_HEADS, dh_b)

    n_even = (depth + 1) // 2
    krows_p = jnp.zeros((n_even, bp * seq * N_HEADS, dh_b), F32)
    vrows_p = jnp.zeros((n_even, bp * seq * N_HEADS, dh_b), F32)
    k_rows_s, v_rows_s = [], []
    conv_p, s_p, conv_s, s_s = [], [], [], []
    c_p, n_p, m_p, c_s, n_s, m_s = [], [], [], [], [], []

    for layer in range(depth):
        last = layer == depth - 1
        if layer % 2 == 0:
            e = layer // 2
            lam_init = 0.8 - 0.6 * math.exp(-0.3 * layer)
            w = w_in_even[e]
            g0 = even_w
            w_r = jnp.concatenate([w[:, :g0], w[:, g0 + 2 * N_HEADS:], w[:, g0:g0 + 2 * N_HEADS],
                                   jnp.zeros((d, 128 - 2 * N_HEADS), F32)], axis=1).astype(BF16)
            w_o = w_out_even[e].astype(BF16)
            hp = jnp.zeros((8, 128), F32).at[0].set(_lane_row(a_log[e], N_HEADS)).at[1].set(
                _lane_row(dt_bias[e], N_HEADS))
            lq = lambda_qk[e]

            wq_t = w[:, g0 + 2 * N_HEADS:g0 + 2 * N_HEADS + w_b].T.astype(BF16)
            wv_t = w[:, g0 + 2 * N_HEADS + 2 * w_b:g0 + 2 * N_HEADS + 3 * w_b].T.astype(BF16)
            az, krows_p, vrows_p, zb, gt, k16, wq, vt = _proj(
                xp, norm_w[layer], w_r, seg_even_p, (rope_even, rope_p), tm_p, k16_seg=1,
                feat=(wq_t, wv_t, cos_t, sin_t), rows=((1, 2), e, (krows_p, vrows_p)))
            oa, cst, sst = _gdn(az, gt, jnp.zeros((bp, 8, conv_ch), F32),
                                jnp.zeros((1, bp, N_HEADS, dk_a, dk_a), F32), 0,
                                conv_w[e], hp, gdn_norm_w[e], bn=bp, c=CHUNK, nch=nch_p, tv=CHUNK)
            ob = _attn_prompt(wq, k16, vt, zb, lq, diff_norm_w[e], bn=bp, seq=seq, lam_init=lam_init)
            xp = _out_proj(oa, ob, w_a, 0, w_o, xp, final_norm_w, last, tm_p)
            conv_p.append(cst[:, 8 - (CONV_W - 1):])
            s_p.append(sst)

            az, qb, kb, vb, zb, gt = _proj(xs, norm_w[layer], w_r, seg_even, (rope_even, rope_s), tm_s)
            conv0 = jnp.pad(state_gdn_conv[e], ((0, 0), (8 - (CONV_W - 1), 0), (0, 0)))
            oa, cst, sst = _gdn(az, gt, conv0, state_gdn_s, e, conv_w[e], hp, gdn_norm_w[e],
                                bn=bs, c=tp, nch=seq_s, tv=ts, per_seq=True)
            ob = _attn_sample(page_table, qb, kb, vb, zb, lq, diff_norm_w[e], ck, cv, e=e, tv=ts,
                              lam_init=lam_init)
            xs = _out_proj(oa, ob, w_a, 0, w_o, xs, final_norm_w, last, tm_s)
            k_rows_s.append(kb.reshape(bs, tp, N_HEADS, dh_b)[:, :ts])
            v_rows_s.append(vb.reshape(bs, tp, N_HEADS, dh_b)[:, :ts])
            conv_s.append(cst[:, 8 - (CONV_W - 1):])
            s_s.append(sst)
        else:
            o = layer // 2
            w_r = jnp.concatenate([w_in_odd[o], jnp.zeros((d, 128 - 2 * N_HEADS), F32)], axis=1).astype(BF16)
            w_o = w_out_odd[o].astype(BF16)
            hp = jnp.zeros((8, 128), F32).at[0].set(_lane_row(b_i[o], 0)).at[1].set(_lane_row(b_f[o], N_HEADS))

            q, k, v, z, og, gt = _proj(xp, norm_w[layer], w_r, seg_odd, None, tm_p)
            hm, cf, nf, mf = _mlstm(q, k, v, z, og, gt, jnp.zeros((1, bp, N_HEADS, dqk_c, dv_c), F32), 0,
                                    jnp.zeros((bp, N_HEADS, dqk_c, 1), F32), jnp.zeros((bp, 8, 128), F32),
                                    hp, mlstm_norm_w[o], bn=bp, c=CHUNK, nch=math.gcd(seq // CHUNK, 2),
                                    sb=1, tv=CHUNK)
            xp = _out_proj(hm, hm, w_c // 2, 1, w_o, xp, final_norm_w, last, tm_p)
            c_p.append(cf)
            n_p.append(nf[..., 0])
            m_p.append(mf[:, 0, :N_HEADS])

            q, k, v, z, og, gt = _proj(xs, norm_w[layer], w_r, seg_odd, None, tm_s)
            m0 = jnp.zeros((bs, 8, 128), F32).at[:, 0, :N_HEADS].set(state_mlstm_m[o])
            hm, cf, nf, mf = _mlstm(q, k, v, z, og, gt, state_mlstm_c, o, state_mlstm_n[o][..., None], m0,
                                    hp, mlstm_norm_w[o], bn=bs, c=tp, nch=1, sb=seq_s, tv=ts)
            xs = _out_proj(hm, hm, w_c // 2, 1, w_o, xs, final_norm_w, last, tm_s)
            c_s.append(cf)
            n_s.append(nf[..., 0])
            m_s.append(mf[:, 0, :N_HEADS])

    y_prompt = xp.reshape(bp, seq, d)
    y_sample = xs.reshape(bs, tp, d)[:, :ts]
    st = jnp.stack
    kv_shape = (n_even, bp, seq, N_HEADS, dh_b)
    return (y_prompt, y_sample, krows_p.reshape(kv_shape), vrows_p.reshape(kv_shape), st(k_rows_s), st(v_rows_s),
            st(conv_p), st(s_p), st(conv_s), st(s_s),
            st(c_p), st(n_p), st(m_p), st(c_s), st(n_s), st(m_s))
```

```python
import functools
import math

import jax
import jax.numpy as jnp
from jax import lax
from jax.experimental import pallas as pl
from jax.experimental.pallas import tpu as pltpu

F32 = jnp.float32
BF16 = jnp.bfloat16

EPS = 1e-6
NEG = -1e30
N_HEADS = 4
CONV_W = 4
ROPE_THETA = 500000.0
CHUNK = 64
SAMPLE_PAD = 8
PAGES_PER_STEP = 16
V7X_VMEM_LIMIT = 56 * 1024 * 1024


def _cparams(sem):
    return pltpu.CompilerParams(dimension_semantics=sem, vmem_limit_bytes=V7X_VMEM_LIMIT)


def _dot(a, b):
    return jnp.dot(a, b, preferred_element_type=F32)


def _dot_nt(a, b):
    return lax.dot_general(a, b, (((1,), (1,)), ((), ())), preferred_element_type=F32)


def _dot_tn(a, b):
    return lax.dot_general(a, b, (((0,), (0,)), ((), ())), preferred_element_type=F32)


def _split3(x):
    hi = x.astype(BF16)
    r = x - hi.astype(F32)
    mid = r.astype(BF16)
    lo = (r - mid.astype(F32)).astype(BF16)
    return hi, mid, lo


def _exact_dot(a_bf16, x):
    hi, mid, lo = _split3(x)
    return _dot(a_bf16, hi) + (_dot(a_bf16, mid) + _dot(a_bf16, lo))


def _exact_dot_nt(a_bf16, x):
    hi, mid, lo = _split3(x)
    return _dot_nt(a_bf16, hi) + (_dot_nt(a_bf16, mid) + _dot_nt(a_bf16, lo))


def _sigmoid(x):
    return 1.0 / (1.0 + jnp.exp(-x))


def _silu(x):
    return x * _sigmoid(x)


def _softplus(x):
    return jnp.maximum(x, 0.0) + jnp.log(1.0 + jnp.exp(-jnp.abs(x)))


def _rms(x, w):
    return x * lax.rsqrt(jnp.mean(x * x, axis=-1, keepdims=True) + EPS) * w


def _iota(shape, dim):
    return lax.broadcasted_iota(jnp.int32, shape, dim)


def _proj_kernel(*refs, segs, rope_blocks, k16_seg, feat_major, row_segs, fuse_out):
    n_out = len(segs)
    pos = 3
    x_ref, nw_ref, w_ref = refs[:3]
    if rope_blocks:
        cos_ref, sa_ref, sb_ref = refs[pos:pos + 3]
        pos += 3
    if feat_major:
        wqt_ref, wvt_ref, cost_ref, sint_ref = refs[pos:pos + 4]
        pos += 4
    if fuse_out:
        a_ref, b_ref, wo_ref = refs[pos:pos + 3]
        pos += 3
    pos += len(row_segs)
    outs = refs[pos:pos + n_out]
    pos += n_out
    x = x_ref[...]
    if fuse_out:
        ka = a_ref.shape[1]
        x = x + (_dot(a_ref[...].astype(BF16), wo_ref[0:ka, :]) + _dot(b_ref[...].astype(BF16), wo_ref[ka:, :]))
        refs[-1][...] = x
    h = _rms(x, nw_ref[...]).astype(BF16)
    for si, ((c0, width), o_ref) in enumerate(zip(segs, outs)):
        for s0 in range(0, width, 512):
            sw = min(512, width - s0)
            acc = _dot(h, w_ref[:, c0 + s0:c0 + s0 + sw])
            if rope_blocks and (c0 + s0) // 128 in rope_blocks:
                cos = cos_ref[...]
                sa = sa_ref[...]
                sb = sb_ref[...]
                parts = []
                for t0 in range(0, sw, 128):
                    a = acc[:, t0:t0 + 128]
                    parts.append(a * cos + pltpu.roll(a, 8, 1) * sa + pltpu.roll(a, 120, 1) * sb)
                acc = jnp.concatenate(parts, axis=1)
            if si in row_segs:
                dh = sw // N_HEADS
                for hd in range(N_HEADS):
                    o_ref[pl.ds(hd, acc.shape[0], stride=N_HEADS), :] = acc[:, hd * dh:(hd + 1) * dh]
            else:
                o_ref[:, s0:s0 + sw] = acc
            if si == k16_seg:
                refs[pos][:, s0:s0 + sw] = acc.astype(BF16)
    if feat_major:
        wq_ref, vt_ref = refs[pos + 1], refs[pos + 2]
        tm = x.shape[0]
        vt_ref[0] = _dot_nt(wvt_ref[...], h).astype(BF16)
        qt = _dot_nt(wqt_ref[...], h)
        dqk = qt.shape[0] // (2 * N_HEADS)
        half = dqk // 8
        cos = cost_ref[...]
        sin = sint_ref[...]
        scale = (dqk ** -0.5) * math.log2(math.e)
        zero = jnp.zeros((dqk, tm), F32)
        for hd in range(N_HEADS):
            maps = []
            for m in range(2):
                r0 = (2 * hd + m) * dqk
                x1 = qt[r0:r0 + half]
                x2 = qt[r0 + half:r0 + 2 * half]
                maps.append(jnp.concatenate([x1 * cos - x2 * sin, x2 * cos + x1 * sin,
                                             qt[r0 + 2 * half:r0 + dqk]], axis=0) * scale)
            top = jnp.concatenate([maps[0], zero], axis=1)
            bot = jnp.concatenate([zero, maps[1]], axis=1)
            wq_ref[0, hd] = jnp.concatenate([top, bot], axis=0).astype(BF16)


def _proj(x, nw, w, segs, rope, tm, k16_seg=None, feat=None, rows=None, pre=None):
    n, d = x.shape
    ncols = w.shape[1]
    grid = (n // tm,)
    in_specs = [pl.BlockSpec((tm, d), lambda i: (i, 0)),
                pl.BlockSpec((1, d), lambda i: (0, 0)),
                pl.BlockSpec((d, ncols), lambda i: (0, 0))]
    args = [x, nw.reshape(1, d), w]
    rope_blocks = ()
    if rope is not None:
        rope_blocks, tables = rope
        nt = tables[0].shape[0] // tm
        for t in tables:
            in_specs.append(pl.BlockSpec((tm, 128), lambda i, nt=nt: (i % nt, 0)))
            args.append(t)
    out_shape = [jax.ShapeDtypeStruct((n, width), F32) for _, width in segs]
    out_specs = [pl.BlockSpec((tm, width), lambda i: (i, 0)) for _, width in segs]
    if feat is not None:
        wqt, wvt, cost, sint = feat
        ntt = cost.shape[1] // tm
        in_specs += [pl.BlockSpec(wqt.shape, lambda i: (0, 0)), pl.BlockSpec(wvt.shape, lambda i: (0, 0)),
                     pl.BlockSpec((cost.shape[0], tm), lambda i, ntt=ntt: (0, i % ntt)),
                     pl.BlockSpec((sint.shape[0], tm), lambda i, ntt=ntt: (0, i % ntt))]
        args += [wqt, wvt, cost, sint]
    if pre is not None:
        a, b, ka, b_col, w_out = pre
        kb = w_out.shape[0] - ka
        in_specs += [pl.BlockSpec((tm, ka), lambda i: (i, 0)), pl.BlockSpec((tm, kb), lambda i: (i, b_col)),
                     pl.BlockSpec(w_out.shape, lambda i: (0, 0))]
        args += [a, b, w_out]
    row_segs, aliases = (), {}
    if rows is not None:
        row_segs, layer, stacked = rows
        for si, arr in zip(row_segs, stacked):
            aliases[len(args)] = si
            in_specs.append(pl.BlockSpec(memory_space=pl.ANY))
            args.append(arr)
            out_shape[si] = jax.ShapeDtypeStruct(arr.shape, F32)
            out_specs[si] = pl.BlockSpec((None, tm * N_HEADS, arr.shape[2]), lambda i, layer=layer: (layer, i, 0))
    if k16_seg is not None:
        kw = segs[k16_seg][1]
        out_shape.append(jax.ShapeDtypeStruct((n, kw), BF16))
        out_specs.append(pl.BlockSpec((tm, kw), lambda i: (i, 0)))
    if feat is not None:
        hq, hv = wqt.shape[0], wvt.shape[0]
        dh = hq // N_HEADS
        out_shape += [jax.ShapeDtypeStruct((n // tm, N_HEADS, dh, 2 * tm), BF16),
                      jax.ShapeDtypeStruct((n // tm, hv, tm), BF16)]
        out_specs += [pl.BlockSpec((1, N_HEADS, dh, 2 * tm), lambda i: (i, 0, 0, 0)),
                      pl.BlockSpec((1, hv, tm), lambda i: (i, 0, 0))]
    if pre is not None:
        out_shape.append(jax.ShapeDtypeStruct((n, d), F32))
        out_specs.append(pl.BlockSpec((tm, d), lambda i: (i, 0)))
    return list(pl.pallas_call(
        functools.partial(_proj_kernel, segs=tuple(segs), rope_blocks=tuple(rope_blocks), k16_seg=k16_seg,
                          feat_major=feat is not None, row_segs=tuple(row_segs), fuse_out=pre is not None),
        grid=grid, in_specs=in_specs, out_specs=out_specs, out_shape=out_shape, input_output_aliases=aliases,
        compiler_params=_cparams(("parallel",)), name="norm_proj")(*args))


def _out_kernel(a_ref, b_ref, w_ref, x_ref, fw_ref, o_ref, *, final):
    ka = a_ref.shape[1]
    y = _dot(a_ref[...].astype(BF16), w_ref[0:ka, :]) + _dot(b_ref[...].astype(BF16), w_ref[ka:, :])
    xn = x_ref[...] + y
    if final:
        xn = _rms(xn, fw_ref[...])
    o_ref[...] = xn


def _out_proj(a, b, ka, b_col, w, x, fw, final, tm):
    n, d = x.shape
    kb = w.shape[0] - ka
    return pl.pallas_call(
        functools.partial(_out_kernel, final=final),
        grid=(n // tm,),
        in_specs=[pl.BlockSpec((tm, ka), lambda i: (i, 0)),
                  pl.BlockSpec((tm, kb), lambda i: (i, b_col)),
                  pl.BlockSpec(w.shape, lambda i: (0, 0)),
                  pl.BlockSpec((tm, d), lambda i: (i, 0)),
                  pl.BlockSpec((1, d), lambda i: (0, 0))],
        out_specs=pl.BlockSpec((tm, d), lambda i: (i, 0)),
        out_shape=jax.ShapeDtypeStruct((n, d), F32),
        compiler_params=_cparams(("parallel",)), name="out_proj")(a, b, w, x, fw.reshape(1, d))


def _neumann(a_list, c):
    eye = jnp.where(_iota((c, c), 0) == _iota((c, c), 1), 1.0, 0.0)
    ts = [eye - a for a in a_list]
    abs_ = [a.astype(BF16) for a in a_list]
    ms = [_dot(ab, ab) for ab in abs_]
    levels = int(math.log2(c)) - 1
    for k in range(levels):
        mbs = [m.astype(BF16) for m in ms]
        ts = [t + _dot(mb, t.astype(BF16)) for mb, t in zip(mbs, ts)]
        if k + 1 < levels:
            ms = [_dot(mb, mb) for mb in mbs]
    return ts


def _gdn_kernel(x_ref, z_ref, gt_ref, conv0_ref, s0_ref, cw_ref, hp_ref, nw_ref,
                o_ref, convo_ref, so_ref,
                xbuf, qkv_scr, g_scr, b_scr, s_scr, *, c, nch, tv, dk, unroll, per_seq):
    gi = pl.program_id(1)
    tg = c * nch
    hq = N_HEADS * dk
    cw = cw_ref[...]

    def conv(base, rows):
        return _silu((xbuf[base + 5:base + 5 + rows, :] * cw[0:1, :] + xbuf[base + 6:base + 6 + rows, :] * cw[1:2, :])
                     + (xbuf[base + 7:base + 7 + rows, :] * cw[2:3, :] + xbuf[base + 8:base + 8 + rows, :] * cw[3:4, :]))

    if per_seq:
        for s in range(nch):
            base = s * (8 + c)
            xbuf[base:base + 8, :] = conv0_ref[s]
            xbuf[base + 8:base + 8 + c, :] = x_ref[s * c:(s + 1) * c, :]
            qkv_scr[s * c:(s + 1) * c, :] = conv(base, c)
            convo_ref[s, 0:8 - (CONV_W - 1), :] = jnp.zeros((8 - (CONV_W - 1), xbuf.shape[1]), F32)
            convo_ref[s, 8 - (CONV_W - 1):8, :] = xbuf[base + 8 + tv - (CONV_W - 1):base + 8 + tv, :]
    else:
        @pl.when(gi == 0)
        def _():
            xbuf[0:8, :] = conv0_ref[0]
            s_scr[...] = s0_ref[0]

        xbuf[8:8 + tg, :] = x_ref[...]
        qkv_scr[...] = conv(0, tg)
        xbuf[5:8, :] = xbuf[8 + tg - 3:8 + tg, :]

    gt = gt_ref[...]
    hp = hp_ref[...]
    lane = _iota((tg, 128), 1)
    beta = _sigmoid(gt)
    g = -jnp.exp(hp[0:1, :]) * _softplus(gt + hp[1:2, :])
    g = jnp.where(lane >= N_HEADS, jnp.where(lane < 2 * N_HEADS, g, 0.0), 0.0)
    if tv < c:
        valid = (_iota((tg, 128), 0) % c) < tv
        g = jnp.where(valid, g, 0.0)
        beta = jnp.where(valid, beta, 0.0)
    g_scr[...] = g
    b_scr[...] = beta

    row = _iota((c, c), 0)
    col = _iota((c, c), 1)
    ige = row >= col
    igt = row > col
    tri = jnp.where(ige, 1.0, 0.0).astype(BF16)
    lane_c = _iota((c, 128), 1)
    nw = nw_ref[...]

    sels = [jnp.where(lane_c == N_HEADS + h, 1.0, 0.0).astype(BF16) for h in range(N_HEADS)]

    def prep(starts):
        items = [(ci, h) for ci in range(len(starts)) for h in range(N_HEADS)]
        gcums = [_exact_dot(tri, g_scr[pl.ds(r0, c), :]) for r0 in starts]
        bchs = [b_scr[pl.ds(r0, c), :] for r0 in starts]
        qns, kns, vhs = [], [], []
        for ci, h in items:
            r0 = starts[ci]
            qh = qkv_scr[pl.ds(r0, c), h * dk:(h + 1) * dk]
            kh = qkv_scr[pl.ds(r0, c), hq + h * dk:hq + (h + 1) * dk]
            vhs.append(qkv_scr[pl.ds(r0, c), 2 * hq + h * dk:2 * hq + (h + 1) * dk])
            qns.append(qh * lax.rsqrt(jnp.sum(qh * qh, axis=-1, keepdims=True) + EPS) * (dk ** -0.5))
            kns.append(kh * lax.rsqrt(jnp.sum(kh * kh, axis=-1, keepdims=True) + EPS))
        qkks = [_dot_nt(jnp.concatenate([qn, kn], axis=0).astype(BF16), kn.astype(BF16))
                for qn, kn in zip(qns, kns)]
        grows = [_exact_dot_nt(sels[h], gcums[ci]) for ci, h in items]
        gcols = [gcums[ci][:, N_HEADS + h:N_HEADS + h + 1] for ci, h in items]
        bcols = [bchs[ci][:, h:h + 1] for ci, h in items]
        decays = [jnp.where(ige, jnp.exp(jnp.where(ige, gcol - grow, 0.0)), 0.0)
                  for gcol, grow in zip(gcols, grows)]
        t_invs = _neumann([jnp.where(igt, qkk[c:] * decay * bcol, 0.0)
                           for qkk, decay, bcol in zip(qkks, decays, bcols)], c)
        egs = [jnp.exp(gcol) for gcol in gcols]
        sols = [_dot(t_inv.astype(BF16),
                     jnp.concatenate([vh * bcol, kn * (bcol * eg)], axis=1).astype(BF16))
                for t_inv, vh, kn, bcol, eg in zip(t_invs, vhs, kns, bcols, egs)]
        glasts = [gcums[ci][c - 1:c, N_HEADS + h:N_HEADS + h + 1] for ci, h in items]
        kws = [(kn * jnp.exp(glast - gcol)).astype(BF16) for kn, glast, gcol in zip(kns, glasts, gcols)]
        aqks = [(qkk[:c] * decay).astype(BF16) for qkk, decay in zip(qkks, decays)]
        sol_bs = [sol.astype(BF16) for sol in sols]
        ktus = [_dot_tn(kw, sol_b) for kw, sol_b in zip(kws, sol_bs)]
        aus = [_dot(aqk, sol_b) for aqk, sol_b in zip(aqks, sol_bs)]
        out = []
        for i in range(len(items)):
            lhs = jnp.concatenate([-ktus[i][:, dk:], qns[i] * egs[i] - aus[i][:, dk:]], axis=0).astype(BF16)
            out.append((lhs, ktus[i][:, :dk], aus[i][:, :dk], jnp.exp(glasts[i])))
        return [out[ci * N_HEADS:(ci + 1) * N_HEADS] for ci in range(len(starts))]

    def chunks(starts, seq_ids):
        preps = prep(starts)
        for r0, sid, heads in zip(starts, seq_ids, preps):
            s_olds = [s_scr[h] if sid is None else s0_ref[sid, h] for h in range(N_HEADS)]
            res = [_dot(heads[h][0], s_olds[h].astype(BF16)) for h in range(N_HEADS)]
            for h in range(N_HEADS):
                s_new = s_olds[h] * heads[h][3] + (heads[h][1] + res[h][:dk])
                if sid is None:
                    s_scr[h] = s_new
                else:
                    so_ref[sid, h] = s_new
            for h in range(N_HEADS):
                zz = z_ref[pl.ds(r0, c), h * dk:(h + 1) * dk]
                o = heads[h][2] + res[h][dk:]
                o_ref[pl.ds(r0, c), h * dk:(h + 1) * dk] = _rms(o, nw) * _silu(zz)

    if per_seq:
        chunks([s * c for s in range(nch)], list(range(nch)))
    else:
        def body(it, carry):
            chunks([pl.multiple_of((it * unroll + i) * c, c) for i in range(unroll)], [None] * unroll)
            return carry

        lax.fori_loop(0, nch // unroll, body, 0)

        @pl.when(gi == pl.num_programs(1) - 1)
        def _():
            convo_ref[0] = xbuf[0:8, :]
            so_ref[0] = s_scr[...]


def _gdn(qkvz, gates, conv0, s0, s_idx, conv_w, hp, nw, *, bn, c, nch, tv, per_seq=False):
    n = qkvz.shape[0]
    dk = s0.shape[-1]
    cc = 3 * N_HEADS * dk
    tg = c * nch
    sb = nch if per_seq else 1
    steps = 1 if per_seq else n // (bn * tg)
    kern = functools.partial(_gdn_kernel, c=c, nch=nch, tv=tv, dk=dk, unroll=math.gcd(nch, 4), per_seq=per_seq)
    return pl.pallas_call(
        kern, grid=(bn // sb, steps),
        in_specs=[pl.BlockSpec((tg, cc), lambda b, g: (b * steps + g, 0)),
                  pl.BlockSpec((tg, N_HEADS * dk), lambda b, g: (b * steps + g, 3)),
                  pl.BlockSpec((tg, 128), lambda b, g: (b * steps + g, 0)),
                  pl.BlockSpec((sb, 8, cc), lambda b, g: (b, 0, 0)),
                  pl.BlockSpec((None, sb, N_HEADS, dk, dk), lambda b, g: (s_idx, b, 0, 0, 0)),
                  pl.BlockSpec((CONV_W, cc), lambda b, g: (0, 0)),
                  pl.BlockSpec((8, 128), lambda b, g: (0, 0)),
                  pl.BlockSpec((1, dk), lambda b, g: (0, 0))],
        out_specs=[pl.BlockSpec((tg, N_HEADS * dk), lambda b, g: (b * steps + g, 0)),
                   pl.BlockSpec((sb, 8, cc), lambda b, g: (b, 0, 0)),
                   pl.BlockSpec((sb, N_HEADS, dk, dk), lambda b, g: (b, 0, 0, 0))],
        out_shape=[jax.ShapeDtypeStruct((n, N_HEADS * dk), F32),
                   jax.ShapeDtypeStruct((bn, 8, cc), F32),
                   jax.ShapeDtypeStruct((bn, N_HEADS, dk, dk), F32)],
        scratch_shapes=[pltpu.VMEM((sb * 8 + tg, cc), F32), pltpu.VMEM((tg, cc), F32),
                        pltpu.VMEM((tg, 128), F32), pltpu.VMEM((tg, 128), F32),
                        pltpu.VMEM((N_HEADS, dk, dk), F32)],
        compiler_params=_cparams(("parallel", "arbitrary")), name="gated_delta")(
            qkvz, qkvz, gates, conv0, s0, conv_w, hp, nw.reshape(1, dk))


def _mlstm_kernel(q_ref, k_ref, v_ref, z_ref, og_ref, gt_ref, c0_ref, n0_ref, m0_ref, hp_ref, nw_ref,
                  o_ref, co_ref, no_ref, mo_ref,
                  li_scr, lf_scr, cext, m_scr, *, c, nch, sb, tv, dqk, dv):
    gi = pl.program_id(1)
    tg = c * nch
    lane1 = _iota((dqk, 128), 1)

    @pl.when(gi == 0)
    def _():
        for s in range(sb):
            for h in range(N_HEADS):
                cext[s * N_HEADS + h, :, 0:dv] = c0_ref[s, h]
                cext[s * N_HEADS + h, :, dv:dv + 128] = jnp.where(lane1 == 0, n0_ref[s, h], 0.0)
            m_scr[s * 8:(s + 1) * 8, :] = m0_ref[s]

    hp = hp_ref[...]
    lane = _iota((tg, 128), 1)
    for s in range(sb):
        gt = gt_ref[s]
        li = gt + hp[0:1, :]
        x = gt + hp[1:2, :]
        lf = jnp.minimum(x, 0.0) - jnp.log(1.0 + jnp.exp(-jnp.abs(x)))
        lf = jnp.where(lane >= N_HEADS, jnp.where(lane < 2 * N_HEADS, lf, 0.0), 0.0)
        if tv < c:
            valid = (_iota((tg, 128), 0) % c) < tv
            lf = jnp.where(valid, lf, 0.0)
            li = jnp.where(valid, li, NEG)
        li_scr[s] = li
        lf_scr[s] = lf

    row = _iota((c, c), 0)
    col = _iota((c, c), 1)
    ige = row >= col
    tri = jnp.where(ige, 1.0, 0.0).astype(BF16)
    ones_b = jnp.ones((c, 128), BF16)
    lane_c = _iota((c, 128), 1)
    one_col = jnp.where(lane_c == 0, 1.0, 0.0)
    nw = nw_ref[...]

    def prep(clist):
        items = [(ci, h) for ci in range(len(clist)) for h in range(N_HEADS)]
        bcums = [_exact_dot(tri, lf_scr[s, pl.ds(r0, c), :]) for s, r0 in clist]
        lichs = [li_scr[s, pl.ds(r0, c), :] for s, r0 in clist]
        qbs = [(q_ref[clist[ci][0], pl.ds(clist[ci][1], c), h * dqk:(h + 1) * dqk] * (dqk ** -0.5)).astype(BF16)
               for ci, h in items]
        ks = [k_ref[clist[ci][0], pl.ds(clist[ci][1], c), h * dqk:(h + 1) * dqk] for ci, h in items]
        qk_raws = [_dot_nt(qb, k.astype(BF16)) for qb, k in zip(qbs, ks)]
        bcols = [bcums[ci][:, N_HEADS + h:N_HEADS + h + 1] for ci, h in items]
        licols = [lichs[ci][:, h:h + 1] for ci, h in items]
        rowvs = [_exact_dot_nt(ones_b, jnp.where(lane_c == 0, licol - bcol, 0.0))
                 for licol, bcol in zip(licols, bcols)]
        out = []
        for i, (ci, h) in enumerate(items):
            dmat = jnp.where(ige, bcols[i] + rowvs[i], NEG)
            v = v_ref[clist[ci][0], pl.ds(clist[ci][1], c), h * dv:(h + 1) * dv]
            blast = bcums[ci][c - 1:c, N_HEADS + h:N_HEADS + h + 1]
            out.append((qbs[i], ks[i], jnp.concatenate([v, one_col], axis=1).astype(BF16), bcols[i], dmat,
                        jnp.max(dmat, axis=-1, keepdims=True), qk_raws[i], blast, blast - bcols[i] + licols[i]))
        return [out[ci * N_HEADS:(ci + 1) * N_HEADS] for ci in range(len(clist))]

    def chunks(clist):
        hs = range(N_HEADS)
        for (s, r0), heads in zip(clist, prep(clist)):
            mprevs = [m_scr[s * 8:s * 8 + 1, h:h + 1] for h in hs]
            c_olds = [cext[s * N_HEADS + h] for h in hs]
            qcs = [_dot(heads[h][0], c_olds[h].astype(BF16)) for h in hs]
            inters = [heads[h][3] + mprevs[h] for h in hs]
            mts = [jnp.maximum(inters[h], heads[h][5]) for h in hs]
            mnews = [mts[h][c - 1:c, :] for h in hs]
            for h in hs:
                k, vext, blast, wk_log = heads[h][1], heads[h][2], heads[h][7], heads[h][8]
                cext[s * N_HEADS + h] = (c_olds[h] * jnp.exp(blast + mprevs[h] - mnews[h])
                                         + _dot_tn((k * jnp.exp(wk_log - mnews[h])).astype(BF16), vext))
                m_scr[s * 8:s * 8 + 1, h:h + 1] = mnews[h]
            for h in hs:
                qk = heads[h][6] * jnp.exp(heads[h][4] - mts[h])
                tot = jnp.exp(inters[h] - mts[h]) * qcs[h] + _dot(qk.astype(BF16), heads[h][2])
                hh = tot[:, :dv] / jnp.maximum(jnp.abs(tot[:, dv:dv + 1]), jnp.exp(-mts[h]))
                og = og_ref[s, pl.ds(r0, c), h * dv:(h + 1) * dv]
                zz = z_ref[s, pl.ds(r0, c), h * dv:(h + 1) * dv]
                o_ref[s, pl.ds(r0, c), h * dv:(h + 1) * dv] = _rms(_sigmoid(og) * hh, nw) * _silu(zz)

    chunks([(s, i * c) for i in range(nch) for s in range(sb)])

    @pl.when(gi == pl.num_programs(1) - 1)
    def _():
        for s in range(sb):
            for h in range(N_HEADS):
                co_ref[s, h] = cext[s * N_HEADS + h, :, 0:dv]
                no_ref[s, h] = cext[s * N_HEADS + h, :, dv:dv + 1]
            mo_ref[s] = m_scr[s * 8:(s + 1) * 8, :]


def _mlstm(q, k, v, z, og, gates, c0, c_idx, n0, m0, hp, nw, *, bn, c, nch, sb, tv):
    n = q.shape[0]
    dqk, dv = c0.shape[-2], c0.shape[-1]
    tg = c * nch
    rows = n // bn
    steps = rows // tg
    hq, hv = N_HEADS * dqk, N_HEADS * dv
    row = lambda b, g: (b, g, 0)
    q, k, v, z, og, gates = (a.reshape(bn, rows, a.shape[1]) for a in (q, k, v, z, og, gates))
    kern = functools.partial(_mlstm_kernel, c=c, nch=nch, sb=sb, tv=tv, dqk=dqk, dv=dv)
    out = pl.pallas_call(
        kern, grid=(bn // sb, steps),
        in_specs=[pl.BlockSpec((sb, tg, hq), row), pl.BlockSpec((sb, tg, hq), row),
                  pl.BlockSpec((sb, tg, hv), row), pl.BlockSpec((sb, tg, hv), row),
                  pl.BlockSpec((sb, tg, hv), row), pl.BlockSpec((sb, tg, 128), row),
                  pl.BlockSpec((None, sb, N_HEADS, dqk, dv), lambda b, g: (c_idx, b, 0, 0, 0)),
                  pl.BlockSpec((sb, N_HEADS, dqk, 1), lambda b, g: (b, 0, 0, 0)),
                  pl.BlockSpec((sb, 8, 128), lambda b, g: (b, 0, 0)),
                  pl.BlockSpec((8, 128), lambda b, g: (0, 0)),
                  pl.BlockSpec((1, dv), lambda b, g: (0, 0))],
        out_specs=[pl.BlockSpec((sb, tg, hv), row),
                   pl.BlockSpec((sb, N_HEADS, dqk, dv), lambda b, g: (b, 0, 0, 0)),
                   pl.BlockSpec((sb, N_HEADS, dqk, 1), lambda b, g: (b, 0, 0, 0)),
                   pl.BlockSpec((sb, 8, 128), lambda b, g: (b, 0, 0))],
        out_shape=[jax.ShapeDtypeStruct((bn, rows, hv), F32),
                   jax.ShapeDtypeStruct((bn, N_HEADS, dqk, dv), F32),
                   jax.ShapeDtypeStruct((bn, N_HEADS, dqk, 1), F32),
                   jax.ShapeDtypeStruct((bn, 8, 128), F32)],
        scratch_shapes=[pltpu.VMEM((sb, tg, 128), F32), pltpu.VMEM((sb, tg, 128), F32),
                        pltpu.VMEM((sb * N_HEADS, dqk, dv + 128), F32), pltpu.VMEM((sb * 8, 128), F32)],
        compiler_params=_cparams(("parallel", "arbitrary")), name="mlstm")(
            q, k, v, z, og, gates, c0, n0, m0, hp, nw.reshape(1, dv))
    return (out[0].reshape(n, hv),) + tuple(out[1:])


def _lambda(lq, lam_init):
    a = jnp.sum(lq[0:1, :] * lq[1:2, :], axis=-1, keepdims=True)
    b = jnp.sum(lq[2:3, :] * lq[3:4, :], axis=-1, keepdims=True)
    return jnp.exp(a) - jnp.exp(b) + lam_init


def _attn_kernel(wq_ref, k_ref, vt_ref, z_ref, lq_ref, nw_ref, o_ref, acc_scr, *, t, hpb, lam_init):
    qi = pl.program_id(2)
    dh = k_ref.shape[1] // hpb
    nl = 4 * t
    acc_scr[...] = jnp.zeros(acc_scr.shape, F32)

    def step(jp, carry, masked):
        rows = [pl.multiple_of((2 * jp + i) * t, t) for i in range(2)]
        wqs = [jnp.concatenate([wq_ref[0, a], wq_ref[1, a]], axis=1) for a in range(hpb)]
        scores = [[_dot(k_ref[pl.ds(rows[i], t), a * dh:(a + 1) * dh], wqs[a]) for a in range(hpb)]
                  for i in range(2)]
        out = list(carry)
        for i in range(2):
            for a in range(hpb):
                m_prev, l_prev = out[a]
                s = scores[i][a]
                if masked:
                    lane = _iota((t, nl), 1)
                    qpos = (lane // (2 * t)) * t + lane % t
                    s = jnp.where(_iota((t, nl), 0) + i * t <= qpos, s, NEG)
                m_new = jnp.maximum(m_prev, jnp.max(s, axis=0, keepdims=True))
                alpha = jnp.exp2(m_prev - m_new)
                p = jnp.exp2(s - m_new)
                l_new = alpha * l_prev + jnp.sum(p, axis=0, keepdims=True)
                acc_scr[a] = alpha * acc_scr[a] + _dot(vt_ref[2 * jp + i, a * dh:(a + 1) * dh, :], p.astype(BF16))
                out[a] = (m_new, l_new)
        return tuple(out)

    init = tuple((jnp.full((1, nl), NEG, F32), jnp.zeros((1, nl), F32)) for _ in range(hpb))
    carry = lax.fori_loop(0, qi, lambda j, c: step(j, c, False), init)
    carry = step(qi, carry, True)

    lam = _lambda(lq_ref[...], lam_init)
    nw = nw_ref[...]
    for a in range(hpb):
        acc = acc_scr[a] * (1.0 / carry[a][1])
        for qb in range(2):
            b0 = 2 * t * qb
            o = (acc[:, b0:b0 + t] - lam * acc[:, b0 + t:b0 + 2 * t]).T
            zz = z_ref[qb * t:(qb + 1) * t, a * dh:(a + 1) * dh]
            o_ref[qb * t:(qb + 1) * t, a * dh:(a + 1) * dh] = _rms(o, nw) * (1.0 - lam_init) * _silu(zz)


def _attn_prompt(wq, k16, vt, z, lq, nw, *, bn, seq, lam_init):
    n = k16.shape[0]
    t = vt.shape[2]
    dh = wq.shape[2]
    hpb = 4
    nq = seq // t
    nq2 = nq // 2
    kern = functools.partial(_attn_kernel, t=t, hpb=hpb, lam_init=lam_init)
    return pl.pallas_call(
        kern, grid=(bn, N_HEADS // hpb, nq2),
        in_specs=[pl.BlockSpec((2, hpb, dh, 2 * t), lambda b, h, i: (b * nq2 + i, h, 0, 0)),
                  pl.BlockSpec((seq, hpb * dh), lambda b, h, i: (b, h), pipeline_mode=pl.Buffered(1)),
                  pl.BlockSpec((nq, hpb * dh, t), lambda b, h, i: (b, h, 0), pipeline_mode=pl.Buffered(1)),
                  pl.BlockSpec((2 * t, hpb * dh), lambda b, h, i: (b * nq2 + i, h)),
                  pl.BlockSpec(lq.shape, lambda b, h, i: (0, 0)),
                  pl.BlockSpec((1, dh), lambda b, h, i: (0, 0))],
        out_specs=pl.BlockSpec((2 * t, hpb * dh), lambda b, h, i: (b * nq2 + i, h)),
        out_shape=jax.ShapeDtypeStruct((n, N_HEADS * dh), F32),
        scratch_shapes=[pltpu.VMEM((hpb, dh, 4 * t), F32)],
        compiler_params=_cparams(("parallel", "parallel", "arbitrary")), name="diff_attn_prompt")(
            wq, k16, vt, z, lq, nw.reshape(1, dh))


def _attn_s_kernel(pt_ref, q_ref, kn_ref, vn_ref, z_ref, lq_ref, nw_ref, *rest, pg, tv, lam_init):
    k_pages = rest[:pg]
    v_pages = rest[pg:2 * pg]
    o_ref, qbd, bias, m_scr, l_scr, acc_scr = rest[2 * pg:]
    j = pl.program_id(1)
    tp, hd = q_ref.shape
    dh = hd // N_HEADS
    dqk = dh // 2
    prow = k_pages[0].shape[0]
    rph = 2 * tp

    @pl.when(j == 0)
    def _():
        q = q_ref[...] * (dqk ** -0.5)
        lane = _iota((tp, dh), 1)
        for h in range(N_HEADS):
            qh = q[:, h * dh:(h + 1) * dh]
            qbd[h * rph:h * rph + tp, :] = jnp.where(lane < dqk, qh, 0.0)
            qbd[h * rph + tp:(h + 1) * rph, :] = jnp.where(lane >= dqk, qh, 0.0)
        rhead = _iota(bias.shape, 0) // rph
        chead = _iota(bias.shape, 1) % N_HEADS
        bias[...] = jnp.where(rhead == chead, 0.0, NEG)
        m_scr[...] = jnp.full(m_scr.shape, NEG, F32)
        l_scr[...] = jnp.zeros(l_scr.shape, F32)
        acc_scr[...] = jnp.zeros(acc_scr.shape, F32)

    def update(s, pv):
        m_prev = m_scr[...]
        m_new = jnp.maximum(m_prev, jnp.max(s, axis=-1, keepdims=True))
        alpha = jnp.exp(m_prev - m_new)
        p = jnp.exp(s - m_new[:, 0:1])
        l_scr[...] = alpha * l_scr[...] + jnp.sum(p, axis=-1, keepdims=True)
        acc_scr[...] = alpha * acc_scr[...] + pv(p)
        m_scr[...] = m_new

    qb = qbd[...].astype(BF16)
    bs_ = bias[...]
    s = jnp.concatenate([_dot_nt(qb, kp[...].astype(BF16)) + bs_ for kp in k_pages], axis=1)

    def pv_pages(p):
        pb = p.astype(BF16)
        acc = _dot(pb[:, 0:prow], v_pages[0][...].astype(BF16))
        for i in range(1, pg):
            acc = acc + _dot(pb[:, i * prow:(i + 1) * prow], v_pages[i][...].astype(BF16))
        return acc

    update(s, pv_pages)

    @pl.when(j == pl.num_programs(1) - 1)
    def _():
        qf = qbd[...]
        kn = kn_ref[...]
        vn = vn_ref[...]
        sn = jnp.concatenate([_dot_nt(qf[h * rph:(h + 1) * rph], kn[:, h * dh:(h + 1) * dh])
                              for h in range(N_HEADS)], axis=0)
        tq = _iota(sn.shape, 0) % tp
        tk = _iota(sn.shape, 1)
        sn = jnp.where(tk <= tq, jnp.where(tk < tv, sn, NEG), NEG)
        update(sn, lambda p: jnp.concatenate(
            [_dot(p[h * rph:(h + 1) * rph], vn[:, h * dh:(h + 1) * dh]) for h in range(N_HEADS)], axis=0))
        acc = acc_scr[...] / l_scr[...]
        lam = _lambda(lq_ref[...], lam_init)
        nw = nw_ref[...]
        normed = jnp.concatenate(
            [_rms(acc[h * rph:h * rph + tp] - lam * acc[h * rph + tp:(h + 1) * rph], nw) for h in range(N_HEADS)],
            axis=1)
        o_ref[...] = normed * (1.0 - lam_init) * _silu(z_ref[...])


def _attn_sample(page_table, q, kn, vn, z, lq, nw, cache_k, cache_v, *, e, tv, lam_init):
    bs, npages = page_table.shape
    n, hd = q.shape
    tp = n // bs
    dh = hd // N_HEADS
    pg = math.gcd(npages, PAGES_PER_STEP)
    prow = cache_k.shape[2]
    steps = npages // pg
    rows = 2 * N_HEADS * tp
    tok = lambda b, j, pt: (b, 0)
    page_specs = [pl.BlockSpec((None, None, prow, dh),
                               lambda b, j, pt, i=i: (e, pt[b, j * pg + i], 0, 0)) for i in range(pg)]
    kern = functools.partial(_attn_s_kernel, pg=pg, tv=tv, lam_init=lam_init)
    grid_spec = pltpu.PrefetchScalarGridSpec(
        num_scalar_prefetch=1, grid=(bs, steps),
        in_specs=[pl.BlockSpec((tp, hd), tok), pl.BlockSpec((tp, hd), tok), pl.BlockSpec((tp, hd), tok),
                  pl.BlockSpec((tp, hd), tok),
                  pl.BlockSpec(lq.shape, lambda b, j, pt: (0, 0)),
                  pl.BlockSpec((1, dh), lambda b, j, pt: (0, 0))] + page_specs + page_specs,
        out_specs=pl.BlockSpec((tp, hd), tok),
        scratch_shapes=[pltpu.VMEM((rows, dh), F32), pltpu.VMEM((rows, prow), F32), pltpu.VMEM((rows, 128), F32),
                        pltpu.VMEM((rows, 128), F32), pltpu.VMEM((rows, dh), F32)])
    return pl.pallas_call(
        kern, grid_spec=grid_spec, out_shape=jax.ShapeDtypeStruct((n, hd), F32),
        compiler_params=_cparams(("parallel", "arbitrary")), name="diff_attn_sample")(
            page_table, q, kn, vn, z, lq, nw.reshape(1, dh), *([cache_k] * pg), *([cache_v] * pg))


def _rope_tables(pos, dqk):
    rope_dim = dqk // 4
    half = rope_dim // 2
    inv = jnp.power(ROPE_THETA, -jnp.arange(half, dtype=F32) / half)
    ang = pos.astype(F32)[:, None] * inv[None, :]
    lane = jnp.arange(128) % dqk
    idx = lane % half
    cos = jnp.where(lane < rope_dim, jnp.cos(ang)[:, idx], 1.0)
    sin = jnp.sin(ang)[:, idx]
    sa = jnp.where((lane >= half) & (lane < rope_dim), sin, 0.0)
    sb = jnp.where(lane < half, -sin, 0.0)
    return cos.astype(F32), sa.astype(F32), sb.astype(F32)


def _lane_row(vals, offset):
    out = jnp.zeros((128,), F32)
    return out.at[offset:offset + vals.shape[0]].set(vals.astype(F32))


def kernel(x_prompt, x_sample, cache_k, cache_v, page_table, state_gdn_conv, state_gdn_s, state_mlstm_c,
           state_mlstm_n, state_mlstm_m, norm_w, final_norm_w, w_in_even, w_out_even, conv_w, a_log, dt_bias,
           gdn_norm_w, lambda_qk, diff_norm_w, w_in_odd, w_out_odd, b_i, b_f, mlstm_norm_w):
    bp, seq, d = x_prompt.shape
    bs, ts, _ = x_sample.shape
    depth = norm_w.shape[0]
    tp = SAMPLE_PAD
    dk_a = state_gdn_s.shape[-1]
    conv_ch = state_gdn_conv.shape[-1]
    w_a = N_HEADS * dk_a
    dh_b = cache_v.shape[-1]
    dqk_b = dh_b // 2
    w_b = N_HEADS * dh_b
    dqk_c, dv_c = state_mlstm_c.shape[-2], state_mlstm_c.shape[-1]
    hq_c, w_c = N_HEADS * dqk_c, N_HEADS * dv_c
    past = page_table.shape[1] * cache_k.shape[2]
    n_pool, page = cache_k.shape[1], cache_k.shape[2]

    xp = x_prompt.reshape(bp * seq, d)
    xs = jnp.pad(x_sample, ((0, 0), (0, tp - ts), (0, 0))).reshape(bs * tp, d)
    tm_p = math.gcd(seq, 256)
    tm_s = math.gcd(bs * tp, 256)
    nch_p = math.gcd(seq // CHUNK, 4)
    seq_s = math.gcd(bs, 4)

    rope_p = _rope_tables(jnp.arange(seq), dqk_b)
    rope_s = _rope_tables(past + (jnp.arange(tm_s) % tp), dqk_b)
    half = dqk_b // 8
    ang_t = (jnp.power(ROPE_THETA, -jnp.arange(half, dtype=F32) / half)[:, None]
             * jnp.arange(seq).astype(F32)[None, :])
    cos_t, sin_t = jnp.cos(ang_t), jnp.sin(ang_t)

    even_w = conv_ch + w_a
    seg_even = [(0, even_w), (even_w, w_b), (even_w + w_b, w_b), (even_w + 2 * w_b, w_b),
                (even_w + 3 * w_b, w_b), (even_w + 4 * w_b, 128)]
    seg_even_p = [seg_even[0]] + seg_even[2:]
    rope_even = tuple(range(even_w // 128, (even_w + 2 * w_b) // 128))
    odd_w = 2 * hq_c + 3 * w_c
    seg_odd = [(0, hq_c), (hq_c, hq_c), (2 * hq_c, w_c), (2 * hq_c + w_c, w_c), (2 * hq_c + 2 * w_c, w_c),
               (odd_w, 128)]

    ck = cache_k.reshape(cache_k.shape[0], n_pool, page * N_HEADS, dh_b)
    cv = cache_v.reshape(cache_v.shape[0], n_pool, page * N_HEADS, dh_b)

    n_even = (depth + 1) // 2
    krows_p = jnp.zeros((n_even, bp * seq * N_HEADS, dh_b), F32)
    vrows_p = jnp.zeros((n_even, bp * seq * N_HEADS, dh_b), F32)
    k_rows_s, v_rows_s = [], []
    conv_p, s_p, conv_s, s_s = [], [], [], []
    c_p, n_p, m_p, c_s, n_s, m_s = [], [], [], [], [], []

    pend_p = pend_s = None
    for layer in range(depth):
        if layer % 2 == 0:
            e = layer // 2
            lam_init = 0.8 - 0.6 * math.exp(-0.3 * layer)
            w = w_in_even[e]
            g0 = even_w
            w_r = jnp.concatenate([w[:, :g0], w[:, g0 + 2 * N_HEADS:], w[:, g0:g0 + 2 * N_HEADS],
                                   jnp.zeros((d, 128 - 2 * N_HEADS), F32)], axis=1).astype(BF16)
            w_o = w_out_even[e].astype(BF16)
            hp = jnp.zeros((8, 128), F32).at[0].set(_lane_row(a_log[e], N_HEADS)).at[1].set(
                _lane_row(dt_bias[e], N_HEADS))
            lq = lambda_qk[e]

            wq_t = w[:, g0 + 2 * N_HEADS:g0 + 2 * N_HEADS + w_b].T.astype(BF16)
            wv_t = w[:, g0 + 2 * N_HEADS + 2 * w_b:g0 + 2 * N_HEADS + 3 * w_b].T.astype(BF16)
            outs = _proj(xp, norm_w[layer], w_r, seg_even_p, (rope_even, rope_p), tm_p, k16_seg=1,
                         feat=(wq_t, wv_t, cos_t, sin_t), rows=((1, 2), e, (krows_p, vrows_p)), pre=pend_p)
            if pend_p is not None:
                xp = outs.pop()
            az, krows_p, vrows_p, zb, gt, k16, wq, vt = outs
            oa, cst, sst = _gdn(az, gt, jnp.zeros((bp, 8, conv_ch), F32),
                                jnp.zeros((1, bp, N_HEADS, dk_a, dk_a), F32), 0,
                                conv_w[e], hp, gdn_norm_w[e], bn=bp, c=CHUNK, nch=nch_p, tv=CHUNK)
            ob = _attn_prompt(wq, k16, vt, zb, lq, diff_norm_w[e], bn=bp, seq=seq, lam_init=lam_init)
            pend_p = (oa, ob, w_a, 0, w_o)
            conv_p.append(cst[:, 8 - (CONV_W - 1):])
            s_p.append(sst)

            outs = _proj(xs, norm_w[layer], w_r, seg_even, (rope_even, rope_s), tm_s, pre=pend_s)
            if pend_s is not None:
                xs = outs.pop()
            az, qb, kb, vb, zb, gt = outs
            conv0 = jnp.pad(state_gdn_conv[e], ((0, 0), (8 - (CONV_W - 1), 0), (0, 0)))
            oa, cst, sst = _gdn(az, gt, conv0, state_gdn_s, e, conv_w[e], hp, gdn_norm_w[e],
                                bn=bs, c=tp, nch=seq_s, tv=ts, per_seq=True)
            ob = _attn_sample(page_table, qb, kb, vb, zb, lq, diff_norm_w[e], ck, cv, e=e, tv=ts,
                              lam_init=lam_init)
            pend_s = (oa, ob, w_a, 0, w_o)
            k_rows_s.append(kb.reshape(bs, tp, N_HEADS, dh_b)[:, :ts])
            v_rows_s.append(vb.reshape(bs, tp, N_HEADS, dh_b)[:, :ts])
            conv_s.append(cst[:, 8 - (CONV_W - 1):])
            s_s.append(sst)
        else:
            o = layer // 2
            w_r = jnp.concatenate([w_in_odd[o], jnp.zeros((d, 128 - 2 * N_HEADS), F32)], axis=1).astype(BF16)
            w_o = w_out_odd[o].astype(BF16)
            hp = jnp.zeros((8, 128), F32).at[0].set(_lane_row(b_i[o], 0)).at[1].set(_lane_row(b_f[o], N_HEADS))

            outs = _proj(xp, norm_w[layer], w_r, seg_odd, None, tm_p, pre=pend_p)
            if pend_p is not None:
                xp = outs.pop()
            q, k, v, z, og, gt = outs
            hm, cf, nf, mf = _mlstm(q, k, v, z, og, gt, jnp.zeros((1, bp, N_HEADS, dqk_c, dv_c), F32), 0,
                                    jnp.zeros((bp, N_HEADS, dqk_c, 1), F32), jnp.zeros((bp, 8, 128), F32),
                                    hp, mlstm_norm_w[o], bn=bp, c=CHUNK, nch=math.gcd(seq // CHUNK, 2),
                                    sb=1, tv=CHUNK)
            pend_p = (hm, hm, w_c // 2, 1, w_o)
            c_p.append(cf)
            n_p.append(nf[..., 0])
            m_p.append(mf[:, 0, :N_HEADS])

            outs = _proj(xs, norm_w[layer], w_r, seg_odd, None, tm_s, pre=pend_s)
            if pend_s is not None:
                xs = outs.pop()
            q, k, v, z, og, gt = outs
            m0 = jnp.zeros((bs, 8, 128), F32).at[:, 0, :N_HEADS].set(state_mlstm_m[o])
            hm, cf, nf, mf = _mlstm(q, k, v, z, og, gt, state_mlstm_c, o, state_mlstm_n[o][..., None], m0,
                                    hp, mlstm_norm_w[o], bn=bs, c=tp, nch=1, sb=seq_s, tv=ts)
            pend_s = (hm, hm, w_c // 2, 1, w_o)
            c_s.append(cf)
            n_s.append(nf[..., 0])
            m_s.append(mf[:, 0, :N_HEADS])

    xp = _out_proj(*pend_p, xp, final_norm_w, True, tm_p)
    xs = _out_proj(*pend_s, xs, final_norm_w, True, tm_s)
    y_prompt = xp.reshape(bp, seq, d)
    y_sample = xs.reshape(bs, tp, d)[:, :ts]
    st = jnp.stack
    kv_shape = (n_even, bp, seq, N_HEADS, dh_b)
    return (y_prompt, y_sample, krows_p.reshape(kv_shape), vrows_p.reshape(kv_shape), st(k_rows_s), st(v_rows_s),
            st(conv_p), st(s_p), st(conv_s), st(s_s),
            st(c_p), st(n_p), st(m_p), st(c_s), st(n_s), st(m_s))
```

```python
import functools
import math

import jax
import jax.numpy as jnp
from jax import lax
from jax.experimental import pallas as pl
from jax.experimental.pallas import tpu as pltpu

F32 = jnp.float32
BF16 = jnp.bfloat16

EPS = 1e-6
NEG = -1e30
N_HEADS = 4
CONV_W = 4
ROPE_THETA = 500000.0
CHUNK = 64
SAMPLE_PAD = 8
PAGES_PER_STEP = 16
V7X_VMEM_LIMIT = 56 * 1024 * 1024


def _cparams(sem):
    return pltpu.CompilerParams(dimension_semantics=sem, vmem_limit_bytes=V7X_VMEM_LIMIT)


def _dot(a, b):
    return jnp.dot(a, b, preferred_element_type=F32)


def _dot_nt(a, b):
    return lax.dot_general(a, b, (((1,), (1,)), ((), ())), preferred_element_type=F32)


def _dot_tn(a, b):
    return lax.dot_general(a, b, (((0,), (0,)), ((), ())), preferred_element_type=F32)


def _split3(x):
    hi = x.astype(BF16)
    r = x - hi.astype(F32)
    mid = r.astype(BF16)
    lo = (r - mid.astype(F32)).astype(BF16)
    return hi, mid, lo


def _exact_dot(a_bf16, x):
    hi, mid, lo = _split3(x)
    return _dot(a_bf16, hi) + (_dot(a_bf16, mid) + _dot(a_bf16, lo))


def _exact_dot_nt(a_bf16, x):
    hi, mid, lo = _split3(x)
    return _dot_nt(a_bf16, hi) + (_dot_nt(a_bf16, mid) + _dot_nt(a_bf16, lo))


def _sigmoid(x):
    return 1.0 / (1.0 + jnp.exp(-x))


def _silu(x):
    return x * _sigmoid(x)


def _softplus(x):
    return jnp.maximum(x, 0.0) + jnp.log(1.0 + jnp.exp(-jnp.abs(x)))


def _rms(x, w):
    return x * lax.rsqrt(jnp.mean(x * x, axis=-1, keepdims=True) + EPS) * w


def _iota(shape, dim):
    return lax.broadcasted_iota(jnp.int32, shape, dim)


def _proj_kernel(*refs, segs, rope_blocks, k16_seg, feat_major, row_segs, fuse_out):
    n_out = len(segs)
    pos = 3
    x_ref, nw_ref, w_ref = refs[:3]
    if rope_blocks:
        cos_ref, sa_ref, sb_ref = refs[pos:pos + 3]
        pos += 3
    if feat_major:
        wqt_ref, wvt_ref, cost_ref, sint_ref = refs[pos:pos + 4]
        pos += 4
    if fuse_out:
        a_ref, b_ref, wo_ref = refs[pos:pos + 3]
        pos += 3
    pos += len(row_segs)
    outs = refs[pos:pos + n_out]
    pos += n_out
    x = x_ref[...]
    if fuse_out:
        ka = a_ref.shape[1]
        x = x + (_dot(a_ref[...].astype(BF16), wo_ref[0:ka, :]) + _dot(b_ref[...].astype(BF16), wo_ref[ka:, :]))
        refs[-1][...] = x
    h = _rms(x, nw_ref[...]).astype(BF16)
    for si, ((c0, width), o_ref) in enumerate(zip(segs, outs)):
        for s0 in range(0, width, 512):
            sw = min(512, width - s0)
            acc = _dot(h, w_ref[:, c0 + s0:c0 + s0 + sw])
            if rope_blocks and (c0 + s0) // 128 in rope_blocks:
                cos = cos_ref[...]
                sa = sa_ref[...]
                sb = sb_ref[...]
                parts = []
                for t0 in range(0, sw, 128):
                    a = acc[:, t0:t0 + 128]
                    parts.append(a * cos + pltpu.roll(a, 8, 1) * sa + pltpu.roll(a, 120, 1) * sb)
                acc = jnp.concatenate(parts, axis=1)
            if si in row_segs:
                dh = sw // N_HEADS
                for hd in range(N_HEADS):
                    o_ref[pl.ds(hd, acc.shape[0], stride=N_HEADS), :] = acc[:, hd * dh:(hd + 1) * dh]
            else:
                o_ref[:, s0:s0 + sw] = acc
            if si == k16_seg:
                refs[pos][:, s0:s0 + sw] = acc.astype(BF16)
    if feat_major:
        wq_ref, vt_ref = refs[pos + 1], refs[pos + 2]
        tm = x.shape[0]
        vt_ref[0] = _dot_nt(wvt_ref[...], h).astype(BF16)
        qt = _dot_nt(wqt_ref[...], h)
        dqk = qt.shape[0] // (2 * N_HEADS)
        half = dqk // 8
        cos = cost_ref[...]
        sin = sint_ref[...]
        scale = (dqk ** -0.5) * math.log2(math.e)
        zero = jnp.zeros((dqk, tm), F32)
        for hd in range(N_HEADS):
            maps = []
            for m in range(2):
                r0 = (2 * hd + m) * dqk
                x1 = qt[r0:r0 + half]
                x2 = qt[r0 + half:r0 + 2 * half]
                maps.append(jnp.concatenate([x1 * cos - x2 * sin, x2 * cos + x1 * sin,
                                             qt[r0 + 2 * half:r0 + dqk]], axis=0) * scale)
            top = jnp.concatenate([maps[0], zero], axis=1)
            bot = jnp.concatenate([zero, maps[1]], axis=1)
            wq_ref[0, hd] = jnp.concatenate([top, bot], axis=0).astype(BF16)


def _proj(x, nw, w, segs, rope, tm, k16_seg=None, feat=None, rows=None, pre=None):
    n, d = x.shape
    ncols = w.shape[1]
    grid = (n // tm,)
    in_specs = [pl.BlockSpec((tm, d), lambda i: (i, 0)),
                pl.BlockSpec((1, d), lambda i: (0, 0)),
                pl.BlockSpec((d, ncols), lambda i: (0, 0))]
    args = [x, nw.reshape(1, d), w]
    rope_blocks = ()
    if rope is not None:
        rope_blocks, tables = rope
        nt = tables[0].shape[0] // tm
        for t in tables:
            in_specs.append(pl.BlockSpec((tm, 128), lambda i, nt=nt: (i % nt, 0)))
            args.append(t)
    out_shape = [jax.ShapeDtypeStruct((n, width), F32) for _, width in segs]
    out_specs = [pl.BlockSpec((tm, width), lambda i: (i, 0)) for _, width in segs]
    if feat is not None:
        wqt, wvt, cost, sint = feat
        ntt = cost.shape[1] // tm
        in_specs += [pl.BlockSpec(wqt.shape, lambda i: (0, 0)), pl.BlockSpec(wvt.shape, lambda i: (0, 0)),
                     pl.BlockSpec((cost.shape[0], tm), lambda i, ntt=ntt: (0, i % ntt)),
                     pl.BlockSpec((sint.shape[0], tm), lambda i, ntt=ntt: (0, i % ntt))]
        args += [wqt, wvt, cost, sint]
    if pre is not None:
        a, b, ka, b_col, w_out = pre
        kb = w_out.shape[0] - ka
        in_specs += [pl.BlockSpec((tm, ka), lambda i: (i, 0)), pl.BlockSpec((tm, kb), lambda i: (i, b_col)),
                     pl.BlockSpec(w_out.shape, lambda i: (0, 0))]
        args += [a, b, w_out]
    row_segs, aliases = (), {}
    if rows is not None:
        row_segs, layer, stacked = rows
        for si, arr in zip(row_segs, stacked):
            aliases[len(args)] = si
            in_specs.append(pl.BlockSpec(memory_space=pl.ANY))
            args.append(arr)
            out_shape[si] = jax.ShapeDtypeStruct(arr.shape, F32)
            out_specs[si] = pl.BlockSpec((None, tm * N_HEADS, arr.shape[2]), lambda i, layer=layer: (layer, i, 0))
    if k16_seg is not None:
        kw = segs[k16_seg][1]
        out_shape.append(jax.ShapeDtypeStruct((n, kw), BF16))
        out_specs.append(pl.BlockSpec((tm, kw), lambda i: (i, 0)))
    if feat is not None:
        hq, hv = wqt.shape[0], wvt.shape[0]
        dh = hq // N_HEADS
        out_shape += [jax.ShapeDtypeStruct((n // tm, N_HEADS, dh, 2 * tm), BF16),
                      jax.ShapeDtypeStruct((n // tm, hv, tm), BF16)]
        out_specs += [pl.BlockSpec((1, N_HEADS, dh, 2 * tm), lambda i: (i, 0, 0, 0)),
                      pl.BlockSpec((1, hv, tm), lambda i: (i, 0, 0))]
    if pre is not None:
        out_shape.append(jax.ShapeDtypeStruct((n, d), F32))
        out_specs.append(pl.BlockSpec((tm, d), lambda i: (i, 0)))
    return list(pl.pallas_call(
        functools.partial(_proj_kernel, segs=tuple(segs), rope_blocks=tuple(rope_blocks), k16_seg=k16_seg,
                          feat_major=feat is not None, row_segs=tuple(row_segs), fuse_out=pre is not None),
        grid=grid, in_specs=in_specs, out_specs=out_specs, out_shape=out_shape, input_output_aliases=aliases,
        compiler_params=_cparams(("parallel",)), name="norm_proj")(*args))


def _out_kernel(a_ref, b_ref, w_ref, x_ref, fw_ref, o_ref, *, final):
    ka = a_ref.shape[1]
    y = _dot(a_ref[...].astype(BF16), w_ref[0:ka, :]) + _dot(b_ref[...].astype(BF16), w_ref[ka:, :])
    xn = x_ref[...] + y
    if final:
        xn = _rms(xn, fw_ref[...])
    o_ref[...] = xn


def _out_proj(a, b, ka, b_col, w, x, fw, final, tm):
    n, d = x.shape
    kb = w.shape[0] - ka
    return pl.pallas_call(
        functools.partial(_out_kernel, final=final),
        grid=(n // tm,),
        in_specs=[pl.BlockSpec((tm, ka), lambda i: (i, 0)),
                  pl.BlockSpec((tm, kb), lambda i: (i, b_col)),
                  pl.BlockSpec(w.shape, lambda i: (0, 0)),
                  pl.BlockSpec((tm, d), lambda i: (i, 0)),
                  pl.BlockSpec((1, d), lambda i: (0, 0))],
        out_specs=pl.BlockSpec((tm, d), lambda i: (i, 0)),
        out_shape=jax.ShapeDtypeStruct((n, d), F32),
        compiler_params=_cparams(("parallel",)), name="out_proj")(a, b, w, x, fw.reshape(1, d))


def _neumann(a_list, c):
    eye = jnp.where(_iota((c, c), 0) == _iota((c, c), 1), 1.0, 0.0)
    ts = [eye - a for a in a_list]
    abs_ = [a.astype(BF16) for a in a_list]
    ms = [_dot(ab, ab) for ab in abs_]
    levels = int(math.log2(c)) - 1
    for k in range(levels):
        mbs = [m.astype(BF16) for m in ms]
        ts = [t + _dot(mb, t.astype(BF16)) for mb, t in zip(mbs, ts)]
        if k + 1 < levels:
            ms = [_dot(mb, mb) for mb in mbs]
    return ts


def _gdn_kernel(x_ref, z_ref, gt_ref, conv0_ref, s0_ref, cw_ref, hp_ref, nw_ref, stack_ref,
                o_ref, convo_ref, so_ref,
                xbuf, qkv_scr, g_scr, b_scr, s_scr, *, c, nch, tv, dk, unroll, per_seq):
    gi = pl.program_id(1)
    tg = c * nch
    hq = N_HEADS * dk
    cw = cw_ref[...]

    def conv(base, rows):
        return _silu((xbuf[base + 5:base + 5 + rows, :] * cw[0:1, :] + xbuf[base + 6:base + 6 + rows, :] * cw[1:2, :])
                     + (xbuf[base + 7:base + 7 + rows, :] * cw[2:3, :] + xbuf[base + 8:base + 8 + rows, :] * cw[3:4, :]))

    if per_seq:
        for s in range(nch):
            base = s * (8 + c)
            xbuf[base:base + 8, :] = conv0_ref[s]
            xbuf[base + 8:base + 8 + c, :] = x_ref[s * c:(s + 1) * c, :]
            qkv_scr[s * c:(s + 1) * c, :] = conv(base, c)
            convo_ref[s, 0:8 - (CONV_W - 1), :] = jnp.zeros((8 - (CONV_W - 1), xbuf.shape[1]), F32)
            convo_ref[s, 8 - (CONV_W - 1):8, :] = xbuf[base + 8 + tv - (CONV_W - 1):base + 8 + tv, :]
    else:
        @pl.when(gi == 0)
        def _():
            xbuf[0:8, :] = conv0_ref[0]
            s_scr[...] = s0_ref[0]

        xbuf[8:8 + tg, :] = x_ref[...]
        qkv_scr[...] = conv(0, tg)
        xbuf[5:8, :] = xbuf[8 + tg - 3:8 + tg, :]

    gt = gt_ref[...]
    hp = hp_ref[...]
    lane = _iota((tg, 128), 1)
    beta = _sigmoid(gt)
    g = -jnp.exp(hp[0:1, :]) * _softplus(gt + hp[1:2, :])
    g = jnp.where(lane >= N_HEADS, jnp.where(lane < 2 * N_HEADS, g, 0.0), 0.0)
    if tv < c:
        valid = (_iota((tg, 128), 0) % c) < tv
        g = jnp.where(valid, g, 0.0)
        beta = jnp.where(valid, beta, 0.0)
    g_scr[...] = g
    b_scr[...] = beta

    row = _iota((c, c), 0)
    col = _iota((c, c), 1)
    ige = row >= col
    igt = row > col
    tri = jnp.where(ige, 1.0, 0.0).astype(BF16)
    lane_c = _iota((c, 128), 1)
    nw = nw_ref[...]

    sels = [jnp.where(lane_c == N_HEADS + h, 1.0, 0.0).astype(BF16) for h in range(N_HEADS)]

    def prep(starts):
        items = [(ci, h) for ci in range(len(starts)) for h in range(N_HEADS)]
        gcums = [_exact_dot(tri, g_scr[pl.ds(r0, c), :]) for r0 in starts]
        bchs = [b_scr[pl.ds(r0, c), :] for r0 in starts]
        qns, kns, vhs = [], [], []
        for ci, h in items:
            r0 = starts[ci]
            qh = qkv_scr[pl.ds(r0, c), h * dk:(h + 1) * dk]
            kh = qkv_scr[pl.ds(r0, c), hq + h * dk:hq + (h + 1) * dk]
            vhs.append(qkv_scr[pl.ds(r0, c), 2 * hq + h * dk:2 * hq + (h + 1) * dk])
            qns.append(qh * lax.rsqrt(jnp.sum(qh * qh, axis=-1, keepdims=True) + EPS) * (dk ** -0.5))
            kns.append(kh * lax.rsqrt(jnp.sum(kh * kh, axis=-1, keepdims=True) + EPS))
        qkks = [_dot_nt(jnp.concatenate([qn, kn], axis=0).astype(BF16), kn.astype(BF16))
                for qn, kn in zip(qns, kns)]
        grows = [_exact_dot_nt(sels[h], gcums[ci]) for ci, h in items]
        gcols = [gcums[ci][:, N_HEADS + h:N_HEADS + h + 1] for ci, h in items]
        bcols = [bchs[ci][:, h:h + 1] for ci, h in items]
        decays = [jnp.where(ige, jnp.exp(jnp.where(ige, gcol - grow, 0.0)), 0.0)
                  for gcol, grow in zip(gcols, grows)]
        t_invs = _neumann([jnp.where(igt, qkk[c:] * decay * bcol, 0.0)
                           for qkk, decay, bcol in zip(qkks, decays, bcols)], c)
        egs = [jnp.exp(gcol) for gcol in gcols]
        sols = [_dot(t_inv.astype(BF16),
                     jnp.concatenate([vh * bcol, kn * (bcol * eg)], axis=1).astype(BF16))
                for t_inv, vh, kn, bcol, eg in zip(t_invs, vhs, kns, bcols, egs)]
        glasts = [gcums[ci][c - 1:c, N_HEADS + h:N_HEADS + h + 1] for ci, h in items]
        kws = [(kn * jnp.exp(glast - gcol)).astype(BF16) for kn, glast, gcol in zip(kns, glasts, gcols)]
        aqks = [(qkk[:c] * decay).astype(BF16) for qkk, decay in zip(qkks, decays)]
        sol_bs = [sol.astype(BF16) for sol in sols]
        ktus = [_dot_tn(kw, sol_b) for kw, sol_b in zip(kws, sol_bs)]
        aus = [_dot(aqk, sol_b) for aqk, sol_b in zip(aqks, sol_bs)]
        out = []
        for i in range(len(items)):
            lhs = jnp.concatenate([-ktus[i][:, dk:], qns[i] * egs[i] - aus[i][:, dk:]], axis=0).astype(BF16)
            out.append((lhs, ktus[i][:, :dk], aus[i][:, :dk], jnp.exp(glasts[i])))
        return [out[ci * N_HEADS:(ci + 1) * N_HEADS] for ci in range(len(starts))]

    def chunks(starts, seq_ids):
        preps = prep(starts)
        for r0, sid, heads in zip(starts, seq_ids, preps):
            s_olds = [s_scr[h] if sid is None else s0_ref[sid, h] for h in range(N_HEADS)]
            res = [_dot(heads[h][0], s_olds[h].astype(BF16)) for h in range(N_HEADS)]
            for h in range(N_HEADS):
                s_new = s_olds[h] * heads[h][3] + (heads[h][1] + res[h][:dk])
                if sid is None:
                    s_scr[h] = s_new
                else:
                    so_ref[sid, h] = s_new
            for h in range(N_HEADS):
                zz = z_ref[pl.ds(r0, c), h * dk:(h + 1) * dk]
                o = heads[h][2] + res[h][dk:]
                o_ref[pl.ds(r0, c), h * dk:(h + 1) * dk] = _rms(o, nw) * _silu(zz)

    if per_seq:
        chunks([s * c for s in range(nch)], list(range(nch)))
    else:
        def body(it, carry):
            chunks([pl.multiple_of((it * unroll + i) * c, c) for i in range(unroll)], [None] * unroll)
            return carry

        lax.fori_loop(0, nch // unroll, body, 0)

        @pl.when(gi == pl.num_programs(1) - 1)
        def _():
            convo_ref[0] = xbuf[0:8, :]
            so_ref[0] = s_scr[...]


def _gdn(qkvz, gates, conv0, s0, s_idx, conv_w, hp, nw, s_stack, out_idx, *, bn, c, nch, tv, per_seq=False):
    n = qkvz.shape[0]
    dk = s0.shape[-1]
    cc = 3 * N_HEADS * dk
    tg = c * nch
    sb = nch if per_seq else 1
    steps = 1 if per_seq else n // (bn * tg)
    kern = functools.partial(_gdn_kernel, c=c, nch=nch, tv=tv, dk=dk, unroll=math.gcd(nch, 8), per_seq=per_seq)
    return pl.pallas_call(
        kern, grid=(bn // sb, steps),
        in_specs=[pl.BlockSpec((tg, cc), lambda b, g: (b * steps + g, 0)),
                  pl.BlockSpec((tg, N_HEADS * dk), lambda b, g: (b * steps + g, 3)),
                  pl.BlockSpec((tg, 128), lambda b, g: (b * steps + g, 0)),
                  pl.BlockSpec((sb, 8, cc), lambda b, g: (b, 0, 0)),
                  pl.BlockSpec((None, sb, N_HEADS, dk, dk), lambda b, g: (s_idx, b, 0, 0, 0)),
                  pl.BlockSpec((CONV_W, cc), lambda b, g: (0, 0)),
                  pl.BlockSpec((8, 128), lambda b, g: (0, 0)),
                  pl.BlockSpec((1, dk), lambda b, g: (0, 0)),
                  pl.BlockSpec(memory_space=pl.ANY)],
        out_specs=[pl.BlockSpec((tg, N_HEADS * dk), lambda b, g: (b * steps + g, 0)),
                   pl.BlockSpec((sb, 8, cc), lambda b, g: (b, 0, 0)),
                   pl.BlockSpec((None, sb, N_HEADS, dk, dk), lambda b, g: (out_idx, b, 0, 0, 0))],
        out_shape=[jax.ShapeDtypeStruct((n, N_HEADS * dk), F32),
                   jax.ShapeDtypeStruct((bn, 8, cc), F32),
                   jax.ShapeDtypeStruct(s_stack.shape, F32)],
        input_output_aliases={8: 2},
        scratch_shapes=[pltpu.VMEM((sb * 8 + tg, cc), F32), pltpu.VMEM((tg, cc), F32),
                        pltpu.VMEM((tg, 128), F32), pltpu.VMEM((tg, 128), F32),
                        pltpu.VMEM((N_HEADS, dk, dk), F32)],
        compiler_params=_cparams(("parallel", "arbitrary")), name="gated_delta")(
            qkvz, qkvz, gates, conv0, s0, conv_w, hp, nw.reshape(1, dk), s_stack)


def _mlstm_kernel(q_ref, k_ref, v_ref, z_ref, og_ref, gt_ref, c0_ref, n0_ref, m0_ref, hp_ref, nw_ref, stack_ref,
                  o_ref, co_ref, no_ref, mo_ref,
                  li_scr, lf_scr, cext, m_scr, *, c, nch, sb, tv, dqk, dv):
    gi = pl.program_id(1)
    tg = c * nch
    lane1 = _iota((dqk, 128), 1)

    @pl.when(gi == 0)
    def _():
        for s in range(sb):
            for h in range(N_HEADS):
                cext[s * N_HEADS + h, :, 0:dv] = c0_ref[s, h]
                cext[s * N_HEADS + h, :, dv:dv + 128] = jnp.where(lane1 == 0, n0_ref[s, h], 0.0)
            m_scr[s * 8:(s + 1) * 8, :] = m0_ref[s]

    hp = hp_ref[...]
    lane = _iota((tg, 128), 1)
    for s in range(sb):
        gt = gt_ref[s]
        li = gt + hp[0:1, :]
        x = gt + hp[1:2, :]
        lf = jnp.minimum(x, 0.0) - jnp.log(1.0 + jnp.exp(-jnp.abs(x)))
        lf = jnp.where(lane >= N_HEADS, jnp.where(lane < 2 * N_HEADS, lf, 0.0), 0.0)
        if tv < c:
            valid = (_iota((tg, 128), 0) % c) < tv
            lf = jnp.where(valid, lf, 0.0)
            li = jnp.where(valid, li, NEG)
        li_scr[s] = li
        lf_scr[s] = lf

    row = _iota((c, c), 0)
    col = _iota((c, c), 1)
    ige = row >= col
    tri = jnp.where(ige, 1.0, 0.0).astype(BF16)
    ones_b = jnp.ones((c, 128), BF16)
    lane_c = _iota((c, 128), 1)
    one_col = jnp.where(lane_c == 0, 1.0, 0.0)
    nw = nw_ref[...]

    def prep(clist):
        items = [(ci, h) for ci in range(len(clist)) for h in range(N_HEADS)]
        bcums = [_exact_dot(tri, lf_scr[s, pl.ds(r0, c), :]) for s, r0 in clist]
        lichs = [li_scr[s, pl.ds(r0, c), :] for s, r0 in clist]
        qbs = [(q_ref[clist[ci][0], pl.ds(clist[ci][1], c), h * dqk:(h + 1) * dqk] * (dqk ** -0.5)).astype(BF16)
               for ci, h in items]
        ks = [k_ref[clist[ci][0], pl.ds(clist[ci][1], c), h * dqk:(h + 1) * dqk] for ci, h in items]
        qk_raws = [_dot_nt(qb, k.astype(BF16)) for qb, k in zip(qbs, ks)]
        bcols = [bcums[ci][:, N_HEADS + h:N_HEADS + h + 1] for ci, h in items]
        licols = [lichs[ci][:, h:h + 1] for ci, h in items]
        rowvs = [_exact_dot_nt(ones_b, jnp.where(lane_c == 0, licol - bcol, 0.0))
                 for licol, bcol in zip(licols, bcols)]
        out = []
        for i, (ci, h) in enumerate(items):
            dmat = jnp.where(ige, bcols[i] + rowvs[i], NEG)
            v = v_ref[clist[ci][0], pl.ds(clist[ci][1], c), h * dv:(h + 1) * dv]
            blast = bcums[ci][c - 1:c, N_HEADS + h:N_HEADS + h + 1]
            out.append((qbs[i], ks[i], jnp.concatenate([v, one_col], axis=1).astype(BF16), bcols[i], dmat,
                        jnp.max(dmat, axis=-1, keepdims=True), qk_raws[i], blast, blast - bcols[i] + licols[i]))
        return [out[ci * N_HEADS:(ci + 1) * N_HEADS] for ci in range(len(clist))]

    def chunks(clist):
        hs = range(N_HEADS)
        for (s, r0), heads in zip(clist, prep(clist)):
            mprevs = [m_scr[s * 8:s * 8 + 1, h:h + 1] for h in hs]
            c_olds = [cext[s * N_HEADS + h] for h in hs]
            qcs = [_dot(heads[h][0], c_olds[h].astype(BF16)) for h in hs]
            inters = [heads[h][3] + mprevs[h] for h in hs]
            mts = [jnp.maximum(inters[h], heads[h][5]) for h in hs]
            mnews = [mts[h][c - 1:c, :] for h in hs]
            for h in hs:
                k, vext, blast, wk_log = heads[h][1], heads[h][2], heads[h][7], heads[h][8]
                cext[s * N_HEADS + h] = (c_olds[h] * jnp.exp(blast + mprevs[h] - mnews[h])
                                         + _dot_tn((k * jnp.exp(wk_log - mnews[h])).astype(BF16), vext))
                m_scr[s * 8:s * 8 + 1, h:h + 1] = mnews[h]
            for h in hs:
                qk = heads[h][6] * jnp.exp(heads[h][4] - mts[h])
                tot = jnp.exp(inters[h] - mts[h]) * qcs[h] + _dot(qk.astype(BF16), heads[h][2])
                hh = tot[:, :dv] / jnp.maximum(jnp.abs(tot[:, dv:dv + 1]), jnp.exp(-mts[h]))
                og = og_ref[s, pl.ds(r0, c), h * dv:(h + 1) * dv]
                zz = z_ref[s, pl.ds(r0, c), h * dv:(h + 1) * dv]
                o_ref[s, pl.ds(r0, c), h * dv:(h + 1) * dv] = _rms(_sigmoid(og) * hh, nw) * _silu(zz)

    chunks([(s, i * c) for i in range(nch) for s in range(sb)])

    @pl.when(gi == pl.num_programs(1) - 1)
    def _():
        for s in range(sb):
            for h in range(N_HEADS):
                co_ref[s, h] = cext[s * N_HEADS + h, :, 0:dv]
                no_ref[s, h] = cext[s * N_HEADS + h, :, dv:dv + 1]
            mo_ref[s] = m_scr[s * 8:(s + 1) * 8, :]


def _mlstm(q, k, v, z, og, gates, c0, c_idx, n0, m0, hp, nw, c_stack, out_idx, *, bn, c, nch, sb, tv):
    n = q.shape[0]
    dqk, dv = c0.shape[-2], c0.shape[-1]
    tg = c * nch
    rows = n // bn
    steps = rows // tg
    hq, hv = N_HEADS * dqk, N_HEADS * dv
    row = lambda b, g: (b, g, 0)
    q, k, v, z, og, gates = (a.reshape(bn, rows, a.shape[1]) for a in (q, k, v, z, og, gates))
    kern = functools.partial(_mlstm_kernel, c=c, nch=nch, sb=sb, tv=tv, dqk=dqk, dv=dv)
    out = pl.pallas_call(
        kern, grid=(bn // sb, steps),
        in_specs=[pl.BlockSpec((sb, tg, hq), row), pl.BlockSpec((sb, tg, hq), row),
                  pl.BlockSpec((sb, tg, hv), row), pl.BlockSpec((sb, tg, hv), row),
                  pl.BlockSpec((sb, tg, hv), row), pl.BlockSpec((sb, tg, 128), row),
                  pl.BlockSpec((None, sb, N_HEADS, dqk, dv), lambda b, g: (c_idx, b, 0, 0, 0)),
                  pl.BlockSpec((sb, N_HEADS, dqk, 1), lambda b, g: (b, 0, 0, 0)),
                  pl.BlockSpec((sb, 8, 128), lambda b, g: (b, 0, 0)),
                  pl.BlockSpec((8, 128), lambda b, g: (0, 0)),
                  pl.BlockSpec((1, dv), lambda b, g: (0, 0)),
                  pl.BlockSpec(memory_space=pl.ANY)],
        out_specs=[pl.BlockSpec((sb, tg, hv), row),
                   pl.BlockSpec((None, sb, N_HEADS, dqk, dv), lambda b, g: (out_idx, b, 0, 0, 0)),
                   pl.BlockSpec((sb, N_HEADS, dqk, 1), lambda b, g: (b, 0, 0, 0)),
                   pl.BlockSpec((sb, 8, 128), lambda b, g: (b, 0, 0))],
        out_shape=[jax.ShapeDtypeStruct((bn, rows, hv), F32),
                   jax.ShapeDtypeStruct(c_stack.shape, F32),
                   jax.ShapeDtypeStruct((bn, N_HEADS, dqk, 1), F32),
                   jax.ShapeDtypeStruct((bn, 8, 128), F32)],
        input_output_aliases={11: 1},
        scratch_shapes=[pltpu.VMEM((sb, tg, 128), F32), pltpu.VMEM((sb, tg, 128), F32),
                        pltpu.VMEM((sb * N_HEADS, dqk, dv + 128), F32), pltpu.VMEM((sb * 8, 128), F32)],
        compiler_params=_cparams(("parallel", "arbitrary")), name="mlstm")(
            q, k, v, z, og, gates, c0, n0, m0, hp, nw.reshape(1, dv), c_stack)
    return (out[0].reshape(n, hv),) + tuple(out[1:])


def _lambda(lq, lam_init):
    a = jnp.sum(lq[0:1, :] * lq[1:2, :], axis=-1, keepdims=True)
    b = jnp.sum(lq[2:3, :] * lq[3:4, :], axis=-1, keepdims=True)
    return jnp.exp(a) - jnp.exp(b) + lam_init


def _attn_kernel(wq_ref, k_ref, vt_ref, z_ref, lq_ref, nw_ref, o_ref, acc_scr, *, t, hpb, lam_init):
    qi = pl.program_id(2)
    dh = k_ref.shape[1] // hpb
    nl = 4 * t
    acc_scr[...] = jnp.zeros(acc_scr.shape, F32)

    def step(jp, carry, masked):
        rows = [pl.multiple_of((2 * jp + i) * t, t) for i in range(2)]
        wqs = [jnp.concatenate([wq_ref[0, a], wq_ref[1, a]], axis=1) for a in range(hpb)]
        scores = [[_dot(k_ref[pl.ds(rows[i], t), a * dh:(a + 1) * dh], wqs[a]) for a in range(hpb)]
                  for i in range(2)]
        out = list(carry)
        for i in range(2):
            for a in range(hpb):
                m_prev, l_prev = out[a]
                s = scores[i][a]
                if masked:
                    lane = _iota((t, nl), 1)
                    qpos = (lane // (2 * t)) * t + lane % t
                    s = jnp.where(_iota((t, nl), 0) + i * t <= qpos, s, NEG)
                m_new = jnp.maximum(m_prev, jnp.max(s, axis=0, keepdims=True))
                alpha = jnp.exp2(m_prev - m_new)
                p = jnp.exp2(s - m_new)
                l_new = alpha * l_prev + jnp.sum(p, axis=0, keepdims=True)
                acc_scr[a] = alpha * acc_scr[a] + _dot(vt_ref[2 * jp + i, a * dh:(a + 1) * dh, :], p.astype(BF16))
                out[a] = (m_new, l_new)
        return tuple(out)

    init = tuple((jnp.full((1, nl), NEG, F32), jnp.zeros((1, nl), F32)) for _ in range(hpb))
    carry = lax.fori_loop(0, qi, lambda j, c: step(j, c, False), init)
    carry = step(qi, carry, True)

    lam = _lambda(lq_ref[...], lam_init)
    nw = nw_ref[...]
    for a in range(hpb):
        acc = acc_scr[a] * (1.0 / carry[a][1])
        for qb in range(2):
            b0 = 2 * t * qb
            o = (acc[:, b0:b0 + t] - lam * acc[:, b0 + t:b0 + 2 * t]).T
            zz = z_ref[qb * t:(qb + 1) * t, a * dh:(a + 1) * dh]
            o_ref[qb * t:(qb + 1) * t, a * dh:(a + 1) * dh] = _rms(o, nw) * (1.0 - lam_init) * _silu(zz)


def _attn_prompt(wq, k16, vt, z, lq, nw, *, bn, seq, lam_init):
    n = k16.shape[0]
    t = vt.shape[2]
    dh = wq.shape[2]
    hpb = 4
    nq = seq // t
    nq2 = nq // 2
    kern = functools.partial(_attn_kernel, t=t, hpb=hpb, lam_init=lam_init)
    return pl.pallas_call(
        kern, grid=(bn, N_HEADS // hpb, nq2),
        in_specs=[pl.BlockSpec((2, hpb, dh, 2 * t), lambda b, h, i: (b * nq2 + i, h, 0, 0)),
                  pl.BlockSpec((seq, hpb * dh), lambda b, h, i: (b, h), pipeline_mode=pl.Buffered(1)),
                  pl.BlockSpec((nq, hpb * dh, t), lambda b, h, i: (b, h, 0), pipeline_mode=pl.Buffered(1)),
                  pl.BlockSpec((2 * t, hpb * dh), lambda b, h, i: (b * nq2 + i, h)),
                  pl.BlockSpec(lq.shape, lambda b, h, i: (0, 0)),
                  pl.BlockSpec((1, dh), lambda b, h, i: (0, 0))],
        out_specs=pl.BlockSpec((2 * t, hpb * dh), lambda b, h, i: (b * nq2 + i, h)),
        out_shape=jax.ShapeDtypeStruct((n, N_HEADS * dh), F32),
        scratch_shapes=[pltpu.VMEM((hpb, dh, 4 * t), F32)],
        compiler_params=_cparams(("parallel", "parallel", "arbitrary")), name="diff_attn_prompt")(
            wq, k16, vt, z, lq, nw.reshape(1, dh))


def _attn_s_kernel(pt_ref, q_ref, kn_ref, vn_ref, z_ref, lq_ref, nw_ref, *rest, pg, tv, lam_init):
    k_pages = rest[:pg]
    v_pages = rest[pg:2 * pg]
    o_ref, qbd, bias, m_scr, l_scr, acc_scr = rest[2 * pg:]
    j = pl.program_id(1)
    tp, hd = q_ref.shape
    dh = hd // N_HEADS
    dqk = dh // 2
    prow = k_pages[0].shape[0]
    rph = 2 * tp

    @pl.when(j == 0)
    def _():
        q = q_ref[...] * (dqk ** -0.5)
        lane = _iota((tp, dh), 1)
        for h in range(N_HEADS):
            qh = q[:, h * dh:(h + 1) * dh]
            qbd[h * rph:h * rph + tp, :] = jnp.where(lane < dqk, qh, 0.0)
            qbd[h * rph + tp:(h + 1) * rph, :] = jnp.where(lane >= dqk, qh, 0.0)
        rhead = _iota(bias.shape, 0) // rph
        chead = _iota(bias.shape, 1) % N_HEADS
        bias[...] = jnp.where(rhead == chead, 0.0, NEG)
        m_scr[...] = jnp.full(m_scr.shape, NEG, F32)
        l_scr[...] = jnp.zeros(l_scr.shape, F32)
        acc_scr[...] = jnp.zeros(acc_scr.shape, F32)

    def update(s, pv):
        m_prev = m_scr[...]
        m_new = jnp.maximum(m_prev, jnp.max(s, axis=-1, keepdims=True))
        alpha = jnp.exp(m_prev - m_new)
        p = jnp.exp(s - m_new[:, 0:1])
        l_scr[...] = alpha * l_scr[...] + jnp.sum(p, axis=-1, keepdims=True)
        acc_scr[...] = alpha * acc_scr[...] + pv(p)
        m_scr[...] = m_new

    qb = qbd[...].astype(BF16)
    bs_ = bias[...]
    s = jnp.concatenate([_dot_nt(qb, kp[...].astype(BF16)) + bs_ for kp in k_pages], axis=1)

    def pv_pages(p):
        pb = p.astype(BF16)
        acc = _dot(pb[:, 0:prow], v_pages[0][...].astype(BF16))
        for i in range(1, pg):
            acc = acc + _dot(pb[:, i * prow:(i + 1) * prow], v_pages[i][...].astype(BF16))
        return acc

    update(s, pv_pages)

    @pl.when(j == pl.num_programs(1) - 1)
    def _():
        qf = qbd[...]
        kn = kn_ref[...]
        vn = vn_ref[...]
        sn = jnp.concatenate([_dot_nt(qf[h * rph:(h + 1) * rph], kn[:, h * dh:(h + 1) * dh])
                              for h in range(N_HEADS)], axis=0)
        tq = _iota(sn.shape, 0) % tp
        tk = _iota(sn.shape, 1)
        sn = jnp.where(tk <= tq, jnp.where(tk < tv, sn, NEG), NEG)
        update(sn, lambda p: jnp.concatenate(
            [_dot(p[h * rph:(h + 1) * rph], vn[:, h * dh:(h + 1) * dh]) for h in range(N_HEADS)], axis=0))
        acc = acc_scr[...] / l_scr[...]
        lam = _lambda(lq_ref[...], lam_init)
        nw = nw_ref[...]
        normed = jnp.concatenate(
            [_rms(acc[h * rph:h * rph + tp] - lam * acc[h * rph + tp:(h + 1) * rph], nw) for h in range(N_HEADS)],
            axis=1)
        o_ref[...] = normed * (1.0 - lam_init) * _silu(z_ref[...])


def _attn_sample(page_table, q, kn, vn, z, lq, nw, cache_k, cache_v, *, e, tv, lam_init):
    bs, npages = page_table.shape
    n, hd = q.shape
    tp = n // bs
    dh = hd // N_HEADS
    pg = math.gcd(npages, PAGES_PER_STEP)
    prow = cache_k.shape[2]
    steps = npages // pg
    rows = 2 * N_HEADS * tp
    tok = lambda b, j, pt: (b, 0)
    page_specs = [pl.BlockSpec((None, None, prow, dh),
                               lambda b, j, pt, i=i: (e, pt[b, j * pg + i], 0, 0)) for i in range(pg)]
    kern = functools.partial(_attn_s_kernel, pg=pg, tv=tv, lam_init=lam_init)
    grid_spec = pltpu.PrefetchScalarGridSpec(
        num_scalar_prefetch=1, grid=(bs, steps),
        in_specs=[pl.BlockSpec((tp, hd), tok), pl.BlockSpec((tp, hd), tok), pl.BlockSpec((tp, hd), tok),
                  pl.BlockSpec((tp, hd), tok),
                  pl.BlockSpec(lq.shape, lambda b, j, pt: (0, 0)),
                  pl.BlockSpec((1, dh), lambda b, j, pt: (0, 0))] + page_specs + page_specs,
        out_specs=pl.BlockSpec((tp, hd), tok),
        scratch_shapes=[pltpu.VMEM((rows, dh), F32), pltpu.VMEM((rows, prow), F32), pltpu.VMEM((rows, 128), F32),
                        pltpu.VMEM((rows, 128), F32), pltpu.VMEM((rows, dh), F32)])
    return pl.pallas_call(
        kern, grid_spec=grid_spec, out_shape=jax.ShapeDtypeStruct((n, hd), F32),
        compiler_params=_cparams(("parallel", "arbitrary")), name="diff_attn_sample")(
            page_table, q, kn, vn, z, lq, nw.reshape(1, dh), *([cache_k] * pg), *([cache_v] * pg))


def _rope_tables(pos, dqk):
    rope_dim = dqk // 4
    half = rope_dim // 2
    inv = jnp.power(ROPE_THETA, -jnp.arange(half, dtype=F32) / half)
    ang = pos.astype(F32)[:, None] * inv[None, :]
    lane = jnp.arange(128) % dqk
    idx = lane % half
    cos = jnp.where(lane < rope_dim, jnp.cos(ang)[:, idx], 1.0)
    sin = jnp.sin(ang)[:, idx]
    sa = jnp.where((lane >= half) & (lane < rope_dim), sin, 0.0)
    sb = jnp.where(lane < half, -sin, 0.0)
    return cos.astype(F32), sa.astype(F32), sb.astype(F32)


def _lane_row(vals, offset):
    out = jnp.zeros((128,), F32)
    return out.at[offset:offset + vals.shape[0]].set(vals.astype(F32))


def kernel(x_prompt, x_sample, cache_k, cache_v, page_table, state_gdn_conv, state_gdn_s, state_mlstm_c,
           state_mlstm_n, state_mlstm_m, norm_w, final_norm_w, w_in_even, w_out_even, conv_w, a_log, dt_bias,
           gdn_norm_w, lambda_qk, diff_norm_w, w_in_odd, w_out_odd, b_i, b_f, mlstm_norm_w):
    bp, seq, d = x_prompt.shape
    bs, ts, _ = x_sample.shape
    depth = norm_w.shape[0]
    tp = SAMPLE_PAD
    dk_a = state_gdn_s.shape[-1]
    conv_ch = state_gdn_conv.shape[-1]
    w_a = N_HEADS * dk_a
    dh_b = cache_v.shape[-1]
    dqk_b = dh_b // 2
    w_b = N_HEADS * dh_b
    dqk_c, dv_c = state_mlstm_c.shape[-2], state_mlstm_c.shape[-1]
    hq_c, w_c = N_HEADS * dqk_c, N_HEADS * dv_c
    past = page_table.shape[1] * cache_k.shape[2]
    n_pool, page = cache_k.shape[1], cache_k.shape[2]

    xp = x_prompt.reshape(bp * seq, d)
    xs = jnp.pad(x_sample, ((0, 0), (0, tp - ts), (0, 0))).reshape(bs * tp, d)
    tm_p = math.gcd(seq, 256)
    tm_s = math.gcd(bs * tp, 256)
    nch_p = math.gcd(seq // CHUNK, 8)
    seq_s = math.gcd(bs, 8)
    seq_c = math.gcd(bs, 4)

    rope_p = _rope_tables(jnp.arange(seq), dqk_b)
    rope_s = _rope_tables(past + (jnp.arange(tm_s) % tp), dqk_b)
    half = dqk_b // 8
    ang_t = (jnp.power(ROPE_THETA, -jnp.arange(half, dtype=F32) / half)[:, None]
             * jnp.arange(seq).astype(F32)[None, :])
    cos_t, sin_t = jnp.cos(ang_t), jnp.sin(ang_t)

    even_w = conv_ch + w_a
    seg_even = [(0, even_w), (even_w, w_b), (even_w + w_b, w_b), (even_w + 2 * w_b, w_b),
                (even_w + 3 * w_b, w_b), (even_w + 4 * w_b, 128)]
    seg_even_p = [seg_even[0]] + seg_even[2:]
    rope_even = tuple(range(even_w // 128, (even_w + 2 * w_b) // 128))
    odd_w = 2 * hq_c + 3 * w_c
    seg_odd = [(0, hq_c), (hq_c, hq_c), (2 * hq_c, w_c), (2 * hq_c + w_c, w_c), (2 * hq_c + 2 * w_c, w_c),
               (odd_w, 128)]

    ck = cache_k.reshape(cache_k.shape[0], n_pool, page * N_HEADS, dh_b)
    cv = cache_v.reshape(cache_v.shape[0], n_pool, page * N_HEADS, dh_b)

    n_even = (depth + 1) // 2
    krows_p = jnp.zeros((n_even, bp * seq * N_HEADS, dh_b), F32)
    vrows_p = jnp.zeros((n_even, bp * seq * N_HEADS, dh_b), F32)
    k_rows_s, v_rows_s = [], []
    n_odd = depth // 2
    s_p = jnp.zeros((n_even, bp, N_HEADS, dk_a, dk_a), F32)
    s_s = jnp.zeros((n_even, bs, N_HEADS, dk_a, dk_a), F32)
    c_p = jnp.zeros((n_odd, bp, N_HEADS, dqk_c, dv_c), F32)
    c_s = jnp.zeros((n_odd, bs, N_HEADS, dqk_c, dv_c), F32)
    conv_p, conv_s = [], []
    n_p, m_p, n_s, m_s = [], [], [], []

    pend_p = pend_s = None
    for layer in range(depth):
        if layer % 2 == 0:
            e = layer // 2
            lam_init = 0.8 - 0.6 * math.exp(-0.3 * layer)
            w = w_in_even[e]
            g0 = even_w
            w_r = jnp.concatenate([w[:, :g0], w[:, g0 + 2 * N_HEADS:], w[:, g0:g0 + 2 * N_HEADS],
                                   jnp.zeros((d, 128 - 2 * N_HEADS), F32)], axis=1).astype(BF16)
            w_o = w_out_even[e].astype(BF16)
            hp = jnp.zeros((8, 128), F32).at[0].set(_lane_row(a_log[e], N_HEADS)).at[1].set(
                _lane_row(dt_bias[e], N_HEADS))
            lq = lambda_qk[e]

            wq_t = w[:, g0 + 2 * N_HEADS:g0 + 2 * N_HEADS + w_b].T.astype(BF16)
            wv_t = w[:, g0 + 2 * N_HEADS + 2 * w_b:g0 + 2 * N_HEADS + 3 * w_b].T.astype(BF16)
            outs = _proj(xp, norm_w[layer], w_r, seg_even_p, (rope_even, rope_p), tm_p, k16_seg=1,
                         feat=(wq_t, wv_t, cos_t, sin_t), rows=((1, 2), e, (krows_p, vrows_p)), pre=pend_p)
            if pend_p is not None:
                xp = outs.pop()
            az, krows_p, vrows_p, zb, gt, k16, wq, vt = outs
            oa, cst, s_p = _gdn(az, gt, jnp.zeros((bp, 8, conv_ch), F32),
                                jnp.zeros((1, bp, N_HEADS, dk_a, dk_a), F32), 0,
                                conv_w[e], hp, gdn_norm_w[e], s_p, e, bn=bp, c=CHUNK, nch=nch_p, tv=CHUNK)
            ob = _attn_prompt(wq, k16, vt, zb, lq, diff_norm_w[e], bn=bp, seq=seq, lam_init=lam_init)
            pend_p = (oa, ob, w_a, 0, w_o)
            conv_p.append(cst[:, 8 - (CONV_W - 1):])

            outs = _proj(xs, norm_w[layer], w_r, seg_even, (rope_even, rope_s), tm_s, pre=pend_s)
            if pend_s is not None:
                xs = outs.pop()
            az, qb, kb, vb, zb, gt = outs
            conv0 = jnp.pad(state_gdn_conv[e], ((0, 0), (8 - (CONV_W - 1), 0), (0, 0)))
            oa, cst, s_s = _gdn(az, gt, conv0, state_gdn_s, e, conv_w[e], hp, gdn_norm_w[e], s_s, e,
                                bn=bs, c=tp, nch=seq_s, tv=ts, per_seq=True)
            ob = _attn_sample(page_table, qb, kb, vb, zb, lq, diff_norm_w[e], ck, cv, e=e, tv=ts,
                              lam_init=lam_init)
            pend_s = (oa, ob, w_a, 0, w_o)
            k_rows_s.append(kb.reshape(bs, tp, N_HEADS, dh_b)[:, :ts])
            v_rows_s.append(vb.reshape(bs, tp, N_HEADS, dh_b)[:, :ts])
            conv_s.append(cst[:, 8 - (CONV_W - 1):])
        else:
            o = layer // 2
            w_r = jnp.concatenate([w_in_odd[o], jnp.zeros((d, 128 - 2 * N_HEADS), F32)], axis=1).astype(BF16)
            w_o = w_out_odd[o].astype(BF16)
            hp = jnp.zeros((8, 128), F32).at[0].set(_lane_row(b_i[o], 0)).at[1].set(_lane_row(b_f[o], N_HEADS))

            outs = _proj(xp, norm_w[layer], w_r, seg_odd, None, tm_p, pre=pend_p)
            if pend_p is not None:
                xp = outs.pop()
            q, k, v, z, og, gt = outs
            hm, c_p, nf, mf = _mlstm(q, k, v, z, og, gt, jnp.zeros((1, bp, N_HEADS, dqk_c, dv_c), F32), 0,
                                     jnp.zeros((bp, N_HEADS, dqk_c, 1), F32), jnp.zeros((bp, 8, 128), F32),
                                     hp, mlstm_norm_w[o], c_p, o, bn=bp, c=CHUNK, nch=math.gcd(seq // CHUNK, 2),
                                     sb=1, tv=CHUNK)
            pend_p = (hm, hm, w_c // 2, 1, w_o)
            n_p.append(nf[..., 0])
            m_p.append(mf[:, 0, :N_HEADS])

            outs = _proj(xs, norm_w[layer], w_r, seg_odd, None, tm_s, pre=pend_s)
            if pend_s is not None:
                xs = outs.pop()
            q, k, v, z, og, gt = outs
            m0 = jnp.zeros((bs, 8, 128), F32).at[:, 0, :N_HEADS].set(state_mlstm_m[o])
            hm, c_s, nf, mf = _mlstm(q, k, v, z, og, gt, state_mlstm_c, o, state_mlstm_n[o][..., None], m0,
                                     hp, mlstm_norm_w[o], c_s, o, bn=bs, c=tp, nch=1, sb=seq_c, tv=ts)
            pend_s = (hm, hm, w_c // 2, 1, w_o)
            n_s.append(nf[..., 0])
            m_s.append(mf[:, 0, :N_HEADS])

    xp = _out_proj(*pend_p, xp, final_norm_w, True, tm_p)
    xs = _out_proj(*pend_s, xs, final_norm_w, True, tm_s)
    y_prompt = xp.reshape(bp, seq, d)
    y_sample = xs.reshape(bs, tp, d)[:, :ts]
    st = jnp.stack
    kv_shape = (n_even, bp, seq, N_HEADS, dh_b)
    return (y_prompt, y_sample, krows_p.reshape(kv_shape), vrows_p.reshape(kv_shape), st(k_rows_s), st(v_rows_s),
            st(conv_p), s_p, st(conv_s), s_s,
            c_p, st(n_p), st(m_p), c_s, st(n_s), st(m_s))
```

```python
import functools
import math

import jax
import jax.numpy as jnp
from jax import lax
from jax.experimental import pallas as pl
from jax.experimental.pallas import tpu as pltpu

F32 = jnp.float32
BF16 = jnp.bfloat16

EPS = 1e-6
NEG = -1e30
N_HEADS = 4
CONV_W = 4
ROPE_THETA = 500000.0
CHUNK = 64
SAMPLE_PAD = 8
PAGES_PER_STEP = 16
V7X_VMEM_LIMIT = 56 * 1024 * 1024


def _cparams(sem):
    return pltpu.CompilerParams(dimension_semantics=sem, vmem_limit_bytes=V7X_VMEM_LIMIT)


def _dot(a, b):
    return jnp.dot(a, b, preferred_element_type=F32)


def _dot_nt(a, b):
    return lax.dot_general(a, b, (((1,), (1,)), ((), ())), preferred_element_type=F32)


def _dot_tn(a, b):
    return lax.dot_general(a, b, (((0,), (0,)), ((), ())), preferred_element_type=F32)


def _split3(x):
    hi = x.astype(BF16)
    r = x - hi.astype(F32)
    mid = r.astype(BF16)
    lo = (r - mid.astype(F32)).astype(BF16)
    return hi, mid, lo


def _exact_dot(a_bf16, x):
    hi, mid, lo = _split3(x)
    return _dot(a_bf16, hi) + (_dot(a_bf16, mid) + _dot(a_bf16, lo))


def _exact_dot_nt(a_bf16, x):
    hi, mid, lo = _split3(x)
    return _dot_nt(a_bf16, hi) + (_dot_nt(a_bf16, mid) + _dot_nt(a_bf16, lo))


def _sigmoid(x):
    return 1.0 / (1.0 + jnp.exp(-x))


def _silu(x):
    return x * _sigmoid(x)


def _softplus(x):
    return jnp.maximum(x, 0.0) + jnp.log(1.0 + jnp.exp(-jnp.abs(x)))


def _rms(x, w):
    return x * lax.rsqrt(jnp.mean(x * x, axis=-1, keepdims=True) + EPS) * w


def _iota(shape, dim):
    return lax.broadcasted_iota(jnp.int32, shape, dim)


def _proj_kernel(*refs, segs, rope_blocks, k16_seg, feat_major, row_segs, fuse_out):
    n_out = len(segs)
    pos = 3
    x_ref, nw_ref, w_ref = refs[:3]
    if rope_blocks:
        cos_ref, sa_ref, sb_ref = refs[pos:pos + 3]
        pos += 3
    if feat_major:
        wqt_ref, wvt_ref, cost_ref, sint_ref = refs[pos:pos + 4]
        pos += 4
    if fuse_out:
        a_ref, b_ref, wo_ref = refs[pos:pos + 3]
        pos += 3
    pos += len(row_segs)
    outs = refs[pos:pos + n_out]
    pos += n_out
    x = x_ref[...]
    if fuse_out:
        ka = a_ref.shape[1]
        x = x + (_dot(a_ref[...].astype(BF16), wo_ref[0:ka, :]) + _dot(b_ref[...].astype(BF16), wo_ref[ka:, :]))
        refs[-1][...] = x
    h = _rms(x, nw_ref[...]).astype(BF16)
    for si, ((c0, width), o_ref) in enumerate(zip(segs, outs)):
        for s0 in range(0, width, 512):
            sw = min(512, width - s0)
            acc = _dot(h, w_ref[:, c0 + s0:c0 + s0 + sw])
            if rope_blocks and (c0 + s0) // 128 in rope_blocks:
                cos = cos_ref[...]
                sa = sa_ref[...]
                sb = sb_ref[...]
                parts = []
                for t0 in range(0, sw, 128):
                    a = acc[:, t0:t0 + 128]
                    parts.append(a * cos + pltpu.roll(a, 8, 1) * sa + pltpu.roll(a, 120, 1) * sb)
                acc = jnp.concatenate(parts, axis=1)
            if si in row_segs:
                dh = sw // N_HEADS
                for hd in range(N_HEADS):
                    o_ref[pl.ds(hd, acc.shape[0], stride=N_HEADS), :] = acc[:, hd * dh:(hd + 1) * dh]
            else:
                o_ref[:, s0:s0 + sw] = acc
            if si == k16_seg:
                refs[pos][:, s0:s0 + sw] = acc.astype(BF16)
    if feat_major:
        wq_ref, vt_ref = refs[pos + 1], refs[pos + 2]
        tm = x.shape[0]
        vt_ref[0] = _dot_nt(wvt_ref[...], h).astype(BF16)
        qt = _dot_nt(wqt_ref[...], h)
        dqk = qt.shape[0] // (2 * N_HEADS)
        half = dqk // 8
        cos = cost_ref[...]
        sin = sint_ref[...]
        scale = (dqk ** -0.5) * math.log2(math.e)
        zero = jnp.zeros((dqk, tm), F32)
        for hd in range(N_HEADS):
            maps = []
            for m in range(2):
                r0 = (2 * hd + m) * dqk
                x1 = qt[r0:r0 + half]
                x2 = qt[r0 + half:r0 + 2 * half]
                maps.append(jnp.concatenate([x1 * cos - x2 * sin, x2 * cos + x1 * sin,
                                             qt[r0 + 2 * half:r0 + dqk]], axis=0) * scale)
            top = jnp.concatenate([maps[0], zero], axis=1)
            bot = jnp.concatenate([zero, maps[1]], axis=1)
            wq_ref[0, hd] = jnp.concatenate([top, bot], axis=0).astype(BF16)


def _proj(x, nw, w, segs, rope, tm, k16_seg=None, feat=None, rows=None, pre=None):
    n, d = x.shape
    ncols = w.shape[1]
    grid = (n // tm,)
    in_specs = [pl.BlockSpec((tm, d), lambda i: (i, 0)),
                pl.BlockSpec((1, d), lambda i: (0, 0)),
                pl.BlockSpec((d, ncols), lambda i: (0, 0))]
    args = [x, nw.reshape(1, d), w]
    rope_blocks = ()
    if rope is not None:
        rope_blocks, tables = rope
        nt = tables[0].shape[0] // tm
        for t in tables:
            in_specs.append(pl.BlockSpec((tm, 128), lambda i, nt=nt: (i % nt, 0)))
            args.append(t)
    out_shape = [jax.ShapeDtypeStruct((n, width), F32) for _, width in segs]
    out_specs = [pl.BlockSpec((tm, width), lambda i: (i, 0)) for _, width in segs]
    if feat is not None:
        wqt, wvt, cost, sint = feat
        ntt = cost.shape[1] // tm
        in_specs += [pl.BlockSpec(wqt.shape, lambda i: (0, 0)), pl.BlockSpec(wvt.shape, lambda i: (0, 0)),
                     pl.BlockSpec((cost.shape[0], tm), lambda i, ntt=ntt: (0, i % ntt)),
                     pl.BlockSpec((sint.shape[0], tm), lambda i, ntt=ntt: (0, i % ntt))]
        args += [wqt, wvt, cost, sint]
    if pre is not None:
        a, b, ka, b_col, w_out = pre
        kb = w_out.shape[0] - ka
        in_specs += [pl.BlockSpec((tm, ka), lambda i: (i, 0)), pl.BlockSpec((tm, kb), lambda i: (i, b_col)),
                     pl.BlockSpec(w_out.shape, lambda i: (0, 0))]
        args += [a, b, w_out]
    row_segs, aliases = (), {}
    if rows is not None:
        row_segs, layer, stacked = rows
        for si, arr in zip(row_segs, stacked):
            aliases[len(args)] = si
            in_specs.append(pl.BlockSpec(memory_space=pl.ANY))
            args.append(arr)
            out_shape[si] = jax.ShapeDtypeStruct(arr.shape, F32)
            out_specs[si] = pl.BlockSpec((None, tm * N_HEADS, arr.shape[2]), lambda i, layer=layer: (layer, i, 0))
    if k16_seg is not None:
        kw = segs[k16_seg][1]
        out_shape.append(jax.ShapeDtypeStruct((n, kw), BF16))
        out_specs.append(pl.BlockSpec((tm, kw), lambda i: (i, 0)))
    if feat is not None:
        hq, hv = wqt.shape[0], wvt.shape[0]
        dh = hq // N_HEADS
        out_shape += [jax.ShapeDtypeStruct((n // tm, N_HEADS, dh, 2 * tm), BF16),
                      jax.ShapeDtypeStruct((n // tm, hv, tm), BF16)]
        out_specs += [pl.BlockSpec((1, N_HEADS, dh, 2 * tm), lambda i: (i, 0, 0, 0)),
                      pl.BlockSpec((1, hv, tm), lambda i: (i, 0, 0))]
    if pre is not None:
        out_shape.append(jax.ShapeDtypeStruct((n, d), F32))
        out_specs.append(pl.BlockSpec((tm, d), lambda i: (i, 0)))
    return list(pl.pallas_call(
        functools.partial(_proj_kernel, segs=tuple(segs), rope_blocks=tuple(rope_blocks), k16_seg=k16_seg,
                          feat_major=feat is not None, row_segs=tuple(row_segs), fuse_out=pre is not None),
        grid=grid, in_specs=in_specs, out_specs=out_specs, out_shape=out_shape, input_output_aliases=aliases,
        compiler_params=_cparams(("parallel",)), name="norm_proj")(*args))


def _out_kernel(a_ref, b_ref, w_ref, x_ref, fw_ref, o_ref, *, final):
    ka = a_ref.shape[1]
    y = _dot(a_ref[...].astype(BF16), w_ref[0:ka, :]) + _dot(b_ref[...].astype(BF16), w_ref[ka:, :])
    xn = x_ref[...] + y
    if final:
        xn = _rms(xn, fw_ref[...])
    o_ref[...] = xn


def _out_proj(a, b, ka, b_col, w, x, fw, final, tm):
    n, d = x.shape
    kb = w.shape[0] - ka
    return pl.pallas_call(
        functools.partial(_out_kernel, final=final),
        grid=(n // tm,),
        in_specs=[pl.BlockSpec((tm, ka), lambda i: (i, 0)),
                  pl.BlockSpec((tm, kb), lambda i: (i, b_col)),
                  pl.BlockSpec(w.shape, lambda i: (0, 0)),
                  pl.BlockSpec((tm, d), lambda i: (i, 0)),
                  pl.BlockSpec((1, d), lambda i: (0, 0))],
        out_specs=pl.BlockSpec((tm, d), lambda i: (i, 0)),
        out_shape=jax.ShapeDtypeStruct((n, d), F32),
        compiler_params=_cparams(("parallel",)), name="out_proj")(a, b, w, x, fw.reshape(1, d))


def _neumann(a_list, c):
    eye = jnp.where(_iota((c, c), 0) == _iota((c, c), 1), 1.0, 0.0)
    ts = [eye - a for a in a_list]
    abs_ = [a.astype(BF16) for a in a_list]
    ms = [_dot(ab, ab) for ab in abs_]
    levels = int(math.log2(c)) - 1
    for k in range(levels):
        mbs = [m.astype(BF16) for m in ms]
        ts = [t + _dot(mb, t.astype(BF16)) for mb, t in zip(mbs, ts)]
        if k + 1 < levels:
            ms = [_dot(mb, mb) for mb in mbs]
    return ts


def _gdn_kernel(x_ref, z_ref, gt_ref, conv0_ref, s0_ref, cw_ref, hp_ref, nw_ref, stack_ref,
                o_ref, convo_ref, so_ref,
                xbuf, qkv_scr, g_scr, b_scr, s_scr, *, c, nch, tv, dk, unroll, per_seq):
    gi = pl.program_id(1)
    tg = c * nch
    hq = N_HEADS * dk
    cw = cw_ref[...]

    def conv(base, rows):
        return _silu((xbuf[base + 5:base + 5 + rows, :] * cw[0:1, :] + xbuf[base + 6:base + 6 + rows, :] * cw[1:2, :])
                     + (xbuf[base + 7:base + 7 + rows, :] * cw[2:3, :] + xbuf[base + 8:base + 8 + rows, :] * cw[3:4, :]))

    if per_seq:
        for s in range(nch):
            base = s * (8 + c)
            xbuf[base:base + 8, :] = conv0_ref[s]
            xbuf[base + 8:base + 8 + c, :] = x_ref[s * c:(s + 1) * c, :]
            qkv_scr[s * c:(s + 1) * c, :] = conv(base, c)
            convo_ref[s, 0:8 - (CONV_W - 1), :] = jnp.zeros((8 - (CONV_W - 1), xbuf.shape[1]), F32)
            convo_ref[s, 8 - (CONV_W - 1):8, :] = xbuf[base + 8 + tv - (CONV_W - 1):base + 8 + tv, :]
    else:
        @pl.when(gi == 0)
        def _():
            xbuf[0:8, :] = conv0_ref[0]
            s_scr[...] = s0_ref[0]

        xbuf[8:8 + tg, :] = x_ref[...]
        qkv_scr[...] = conv(0, tg)
        xbuf[5:8, :] = xbuf[8 + tg - 3:8 + tg, :]

    gt = gt_ref[...]
    hp = hp_ref[...]
    lane = _iota((tg, 128), 1)
    beta = _sigmoid(gt)
    g = -jnp.exp(hp[0:1, :]) * _softplus(gt + hp[1:2, :])
    g = jnp.where(lane >= N_HEADS, jnp.where(lane < 2 * N_HEADS, g, 0.0), 0.0)
    if tv < c:
        valid = (_iota((tg, 128), 0) % c) < tv
        g = jnp.where(valid, g, 0.0)
        beta = jnp.where(valid, beta, 0.0)
    g_scr[...] = g
    b_scr[...] = beta

    row = _iota((c, c), 0)
    col = _iota((c, c), 1)
    ige = row >= col
    igt = row > col
    tri = jnp.where(ige, 1.0, 0.0).astype(BF16)
    lane_c = _iota((c, 128), 1)
    nw = nw_ref[...]

    sels = [jnp.where(lane_c == N_HEADS + h, 1.0, 0.0).astype(BF16) for h in range(N_HEADS)]

    def prep(starts):
        items = [(ci, h) for ci in range(len(starts)) for h in range(N_HEADS)]
        gcums = [_exact_dot(tri, g_scr[pl.ds(r0, c), :]) for r0 in starts]
        bchs = [b_scr[pl.ds(r0, c), :] for r0 in starts]
        qns, kns, vhs = [], [], []
        for ci, h in items:
            r0 = starts[ci]
            qh = qkv_scr[pl.ds(r0, c), h * dk:(h + 1) * dk]
            kh = qkv_scr[pl.ds(r0, c), hq + h * dk:hq + (h + 1) * dk]
            vhs.append(qkv_scr[pl.ds(r0, c), 2 * hq + h * dk:2 * hq + (h + 1) * dk])
            qns.append(qh * lax.rsqrt(jnp.sum(qh * qh, axis=-1, keepdims=True) + EPS) * (dk ** -0.5))
            kns.append(kh * lax.rsqrt(jnp.sum(kh * kh, axis=-1, keepdims=True) + EPS))
        qkks = [_dot_nt(jnp.concatenate([qn, kn], axis=0).astype(BF16), kn.astype(BF16))
                for qn, kn in zip(qns, kns)]
        grows = [_exact_dot_nt(sels[h], gcums[ci]) for ci, h in items]
        gcols = [gcums[ci][:, N_HEADS + h:N_HEADS + h + 1] for ci, h in items]
        bcols = [bchs[ci][:, h:h + 1] for ci, h in items]
        decays = [jnp.where(ige, jnp.exp(jnp.where(ige, gcol - grow, 0.0)), 0.0)
                  for gcol, grow in zip(gcols, grows)]
        t_invs = _neumann([jnp.where(igt, qkk[c:] * decay * bcol, 0.0)
                           for qkk, decay, bcol in zip(qkks, decays, bcols)], c)
        egs = [jnp.exp(gcol) for gcol in gcols]
        sols = [_dot(t_inv.astype(BF16),
                     jnp.concatenate([vh * bcol, kn * (bcol * eg)], axis=1).astype(BF16))
                for t_inv, vh, kn, bcol, eg in zip(t_invs, vhs, kns, bcols, egs)]
        glasts = [gcums[ci][c - 1:c, N_HEADS + h:N_HEADS + h + 1] for ci, h in items]
        kws = [(kn * jnp.exp(glast - gcol)).astype(BF16) for kn, glast, gcol in zip(kns, glasts, gcols)]
        aqks = [(qkk[:c] * decay).astype(BF16) for qkk, decay in zip(qkks, decays)]
        sol_bs = [sol.astype(BF16) for sol in sols]
        ktus = [_dot_tn(kw, sol_b) for kw, sol_b in zip(kws, sol_bs)]
        aus = [_dot(aqk, sol_b) for aqk, sol_b in zip(aqks, sol_bs)]
        out = []
        for i in range(len(items)):
            lhs = jnp.concatenate([-ktus[i][:, dk:], qns[i] * egs[i] - aus[i][:, dk:]], axis=0).astype(BF16)
            out.append((lhs, ktus[i][:, :dk], aus[i][:, :dk], jnp.exp(glasts[i])))
        return [out[ci * N_HEADS:(ci + 1) * N_HEADS] for ci in range(len(starts))]

    def chunks(starts, seq_ids):
        preps = prep(starts)
        for r0, sid, heads in zip(starts, seq_ids, preps):
            s_olds = [s_scr[h] if sid is None else s0_ref[sid, h] for h in range(N_HEADS)]
            res = [_dot(heads[h][0], s_olds[h].astype(BF16)) for h in range(N_HEADS)]
            for h in range(N_HEADS):
                s_new = s_olds[h] * heads[h][3] + (heads[h][1] + res[h][:dk])
                if sid is None:
                    s_scr[h] = s_new
                else:
                    so_ref[sid, h] = s_new
            for h in range(N_HEADS):
                zz = z_ref[pl.ds(r0, c), h * dk:(h + 1) * dk]
                o = heads[h][2] + res[h][dk:]
                o_ref[pl.ds(r0, c), h * dk:(h + 1) * dk] = _rms(o, nw) * _silu(zz)

    if per_seq:
        chunks([s * c for s in range(nch)], list(range(nch)))
    else:
        def body(it, carry):
            chunks([pl.multiple_of((it * unroll + i) * c, c) for i in range(unroll)], [None] * unroll)
            return carry

        lax.fori_loop(0, nch // unroll, body, 0)

        @pl.when(gi == pl.num_programs(1) - 1)
        def _():
            convo_ref[0] = xbuf[0:8, :]
            so_ref[0] = s_scr[...]


def _gdn(qkvz, gates, conv0, s0, s_idx, conv_w, hp, nw, s_stack, out_idx, *, bn, c, nch, tv, per_seq=False):
    n = qkvz.shape[0]
    dk = s0.shape[-1]
    cc = 3 * N_HEADS * dk
    tg = c * nch
    sb = nch if per_seq else 1
    steps = 1 if per_seq else n // (bn * tg)
    kern = functools.partial(_gdn_kernel, c=c, nch=nch, tv=tv, dk=dk, unroll=math.gcd(nch, 8), per_seq=per_seq)
    return pl.pallas_call(
        kern, grid=(bn // sb, steps),
        in_specs=[pl.BlockSpec((tg, cc), lambda b, g: (b * steps + g, 0)),
                  pl.BlockSpec((tg, N_HEADS * dk), lambda b, g: (b * steps + g, 3)),
                  pl.BlockSpec((tg, 128), lambda b, g: (b * steps + g, 0)),
                  pl.BlockSpec((sb, 8, cc), lambda b, g: (b, 0, 0)),
                  pl.BlockSpec((None, sb, N_HEADS, dk, dk), lambda b, g: (s_idx, b, 0, 0, 0)),
                  pl.BlockSpec((CONV_W, cc), lambda b, g: (0, 0)),
                  pl.BlockSpec((8, 128), lambda b, g: (0, 0)),
                  pl.BlockSpec((1, dk), lambda b, g: (0, 0)),
                  pl.BlockSpec(memory_space=pl.ANY)],
        out_specs=[pl.BlockSpec((tg, N_HEADS * dk), lambda b, g: (b * steps + g, 0)),
                   pl.BlockSpec((sb, 8, cc), lambda b, g: (b, 0, 0)),
                   pl.BlockSpec((None, sb, N_HEADS, dk, dk), lambda b, g: (out_idx, b, 0, 0, 0))],
        out_shape=[jax.ShapeDtypeStruct((n, N_HEADS * dk), F32),
                   jax.ShapeDtypeStruct((bn, 8, cc), F32),
                   jax.ShapeDtypeStruct(s_stack.shape, F32)],
        input_output_aliases={8: 2},
        scratch_shapes=[pltpu.VMEM((sb * 8 + tg, cc), F32), pltpu.VMEM((tg, cc), F32),
                        pltpu.VMEM((tg, 128), F32), pltpu.VMEM((tg, 128), F32),
                        pltpu.VMEM((N_HEADS, dk, dk), F32)],
        compiler_params=_cparams(("parallel", "arbitrary")), name="gated_delta")(
            qkvz, qkvz, gates, conv0, s0, conv_w, hp, nw.reshape(1, dk), s_stack)


def _mlstm_kernel(q_ref, k_ref, v_ref, z_ref, og_ref, gt_ref, c0_ref, n0_ref, m0_ref, hp_ref, nw_ref, stack_ref,
                  o_ref, co_ref, no_ref, mo_ref,
                  li_scr, lf_scr, cext, m_scr, *, c, nch, sb, tv, dqk, dv):
    gi = pl.program_id(1)
    tg = c * nch
    lane1 = _iota((dqk, 128), 1)

    @pl.when(gi == 0)
    def _():
        for s in range(sb):
            m_scr[s * 8:(s + 1) * 8, :] = jnp.zeros((8, 128), F32)
            for h in range(N_HEADS):
                cext[s * N_HEADS + h, :, 0:dv] = c0_ref[s, h]
                cext[s * N_HEADS + h, :, dv:dv + 128] = jnp.broadcast_to(n0_ref[s, h], (dqk, 128))
                m_scr[s * 8 + h:s * 8 + h + 1, :] = jnp.broadcast_to(m0_ref[s, 0:1, h:h + 1], (1, 128))

    hp = hp_ref[...]
    lane = _iota((tg, 128), 1)
    for s in range(sb):
        gt = gt_ref[s]
        li = gt + hp[0:1, :]
        x = gt + hp[1:2, :]
        lf = jnp.minimum(x, 0.0) - jnp.log(1.0 + jnp.exp(-jnp.abs(x)))
        lf = jnp.where(lane >= N_HEADS, jnp.where(lane < 2 * N_HEADS, lf, 0.0), 0.0)
        if tv < c:
            valid = (_iota((tg, 128), 0) % c) < tv
            lf = jnp.where(valid, lf, 0.0)
            li = jnp.where(valid, li, NEG)
        li_scr[s] = li
        lf_scr[s] = lf

    row = _iota((c, c), 0)
    col = _iota((c, c), 1)
    ige = row >= col
    tri = jnp.where(ige, 1.0, 0.0).astype(BF16)
    ones_b = jnp.ones((c, 128), BF16)
    lane_c = _iota((c, 128), 1)
    ones_blk = jnp.ones((c, 128), F32)
    nw = nw_ref[...]
    nrep = dv // 128 + 1

    def wide(x):
        return jnp.concatenate([x] * nrep, axis=1)

    def prep(clist):
        items = [(ci, h) for ci in range(len(clist)) for h in range(N_HEADS)]
        bcums = [_exact_dot(tri, lf_scr[s, pl.ds(r0, c), :]) for s, r0 in clist]
        lichs = [li_scr[s, pl.ds(r0, c), :] for s, r0 in clist]
        qbs = [(q_ref[clist[ci][0], pl.ds(clist[ci][1], c), h * dqk:(h + 1) * dqk] * (dqk ** -0.5)).astype(BF16)
               for ci, h in items]
        ks = [k_ref[clist[ci][0], pl.ds(clist[ci][1], c), h * dqk:(h + 1) * dqk] for ci, h in items]
        qk_raws = [_dot_nt(qb, k.astype(BF16)) for qb, k in zip(qbs, ks)]
        bcols = [jnp.broadcast_to(bcums[ci][:, N_HEADS + h:N_HEADS + h + 1], (c, 128)) for ci, h in items]
        licols = [jnp.broadcast_to(lichs[ci][:, h:h + 1], (c, 128)) for ci, h in items]
        rowvs = [_exact_dot_nt(ones_b, jnp.where(lane_c == 0, licol - bcol, 0.0))
                 for licol, bcol in zip(licols, bcols)]
        dmats = [jnp.where(ige, bcol[:, :c] + rowv, NEG) for bcol, rowv in zip(bcols, rowvs)]
        dmaxs = [jnp.broadcast_to(jnp.max(dmat, axis=-1, keepdims=True), (c, 128)) for dmat in dmats]
        vexts = [jnp.concatenate([v_ref[clist[ci][0], pl.ds(clist[ci][1], c), h * dv:(h + 1) * dv], ones_blk],
                                 axis=1).astype(BF16) for ci, h in items]
        intras = [_dot((qk_raw * jnp.exp(dmat - dmax[:, :c])).astype(BF16), vext)
                  for qk_raw, dmat, dmax, vext in zip(qk_raws, dmats, dmaxs, vexts)]
        blasts = [bcol[c - 1:c, :] for bcol in bcols]
        mus = [dmax[c - 1:c, :] for dmax in dmaxs]
        kvs = [_dot_tn((k * jnp.exp(blast - bcol + licol - mu)).astype(BF16), vext)
               for k, blast, bcol, licol, mu, vext in zip(ks, blasts, bcols, licols, mus, vexts)]
        out = [(qbs[i], bcols[i], dmaxs[i], intras[i], blasts[i], mus[i], kvs[i]) for i in range(len(items))]
        return [out[ci * N_HEADS:(ci + 1) * N_HEADS] for ci in range(len(clist))]

    def chunks(clist):
        hs = range(N_HEADS)
        for (s, r0), heads in zip(clist, prep(clist)):
            c_olds = [cext[s * N_HEADS + h] for h in hs]
            qcs = [_dot(heads[h][0], c_olds[h].astype(BF16)) for h in hs]
            for h in hs:
                qb, bcol, dmax, intra, blast, mu, kv = heads[h]
                mprev = m_scr[s * 8 + h:s * 8 + h + 1, :]
                inter = bcol + mprev
                mt = jnp.maximum(inter, dmax)
                mnew = jnp.maximum(blast + mprev, mu)
                cext[s * N_HEADS + h] = (c_olds[h] * wide(jnp.exp(blast + mprev - mnew))
                                         + wide(jnp.exp(mu - mnew)) * kv)
                m_scr[s * 8 + h:s * 8 + h + 1, :] = mnew
                tot = wide(jnp.exp(inter - mt)) * qcs[h] + wide(jnp.exp(dmax - mt)) * intra
                den = jnp.maximum(jnp.abs(tot[:, dv:dv + 128]), jnp.exp(-mt))
                hh = tot[:, :dv] / jnp.concatenate([den] * (dv // 128), axis=1)
                og = og_ref[s, pl.ds(r0, c), h * dv:(h + 1) * dv]
                zz = z_ref[s, pl.ds(r0, c), h * dv:(h + 1) * dv]
                o_ref[s, pl.ds(r0, c), h * dv:(h + 1) * dv] = _rms(_sigmoid(og) * hh, nw) * _silu(zz)

    chunks([(s, i * c) for i in range(nch) for s in range(sb)])

    @pl.when(gi == pl.num_programs(1) - 1)
    def _():
        for s in range(sb):
            for h in range(N_HEADS):
                co_ref[s, h] = cext[s * N_HEADS + h, :, 0:dv]
                no_ref[s, h] = cext[s * N_HEADS + h, :, dv:dv + 1]
            mrows = m_scr[s * 8:(s + 1) * 8, :]
            diag = jnp.where(_iota((8, 128), 0) == _iota((8, 128), 1), mrows, 0.0)
            mo_ref[s] = jnp.broadcast_to(jnp.sum(diag, axis=0, keepdims=True), (8, 128))


def _mlstm(q, k, v, z, og, gates, c0, c_idx, n0, m0, hp, nw, c_stack, out_idx, *, bn, c, nch, sb, tv):
    n = q.shape[0]
    dqk, dv = c0.shape[-2], c0.shape[-1]
    tg = c * nch
    rows = n // bn
    steps = rows // tg
    hq, hv = N_HEADS * dqk, N_HEADS * dv
    row = lambda b, g: (b, g, 0)
    q, k, v, z, og, gates = (a.reshape(bn, rows, a.shape[1]) for a in (q, k, v, z, og, gates))
    kern = functools.partial(_mlstm_kernel, c=c, nch=nch, sb=sb, tv=tv, dqk=dqk, dv=dv)
    out = pl.pallas_call(
        kern, grid=(bn // sb, steps),
        in_specs=[pl.BlockSpec((sb, tg, hq), row), pl.BlockSpec((sb, tg, hq), row),
                  pl.BlockSpec((sb, tg, hv), row), pl.BlockSpec((sb, tg, hv), row),
                  pl.BlockSpec((sb, tg, hv), row), pl.BlockSpec((sb, tg, 128), row),
                  pl.BlockSpec((None, sb, N_HEADS, dqk, dv), lambda b, g: (c_idx, b, 0, 0, 0)),
                  pl.BlockSpec((sb, N_HEADS, dqk, 1), lambda b, g: (b, 0, 0, 0)),
                  pl.BlockSpec((sb, 8, 128), lambda b, g: (b, 0, 0)),
                  pl.BlockSpec((8, 128), lambda b, g: (0, 0)),
                  pl.BlockSpec((1, dv), lambda b, g: (0, 0)),
                  pl.BlockSpec(memory_space=pl.ANY)],
        out_specs=[pl.BlockSpec((sb, tg, hv), row),
                   pl.BlockSpec((None, sb, N_HEADS, dqk, dv), lambda b, g: (out_idx, b, 0, 0, 0)),
                   pl.BlockSpec((sb, N_HEADS, dqk, 1), lambda b, g: (b, 0, 0, 0)),
                   pl.BlockSpec((sb, 8, 128), lambda b, g: (b, 0, 0))],
        out_shape=[jax.ShapeDtypeStruct((bn, rows, hv), F32),
                   jax.ShapeDtypeStruct(c_stack.shape, F32),
                   jax.ShapeDtypeStruct((bn, N_HEADS, dqk, 1), F32),
                   jax.ShapeDtypeStruct((bn, 8, 128), F32)],
        input_output_aliases={11: 1},
        scratch_shapes=[pltpu.VMEM((sb, tg, 128), F32), pltpu.VMEM((sb, tg, 128), F32),
                        pltpu.VMEM((sb * N_HEADS, dqk, dv + 128), F32), pltpu.VMEM((sb * 8, 128), F32)],
        compiler_params=_cparams(("parallel", "arbitrary")), name="mlstm")(
            q, k, v, z, og, gates, c0, n0, m0, hp, nw.reshape(1, dv), c_stack)
    return (out[0].reshape(n, hv),) + tuple(out[1:])


def _lambda(lq, lam_init):
    a = jnp.sum(lq[0:1, :] * lq[1:2, :], axis=-1, keepdims=True)
    b = jnp.sum(lq[2:3, :] * lq[3:4, :], axis=-1, keepdims=True)
    return jnp.exp(a) - jnp.exp(b) + lam_init


def _attn_kernel(wq_ref, k_ref, vt_ref, z_ref, lq_ref, nw_ref, o_ref, acc_scr, *, t, hpb, lam_init):
    qi = pl.program_id(2)
    dh = k_ref.shape[1] // hpb
    nl = 4 * t
    acc_scr[...] = jnp.zeros(acc_scr.shape, F32)

    def step(jp, carry, masked):
        rows = [pl.multiple_of((2 * jp + i) * t, t) for i in range(2)]
        wqs = [jnp.concatenate([wq_ref[0, a], wq_ref[1, a]], axis=1) for a in range(hpb)]
        scores = [[_dot(k_ref[pl.ds(rows[i], t), a * dh:(a + 1) * dh], wqs[a]) for a in range(hpb)]
                  for i in range(2)]
        out = list(carry)
        for i in range(2):
            for a in range(hpb):
                m_prev, l_prev = out[a]
                s = scores[i][a]
                if masked:
                    lane = _iota((t, nl), 1)
                    qpos = (lane // (2 * t)) * t + lane % t
                    s = jnp.where(_iota((t, nl), 0) + i * t <= qpos, s, NEG)
                m_new = jnp.maximum(m_prev, jnp.max(s, axis=0, keepdims=True))
                alpha = jnp.exp2(m_prev - m_new)
                p = jnp.exp2(s - m_new)
                l_new = alpha * l_prev + jnp.sum(p, axis=0, keepdims=True)
                acc_scr[a] = alpha * acc_scr[a] + _dot(vt_ref[2 * jp + i, a * dh:(a + 1) * dh, :], p.astype(BF16))
                out[a] = (m_new, l_new)
        return tuple(out)

    init = tuple((jnp.full((1, nl), NEG, F32), jnp.zeros((1, nl), F32)) for _ in range(hpb))
    carry = lax.fori_loop(0, qi, lambda j, c: step(j, c, False), init)
    carry = step(qi, carry, True)

    lam = _lambda(lq_ref[...], lam_init)
    nw = nw_ref[...]
    for a in range(hpb):
        acc = acc_scr[a] * (1.0 / carry[a][1])
        for qb in range(2):
            b0 = 2 * t * qb
            o = (acc[:, b0:b0 + t] - lam * acc[:, b0 + t:b0 + 2 * t]).T
            zz = z_ref[qb * t:(qb + 1) * t, a * dh:(a + 1) * dh]
            o_ref[qb * t:(qb + 1) * t, a * dh:(a + 1) * dh] = _rms(o, nw) * (1.0 - lam_init) * _silu(zz)


def _attn_prompt(wq, k16, vt, z, lq, nw, *, bn, seq, lam_init):
    n = k16.shape[0]
    t = vt.shape[2]
    dh = wq.shape[2]
    hpb = 4
    nq = seq // t
    nq2 = nq // 2
    kern = functools.partial(_attn_kernel, t=t, hpb=hpb, lam_init=lam_init)
    return pl.pallas_call(
        kern, grid=(bn, N_HEADS // hpb, nq2),
        in_specs=[pl.BlockSpec((2, hpb, dh, 2 * t), lambda b, h, i: (b * nq2 + i, h, 0, 0)),
                  pl.BlockSpec((seq, hpb * dh), lambda b, h, i: (b, h), pipeline_mode=pl.Buffered(1)),
                  pl.BlockSpec((nq, hpb * dh, t), lambda b, h, i: (b, h, 0), pipeline_mode=pl.Buffered(1)),
                  pl.BlockSpec((2 * t, hpb * dh), lambda b, h, i: (b * nq2 + i, h)),
                  pl.BlockSpec(lq.shape, lambda b, h, i: (0, 0)),
                  pl.BlockSpec((1, dh), lambda b, h, i: (0, 0))],
        out_specs=pl.BlockSpec((2 * t, hpb * dh), lambda b, h, i: (b * nq2 + i, h)),
        out_shape=jax.ShapeDtypeStruct((n, N_HEADS * dh), F32),
        scratch_shapes=[pltpu.VMEM((hpb, dh, 4 * t), F32)],
        compiler_params=_cparams(("parallel", "parallel", "arbitrary")), name="diff_attn_prompt")(
            wq, k16, vt, z, lq, nw.reshape(1, dh))


def _attn_s_kernel(pt_ref, q_ref, kn_ref, vn_ref, z_ref, lq_ref, nw_ref, *rest, pg, tv, lam_init):
    k_pages = rest[:pg]
    v_pages = rest[pg:2 * pg]
    o_ref, qbd, bias, m_scr, l_scr, acc_scr = rest[2 * pg:]
    j = pl.program_id(1)
    tp, hd = q_ref.shape
    dh = hd // N_HEADS
    dqk = dh // 2
    prow = k_pages[0].shape[0]
    rph = 2 * tp

    @pl.when(j == 0)
    def _():
        q = q_ref[...] * (dqk ** -0.5)
        lane = _iota((tp, dh), 1)
        for h in range(N_HEADS):
            qh = q[:, h * dh:(h + 1) * dh]
            qbd[h * rph:h * rph + tp, :] = jnp.where(lane < dqk, qh, 0.0)
            qbd[h * rph + tp:(h + 1) * rph, :] = jnp.where(lane >= dqk, qh, 0.0)
        rhead = _iota(bias.shape, 0) // rph
        chead = _iota(bias.shape, 1) % N_HEADS
        bias[...] = jnp.where(rhead == chead, 0.0, NEG)
        m_scr[...] = jnp.full(m_scr.shape, NEG, F32)
        l_scr[...] = jnp.zeros(l_scr.shape, F32)
        acc_scr[...] = jnp.zeros(acc_scr.shape, F32)

    def update(s, pv):
        m_prev = m_scr[...]
        m_new = jnp.maximum(m_prev, jnp.max(s, axis=-1, keepdims=True))
        alpha = jnp.exp(m_prev - m_new)
        p = jnp.exp(s - m_new[:, 0:1])
        l_scr[...] = alpha * l_scr[...] + jnp.sum(p, axis=-1, keepdims=True)
        acc_scr[...] = alpha * acc_scr[...] + pv(p)
        m_scr[...] = m_new

    qb = qbd[...].astype(BF16)
    bs_ = bias[...]
    s = jnp.concatenate([_dot_nt(qb, kp[...].astype(BF16)) + bs_ for kp in k_pages], axis=1)

    def pv_pages(p):
        pb = p.astype(BF16)
        acc = _dot(pb[:, 0:prow], v_pages[0][...].astype(BF16))
        for i in range(1, pg):
            acc = acc + _dot(pb[:, i * prow:(i + 1) * prow], v_pages[i][...].astype(BF16))
        return acc

    update(s, pv_pages)

    @pl.when(j == pl.num_programs(1) - 1)
    def _():
        qf = qbd[...]
        kn = kn_ref[...]
        vn = vn_ref[...]
        sn = jnp.concatenate([_dot_nt(qf[h * rph:(h + 1) * rph], kn[:, h * dh:(h + 1) * dh])
                              for h in range(N_HEADS)], axis=0)
        tq = _iota(sn.shape, 0) % tp
        tk = _iota(sn.shape, 1)
        sn = jnp.where(tk <= tq, jnp.where(tk < tv, sn, NEG), NEG)
        update(sn, lambda p: jnp.concatenate(
            [_dot(p[h * rph:(h + 1) * rph], vn[:, h * dh:(h + 1) * dh]) for h in range(N_HEADS)], axis=0))
        acc = acc_scr[...] / l_scr[...]
        lam = _lambda(lq_ref[...], lam_init)
        nw = nw_ref[...]
        normed = jnp.concatenate(
            [_rms(acc[h * rph:h * rph + tp] - lam * acc[h * rph + tp:(h + 1) * rph], nw) for h in range(N_HEADS)],
            axis=1)
        o_ref[...] = normed * (1.0 - lam_init) * _silu(z_ref[...])


def _attn_sample(page_table, q, kn, vn, z, lq, nw, cache_k, cache_v, *, e, tv, lam_init):
    bs, npages = page_table.shape
    n, hd = q.shape
    tp = n // bs
    dh = hd // N_HEADS
    pg = math.gcd(npages, PAGES_PER_STEP)
    prow = cache_k.shape[2]
    steps = npages // pg
    rows = 2 * N_HEADS * tp
    tok = lambda b, j, pt: (b, 0)
    page_specs = [pl.BlockSpec((None, None, prow, dh),
                               lambda b, j, pt, i=i: (e, pt[b, j * pg + i], 0, 0)) for i in range(pg)]
    kern = functools.partial(_attn_s_kernel, pg=pg, tv=tv, lam_init=lam_init)
    grid_spec = pltpu.PrefetchScalarGridSpec(
        num_scalar_prefetch=1, grid=(bs, steps),
        in_specs=[pl.BlockSpec((tp, hd), tok), pl.BlockSpec((tp, hd), tok), pl.BlockSpec((tp, hd), tok),
                  pl.BlockSpec((tp, hd), tok),
                  pl.BlockSpec(lq.shape, lambda b, j, pt: (0, 0)),
                  pl.BlockSpec((1, dh), lambda b, j, pt: (0, 0))] + page_specs + page_specs,
        out_specs=pl.BlockSpec((tp, hd), tok),
        scratch_shapes=[pltpu.VMEM((rows, dh), F32), pltpu.VMEM((rows, prow), F32), pltpu.VMEM((rows, 128), F32),
                        pltpu.VMEM((rows, 128), F32), pltpu.VMEM((rows, dh), F32)])
    return pl.pallas_call(
        kern, grid_spec=grid_spec, out_shape=jax.ShapeDtypeStruct((n, hd), F32),
        compiler_params=_cparams(("parallel", "arbitrary")), name="diff_attn_sample")(
            page_table, q, kn, vn, z, lq, nw.reshape(1, dh), *([cache_k] * pg), *([cache_v] * pg))


def _rope_tables(pos, dqk):
    rope_dim = dqk // 4
    half = rope_dim // 2
    inv = jnp.power(ROPE_THETA, -jnp.arange(half, dtype=F32) / half)
    ang = pos.astype(F32)[:, None] * inv[None, :]
    lane = jnp.arange(128) % dqk
    idx = lane % half
    cos = jnp.where(lane < rope_dim, jnp.cos(ang)[:, idx], 1.0)
    sin = jnp.sin(ang)[:, idx]
    sa = jnp.where((lane >= half) & (lane < rope_dim), sin, 0.0)
    sb = jnp.where(lane < half, -sin, 0.0)
    return cos.astype(F32), sa.astype(F32), sb.astype(F32)


def _lane_row(vals, offset):
    out = jnp.zeros((128,), F32)
    return out.at[offset:offset + vals.shape[0]].set(vals.astype(F32))


def kernel(x_prompt, x_sample, cache_k, cache_v, page_table, state_gdn_conv, state_gdn_s, state_mlstm_c,
           state_mlstm_n, state_mlstm_m, norm_w, final_norm_w, w_in_even, w_out_even, conv_w, a_log, dt_bias,
           gdn_norm_w, lambda_qk, diff_norm_w, w_in_odd, w_out_odd, b_i, b_f, mlstm_norm_w):
    bp, seq, d = x_prompt.shape
    bs, ts, _ = x_sample.shape
    depth = norm_w.shape[0]
    tp = SAMPLE_PAD
    dk_a = state_gdn_s.shape[-1]
    conv_ch = state_gdn_conv.shape[-1]
    w_a = N_HEADS * dk_a
    dh_b = cache_v.shape[-1]
    dqk_b = dh_b // 2
    w_b = N_HEADS * dh_b
    dqk_c, dv_c = state_mlstm_c.shape[-2], state_mlstm_c.shape[-1]
    hq_c, w_c = N_HEADS * dqk_c, N_HEADS * dv_c
    past = page_table.shape[1] * cache_k.shape[2]
    n_pool, page = cache_k.shape[1], cache_k.shape[2]

    xp = x_prompt.reshape(bp * seq, d)
    xs = jnp.pad(x_sample, ((0, 0), (0, tp - ts), (0, 0))).reshape(bs * tp, d)
    tm_p = math.gcd(seq, 256)
    tm_s = math.gcd(bs * tp, 256)
    nch_p = math.gcd(seq // CHUNK, 8)
    seq_s = math.gcd(bs, 8)
    seq_c = math.gcd(bs, 8)

    rope_p = _rope_tables(jnp.arange(seq), dqk_b)
    rope_s = _rope_tables(past + (jnp.arange(tm_s) % tp), dqk_b)
    half = dqk_b // 8
    ang_t = (jnp.power(ROPE_THETA, -jnp.arange(half, dtype=F32) / half)[:, None]
             * jnp.arange(seq).astype(F32)[None, :])
    cos_t, sin_t = jnp.cos(ang_t), jnp.sin(ang_t)

    even_w = conv_ch + w_a
    seg_even = [(0, even_w), (even_w, w_b), (even_w + w_b, w_b), (even_w + 2 * w_b, w_b),
                (even_w + 3 * w_b, w_b), (even_w + 4 * w_b, 128)]
    seg_even_p = [seg_even[0]] + seg_even[2:]
    rope_even = tuple(range(even_w // 128, (even_w + 2 * w_b) // 128))
    odd_w = 2 * hq_c + 3 * w_c
    seg_odd = [(0, hq_c), (hq_c, hq_c), (2 * hq_c, w_c), (2 * hq_c + w_c, w_c), (2 * hq_c + 2 * w_c, w_c),
               (odd_w, 128)]

    ck = cache_k.reshape(cache_k.shape[0], n_pool, page * N_HEADS, dh_b)
    cv = cache_v.reshape(cache_v.shape[0], n_pool, page * N_HEADS, dh_b)

    n_even = (depth + 1) // 2
    krows_p = jnp.zeros((n_even, bp * seq * N_HEADS, dh_b), F32)
    vrows_p = jnp.zeros((n_even, bp * seq * N_HEADS, dh_b), F32)
    k_rows_s, v_rows_s = [], []
    n_odd = depth // 2
    s_p = jnp.zeros((n_even, bp, N_HEADS, dk_a, dk_a), F32)
    s_s = jnp.zeros((n_even, bs, N_HEADS, dk_a, dk_a), F32)
    c_p = jnp.zeros((n_odd, bp, N_HEADS, dqk_c, dv_c), F32)
    c_s = jnp.zeros((n_odd, bs, N_HEADS, dqk_c, dv_c), F32)
    conv_p, conv_s = [], []
    n_p, m_p, n_s, m_s = [], [], [], []

    pend_p = pend_s = None
    for layer in range(depth):
        if layer % 2 == 0:
            e = layer // 2
            lam_init = 0.8 - 0.6 * math.exp(-0.3 * layer)
            w = w_in_even[e]
            g0 = even_w
            w_r = jnp.concatenate([w[:, :g0], w[:, g0 + 2 * N_HEADS:], w[:, g0:g0 + 2 * N_HEADS],
                                   jnp.zeros((d, 128 - 2 * N_HEADS), F32)], axis=1).astype(BF16)
            w_o = w_out_even[e].astype(BF16)
            hp = jnp.zeros((8, 128), F32).at[0].set(_lane_row(a_log[e], N_HEADS)).at[1].set(
                _lane_row(dt_bias[e], N_HEADS))
            lq = lambda_qk[e]

            wq_t = w[:, g0 + 2 * N_HEADS:g0 + 2 * N_HEADS + w_b].T.astype(BF16)
            wv_t = w[:, g0 + 2 * N_HEADS + 2 * w_b:g0 + 2 * N_HEADS + 3 * w_b].T.astype(BF16)
            outs = _proj(xp, norm_w[layer], w_r, seg_even_p, (rope_even, rope_p), tm_p, k16_seg=1,
                         feat=(wq_t, wv_t, cos_t, sin_t), rows=((1, 2), e, (krows_p, vrows_p)), pre=pend_p)
            if pend_p is not None:
                xp = outs.pop()
            az, krows_p, vrows_p, zb, gt, k16, wq, vt = outs
            oa, cst, s_p = _gdn(az, gt, jnp.zeros((bp, 8, conv_ch), F32),
                                jnp.zeros((1, bp, N_HEADS, dk_a, dk_a), F32), 0,
                                conv_w[e], hp, gdn_norm_w[e], s_p, e, bn=bp, c=CHUNK, nch=nch_p, tv=CHUNK)
            ob = _attn_prompt(wq, k16, vt, zb, lq, diff_norm_w[e], bn=bp, seq=seq, lam_init=lam_init)
            pend_p = (oa, ob, w_a, 0, w_o)
            conv_p.append(cst[:, 8 - (CONV_W - 1):])

            outs = _proj(xs, norm_w[layer], w_r, seg_even, (rope_even, rope_s), tm_s, pre=pend_s)
            if pend_s is not None:
                xs = outs.pop()
            az, qb, kb, vb, zb, gt = outs
            conv0 = jnp.pad(state_gdn_conv[e], ((0, 0), (8 - (CONV_W - 1), 0), (0, 0)))
            oa, cst, s_s = _gdn(az, gt, conv0, state_gdn_s, e, conv_w[e], hp, gdn_norm_w[e], s_s, e,
                                bn=bs, c=tp, nch=seq_s, tv=ts, per_seq=True)
            ob = _attn_sample(page_table, qb, kb, vb, zb, lq, diff_norm_w[e], ck, cv, e=e, tv=ts,
                              lam_init=lam_init)
            pend_s = (oa, ob, w_a, 0, w_o)
            k_rows_s.append(kb.reshape(bs, tp, N_HEADS, dh_b)[:, :ts])
            v_rows_s.append(vb.reshape(bs, tp, N_HEADS, dh_b)[:, :ts])
            conv_s.append(cst[:, 8 - (CONV_W - 1):])
        else:
            o = layer // 2
            w_r = jnp.concatenate([w_in_odd[o], jnp.zeros((d, 128 - 2 * N_HEADS), F32)], axis=1).astype(BF16)
            w_o = w_out_odd[o].astype(BF16)
            hp = jnp.zeros((8, 128), F32).at[0].set(_lane_row(b_i[o], 0)).at[1].set(_lane_row(b_f[o], N_HEADS))

            outs = _proj(xp, norm_w[layer], w_r, seg_odd, None, tm_p, pre=pend_p)
            if pend_p is not None:
                xp = outs.pop()
            q, k, v, z, og, gt = outs
            hm, c_p, nf, mf = _mlstm(q, k, v, z, og, gt, jnp.zeros((1, bp, N_HEADS, dqk_c, dv_c), F32), 0,
                                     jnp.zeros((bp, N_HEADS, dqk_c, 1), F32), jnp.zeros((bp, 8, 128), F32),
                                     hp, mlstm_norm_w[o], c_p, o, bn=bp, c=CHUNK, nch=math.gcd(seq // CHUNK, 8),
                                     sb=1, tv=CHUNK)
            pend_p = (hm, hm, w_c // 2, 1, w_o)
            n_p.append(nf[..., 0])
            m_p.append(mf[:, 0, :N_HEADS])

            outs = _proj(xs, norm_w[layer], w_r, seg_odd, None, tm_s, pre=pend_s)
            if pend_s is not None:
                xs = outs.pop()
            q, k, v, z, og, gt = outs
            m0 = jnp.zeros((bs, 8, 128), F32).at[:, 0, :N_HEADS].set(state_mlstm_m[o])
            hm, c_s, nf, mf = _mlstm(q, k, v, z, og, gt, state_mlstm_c, o, state_mlstm_n[o][..., None], m0,
                                     hp, mlstm_norm_w[o], c_s, o, bn=bs, c=tp, nch=1, sb=seq_c, tv=ts)
            pend_s = (hm, hm, w_c // 2, 1, w_o)
            n_s.append(nf[..., 0])
            m_s.append(mf[:, 0, :N_HEADS])

    xp = _out_proj(*pend_p, xp, final_norm_w, True, tm_p)
    xs = _out_proj(*pend_s, xs, final_norm_w, True, tm_s)
    y_prompt = xp.reshape(bp, seq, d)
    y_sample = xs.reshape(bs, tp, d)[:, :ts]
    st = jnp.stack
    kv_shape = (n_even, bp, seq, N_HEADS, dh_b)
    return (y_prompt, y_sample, krows_p.reshape(kv_shape), vrows_p.reshape(kv_shape), st(k_rows_s), st(v_rows_s),
            st(conv_p), s_p, st(conv_s), s_s,
            c_p, st(n_p), st(m_p), c_s, st(n_s), st(m_s))
```

```python
import functools
import math

import jax
import jax.numpy as jnp
from jax import lax
from jax.experimental import pallas as pl
from jax.experimental.pallas import tpu as pltpu

F32 = jnp.float32
BF16 = jnp.bfloat16

EPS = 1e-6
NEG = -1e30
N_HEADS = 4
CONV_W = 4
ROPE_THETA = 500000.0
CHUNK = 64
SAMPLE_PAD = 8
PAGES_PER_STEP = 16
V7X_VMEM_LIMIT = 56 * 1024 * 1024


def _cparams(sem):
    return pltpu.CompilerParams(dimension_semantics=sem, vmem_limit_bytes=V7X_VMEM_LIMIT)


def _dot(a, b):
    return jnp.dot(a, b, preferred_element_type=F32)


def _dot_nt(a, b):
    return lax.dot_general(a, b, (((1,), (1,)), ((), ())), preferred_element_type=F32)


def _dot_tn(a, b):
    return lax.dot_general(a, b, (((0,), (0,)), ((), ())), preferred_element_type=F32)


def _split3(x):
    hi = x.astype(BF16)
    r = x - hi.astype(F32)
    mid = r.astype(BF16)
    lo = (r - mid.astype(F32)).astype(BF16)
    return hi, mid, lo


def _exact_dot(a_bf16, x):
    hi, mid, lo = _split3(x)
    return _dot(a_bf16, hi) + (_dot(a_bf16, mid) + _dot(a_bf16, lo))


def _exact_dot_nt(a_bf16, x):
    hi, mid, lo = _split3(x)
    return _dot_nt(a_bf16, hi) + (_dot_nt(a_bf16, mid) + _dot_nt(a_bf16, lo))


def _sigmoid(x):
    return 1.0 / (1.0 + jnp.exp(-x))


def _silu(x):
    return x * _sigmoid(x)


def _softplus(x):
    return jnp.maximum(x, 0.0) + jnp.log(1.0 + jnp.exp(-jnp.abs(x)))


def _rms(x, w):
    return x * lax.rsqrt(jnp.mean(x * x, axis=-1, keepdims=True) + EPS) * w


def _iota(shape, dim):
    return lax.broadcasted_iota(jnp.int32, shape, dim)


def _proj_kernel(*refs, segs, rope_blocks, k16_seg, feat_major, row_segs, fuse_out):
    n_out = len(segs)
    pos = 3
    x_ref, nw_ref, w_ref = refs[:3]
    if rope_blocks:
        cos_ref, sa_ref, sb_ref = refs[pos:pos + 3]
        pos += 3
    if feat_major:
        wqt_ref, wvt_ref, cost_ref, sint_ref = refs[pos:pos + 4]
        pos += 4
    if fuse_out:
        a_ref, b_ref, wo_ref = refs[pos:pos + 3]
        pos += 3
    pos += len(row_segs)
    outs = refs[pos:pos + n_out]
    pos += n_out
    x = x_ref[...]
    if fuse_out:
        ka = a_ref.shape[1]
        x = x + (_dot(a_ref[...].astype(BF16), wo_ref[0:ka, :]) + _dot(b_ref[...].astype(BF16), wo_ref[ka:, :]))
        refs[-1][...] = x
    h = _rms(x, nw_ref[...]).astype(BF16)
    for si, ((c0, width), o_ref) in enumerate(zip(segs, outs)):
        for s0 in range(0, width, 512):
            sw = min(512, width - s0)
            acc = _dot(h, w_ref[:, c0 + s0:c0 + s0 + sw])
            if rope_blocks and (c0 + s0) // 128 in rope_blocks:
                cos = cos_ref[...]
                sa = sa_ref[...]
                sb = sb_ref[...]
                parts = []
                for t0 in range(0, sw, 128):
                    a = acc[:, t0:t0 + 128]
                    parts.append(a * cos + pltpu.roll(a, 8, 1) * sa + pltpu.roll(a, 120, 1) * sb)
                acc = jnp.concatenate(parts, axis=1)
            if si in row_segs:
                dh = sw // N_HEADS
                for hd in range(N_HEADS):
                    o_ref[pl.ds(hd, acc.shape[0], stride=N_HEADS), :] = acc[:, hd * dh:(hd + 1) * dh]
            else:
                o_ref[:, s0:s0 + sw] = acc
            if si == k16_seg:
                refs[pos][:, s0:s0 + sw] = acc.astype(BF16)
    if feat_major:
        wq_ref, vt_ref = refs[pos + 1], refs[pos + 2]
        tm = x.shape[0]
        vt_ref[0] = _dot_nt(wvt_ref[...], h).astype(BF16)
        qt = _dot_nt(wqt_ref[...], h)
        dqk = qt.shape[0] // (2 * N_HEADS)
        half = dqk // 8
        cos = cost_ref[...]
        sin = sint_ref[...]
        scale = (dqk ** -0.5) * math.log2(math.e)
        zero = jnp.zeros((dqk, tm), F32)
        for hd in range(N_HEADS):
            maps = []
            for m in range(2):
                r0 = (2 * hd + m) * dqk
                x1 = qt[r0:r0 + half]
                x2 = qt[r0 + half:r0 + 2 * half]
                maps.append(jnp.concatenate([x1 * cos - x2 * sin, x2 * cos + x1 * sin,
                                             qt[r0 + 2 * half:r0 + dqk]], axis=0) * scale)
            top = jnp.concatenate([maps[0], zero], axis=1)
            bot = jnp.concatenate([zero, maps[1]], axis=1)
            wq_ref[0, hd] = jnp.concatenate([top, bot], axis=0).astype(BF16)


def _proj(x, nw, w, segs, rope, tm, k16_seg=None, feat=None, rows=None, pre=None):
    n, d = x.shape
    ncols = w.shape[1]
    grid = (n // tm,)
    in_specs = [pl.BlockSpec((tm, d), lambda i: (i, 0)),
                pl.BlockSpec((1, d), lambda i: (0, 0)),
                pl.BlockSpec((d, ncols), lambda i: (0, 0))]
    args = [x, nw.reshape(1, d), w]
    rope_blocks = ()
    if rope is not None:
        rope_blocks, tables = rope
        nt = tables[0].shape[0] // tm
        for t in tables:
            in_specs.append(pl.BlockSpec((tm, 128), lambda i, nt=nt: (i % nt, 0)))
            args.append(t)
    out_shape = [jax.ShapeDtypeStruct((n, width), F32) for _, width in segs]
    out_specs = [pl.BlockSpec((tm, width), lambda i: (i, 0)) for _, width in segs]
    if feat is not None:
        wqt, wvt, cost, sint = feat
        ntt = cost.shape[1] // tm
        in_specs += [pl.BlockSpec(wqt.shape, lambda i: (0, 0)), pl.BlockSpec(wvt.shape, lambda i: (0, 0)),
                     pl.BlockSpec((cost.shape[0], tm), lambda i, ntt=ntt: (0, i % ntt)),
                     pl.BlockSpec((sint.shape[0], tm), lambda i, ntt=ntt: (0, i % ntt))]
        args += [wqt, wvt, cost, sint]
    if pre is not None:
        a, b, ka, b_col, w_out = pre
        kb = w_out.shape[0] - ka
        in_specs += [pl.BlockSpec((tm, ka), lambda i: (i, 0)), pl.BlockSpec((tm, kb), lambda i: (i, b_col)),
                     pl.BlockSpec(w_out.shape, lambda i: (0, 0))]
        args += [a, b, w_out]
    row_segs, aliases = (), {}
    if rows is not None:
        row_segs, layer, stacked = rows
        for si, arr in zip(row_segs, stacked):
            aliases[len(args)] = si
            in_specs.append(pl.BlockSpec(memory_space=pl.ANY))
            args.append(arr)
            out_shape[si] = jax.ShapeDtypeStruct(arr.shape, F32)
            out_specs[si] = pl.BlockSpec((None, tm * N_HEADS, arr.shape[2]), lambda i, layer=layer: (layer, i, 0))
    if k16_seg is not None:
        kw = segs[k16_seg][1]
        out_shape.append(jax.ShapeDtypeStruct((n, kw), BF16))
        out_specs.append(pl.BlockSpec((tm, kw), lambda i: (i, 0)))
    if feat is not None:
        hq, hv = wqt.shape[0], wvt.shape[0]
        dh = hq // N_HEADS
        out_shape += [jax.ShapeDtypeStruct((n // tm, N_HEADS, dh, 2 * tm), BF16),
                      jax.ShapeDtypeStruct((n // tm, hv, tm), BF16)]
        out_specs += [pl.BlockSpec((1, N_HEADS, dh, 2 * tm), lambda i: (i, 0, 0, 0)),
                      pl.BlockSpec((1, hv, tm), lambda i: (i, 0, 0))]
    if pre is not None:
        out_shape.append(jax.ShapeDtypeStruct((n, d), F32))
        out_specs.append(pl.BlockSpec((tm, d), lambda i: (i, 0)))
    return list(pl.pallas_call(
        functools.partial(_proj_kernel, segs=tuple(segs), rope_blocks=tuple(rope_blocks), k16_seg=k16_seg,
                          feat_major=feat is not None, row_segs=tuple(row_segs), fuse_out=pre is not None),
        grid=grid, in_specs=in_specs, out_specs=out_specs, out_shape=out_shape, input_output_aliases=aliases,
        compiler_params=_cparams(("parallel",)), name="norm_proj")(*args))


def _out_kernel(a_ref, b_ref, w_ref, x_ref, fw_ref, o_ref, *, final):
    ka = a_ref.shape[1]
    y = _dot(a_ref[...].astype(BF16), w_ref[0:ka, :]) + _dot(b_ref[...].astype(BF16), w_ref[ka:, :])
    xn = x_ref[...] + y
    if final:
        xn = _rms(xn, fw_ref[...])
    o_ref[...] = xn


def _out_proj(a, b, ka, b_col, w, x, fw, final, tm):
    n, d = x.shape
    kb = w.shape[0] - ka
    return pl.pallas_call(
        functools.partial(_out_kernel, final=final),
        grid=(n // tm,),
        in_specs=[pl.BlockSpec((tm, ka), lambda i: (i, 0)),
                  pl.BlockSpec((tm, kb), lambda i: (i, b_col)),
                  pl.BlockSpec(w.shape, lambda i: (0, 0)),
                  pl.BlockSpec((tm, d), lambda i: (i, 0)),
                  pl.BlockSpec((1, d), lambda i: (0, 0))],
        out_specs=pl.BlockSpec((tm, d), lambda i: (i, 0)),
        out_shape=jax.ShapeDtypeStruct((n, d), F32),
        compiler_params=_cparams(("parallel",)), name="out_proj")(a, b, w, x, fw.reshape(1, d))


def _neumann(a_list, c):
    eye = jnp.where(_iota((c, c), 0) == _iota((c, c), 1), 1.0, 0.0)
    ts = [eye - a for a in a_list]
    abs_ = [a.astype(BF16) for a in a_list]
    ms = [_dot(ab, ab) for ab in abs_]
    levels = int(math.log2(c)) - 1
    for k in range(levels):
        mbs = [m.astype(BF16) for m in ms]
        ts = [t + _dot(mb, t.astype(BF16)) for mb, t in zip(mbs, ts)]
        if k + 1 < levels:
            ms = [_dot(mb, mb) for mb in mbs]
    return ts


def _gdn_kernel(x_ref, z_ref, gt_ref, conv0_ref, s0_ref, cw_ref, hp_ref, nw_ref, stack_ref,
                o_ref, convo_ref, so_ref,
                xbuf, qkv_scr, g_scr, b_scr, s_scr, *, c, nch, tv, dk, unroll, per_seq):
    gi = pl.program_id(1)
    tg = c * nch
    hq = N_HEADS * dk
    cw = cw_ref[...]

    def conv(base, rows):
        return _silu((xbuf[base + 5:base + 5 + rows, :] * cw[0:1, :] + xbuf[base + 6:base + 6 + rows, :] * cw[1:2, :])
                     + (xbuf[base + 7:base + 7 + rows, :] * cw[2:3, :] + xbuf[base + 8:base + 8 + rows, :] * cw[3:4, :]))

    if per_seq:
        for s in range(nch):
            base = s * (8 + c)
            xbuf[base:base + 8, :] = conv0_ref[s]
            xbuf[base + 8:base + 8 + c, :] = x_ref[s * c:(s + 1) * c, :]
            qkv_scr[s * c:(s + 1) * c, :] = conv(base, c)
            convo_ref[s, 0:8 - (CONV_W - 1), :] = jnp.zeros((8 - (CONV_W - 1), xbuf.shape[1]), F32)
            convo_ref[s, 8 - (CONV_W - 1):8, :] = xbuf[base + 8 + tv - (CONV_W - 1):base + 8 + tv, :]
    else:
        @pl.when(gi == 0)
        def _():
            xbuf[0:8, :] = conv0_ref[0]
            s_scr[...] = s0_ref[0]

        xbuf[8:8 + tg, :] = x_ref[...]
        qkv_scr[...] = conv(0, tg)
        xbuf[5:8, :] = xbuf[8 + tg - 3:8 + tg, :]

    gt = gt_ref[...]
    hp = hp_ref[...]
    lane = _iota((tg, 128), 1)
    beta = _sigmoid(gt)
    g = -jnp.exp(hp[0:1, :]) * _softplus(gt + hp[1:2, :])
    g = jnp.where(lane >= N_HEADS, jnp.where(lane < 2 * N_HEADS, g, 0.0), 0.0)
    if tv < c:
        valid = (_iota((tg, 128), 0) % c) < tv
        g = jnp.where(valid, g, 0.0)
        beta = jnp.where(valid, beta, 0.0)
    g_scr[...] = g
    b_scr[...] = beta

    row = _iota((c, c), 0)
    col = _iota((c, c), 1)
    ige = row >= col
    igt = row > col
    tri = jnp.where(ige, 1.0, 0.0).astype(BF16)
    lane_c = _iota((c, 128), 1)
    nw = nw_ref[...]

    sels = [jnp.where(lane_c == N_HEADS + h, 1.0, 0.0).astype(BF16) for h in range(N_HEADS)]

    def prep(starts):
        items = [(ci, h) for ci in range(len(starts)) for h in range(N_HEADS)]
        gcums = [_exact_dot(tri, g_scr[pl.ds(r0, c), :]) for r0 in starts]
        bchs = [b_scr[pl.ds(r0, c), :] for r0 in starts]
        qns, kns, vhs = [], [], []
        for ci, h in items:
            r0 = starts[ci]
            qh = qkv_scr[pl.ds(r0, c), h * dk:(h + 1) * dk]
            kh = qkv_scr[pl.ds(r0, c), hq + h * dk:hq + (h + 1) * dk]
            vhs.append(qkv_scr[pl.ds(r0, c), 2 * hq + h * dk:2 * hq + (h + 1) * dk])
            qns.append(qh * lax.rsqrt(jnp.sum(qh * qh, axis=-1, keepdims=True) + EPS) * (dk ** -0.5))
            kns.append(kh * lax.rsqrt(jnp.sum(kh * kh, axis=-1, keepdims=True) + EPS))
        qkks = [_dot_nt(jnp.concatenate([qn, kn], axis=0).astype(BF16), kn.astype(BF16))
                for qn, kn in zip(qns, kns)]
        grows = [_exact_dot_nt(sels[h], gcums[ci]) for ci, h in items]
        gcols = [gcums[ci][:, N_HEADS + h:N_HEADS + h + 1] for ci, h in items]
        bcols = [bchs[ci][:, h:h + 1] for ci, h in items]
        decays = [jnp.where(ige, jnp.exp(jnp.where(ige, gcol - grow, 0.0)), 0.0)
                  for gcol, grow in zip(gcols, grows)]
        t_invs = _neumann([jnp.where(igt, qkk[c:] * decay * bcol, 0.0)
                           for qkk, decay, bcol in zip(qkks, decays, bcols)], c)
        egs = [jnp.exp(gcol) for gcol in gcols]
        sols = [_dot(t_inv.astype(BF16),
                     jnp.concatenate([vh * bcol, kn * (bcol * eg)], axis=1).astype(BF16))
                for t_inv, vh, kn, bcol, eg in zip(t_invs, vhs, kns, bcols, egs)]
        glasts = [gcums[ci][c - 1:c, N_HEADS + h:N_HEADS + h + 1] for ci, h in items]
        kws = [(kn * jnp.exp(glast - gcol)).astype(BF16) for kn, glast, gcol in zip(kns, glasts, gcols)]
        aqks = [(qkk[:c] * decay).astype(BF16) for qkk, decay in zip(qkks, decays)]
        sol_bs = [sol.astype(BF16) for sol in sols]
        ktus = [_dot_tn(kw, sol_b) for kw, sol_b in zip(kws, sol_bs)]
        aus = [_dot(aqk, sol_b) for aqk, sol_b in zip(aqks, sol_bs)]
        out = []
        for i in range(len(items)):
            lhs = jnp.concatenate([-ktus[i][:, dk:], qns[i] * egs[i] - aus[i][:, dk:]], axis=0).astype(BF16)
            out.append((lhs, ktus[i][:, :dk], aus[i][:, :dk], jnp.exp(glasts[i])))
        return [out[ci * N_HEADS:(ci + 1) * N_HEADS] for ci in range(len(starts))]

    def chunks(starts, seq_ids):
        preps = prep(starts)
        for r0, sid, heads in zip(starts, seq_ids, preps):
            s_olds = [s_scr[h] if sid is None else s0_ref[sid, h] for h in range(N_HEADS)]
            res = [_dot(heads[h][0], s_olds[h].astype(BF16)) for h in range(N_HEADS)]
            for h in range(N_HEADS):
                s_new = s_olds[h] * heads[h][3] + (heads[h][1] + res[h][:dk])
                if sid is None:
                    s_scr[h] = s_new
                else:
                    so_ref[sid, h] = s_new
            for h in range(N_HEADS):
                zz = z_ref[pl.ds(r0, c), h * dk:(h + 1) * dk]
                o = heads[h][2] + res[h][dk:]
                o_ref[pl.ds(r0, c), h * dk:(h + 1) * dk] = _rms(o, nw) * _silu(zz)

    if per_seq:
        chunks([s * c for s in range(nch)], list(range(nch)))
    else:
        def body(it, carry):
            chunks([pl.multiple_of((it * unroll + i) * c, c) for i in range(unroll)], [None] * unroll)
            return carry

        lax.fori_loop(0, nch // unroll, body, 0)

        @pl.when(gi == pl.num_programs(1) - 1)
        def _():
            convo_ref[0] = xbuf[0:8, :]
            so_ref[0] = s_scr[...]


def _gdn(qkvz, gates, conv0, s0, s_idx, conv_w, hp, nw, s_stack, out_idx, *, bn, c, nch, tv, per_seq=False):
    n = qkvz.shape[0]
    dk = s0.shape[-1]
    cc = 3 * N_HEADS * dk
    tg = c * nch
    sb = nch if per_seq else 1
    steps = 1 if per_seq else n // (bn * tg)
    kern = functools.partial(_gdn_kernel, c=c, nch=nch, tv=tv, dk=dk, unroll=math.gcd(nch, 8), per_seq=per_seq)
    return pl.pallas_call(
        kern, grid=(bn // sb, steps),
        in_specs=[pl.BlockSpec((tg, cc), lambda b, g: (b * steps + g, 0)),
                  pl.BlockSpec((tg, N_HEADS * dk), lambda b, g: (b * steps + g, 3)),
                  pl.BlockSpec((tg, 128), lambda b, g: (b * steps + g, 0)),
                  pl.BlockSpec((sb, 8, cc), lambda b, g: (b, 0, 0)),
                  pl.BlockSpec((None, sb, N_HEADS, dk, dk), lambda b, g: (s_idx, b, 0, 0, 0)),
                  pl.BlockSpec((CONV_W, cc), lambda b, g: (0, 0)),
                  pl.BlockSpec((8, 128), lambda b, g: (0, 0)),
                  pl.BlockSpec((1, dk), lambda b, g: (0, 0)),
                  pl.BlockSpec(memory_space=pl.ANY)],
        out_specs=[pl.BlockSpec((tg, N_HEADS * dk), lambda b, g: (b * steps + g, 0)),
                   pl.BlockSpec((sb, 8, cc), lambda b, g: (b, 0, 0)),
                   pl.BlockSpec((None, sb, N_HEADS, dk, dk), lambda b, g: (out_idx, b, 0, 0, 0))],
        out_shape=[jax.ShapeDtypeStruct((n, N_HEADS * dk), F32),
                   jax.ShapeDtypeStruct((bn, 8, cc), F32),
                   jax.ShapeDtypeStruct(s_stack.shape, F32)],
        input_output_aliases={8: 2},
        scratch_shapes=[pltpu.VMEM((sb * 8 + tg, cc), F32), pltpu.VMEM((tg, cc), F32),
                        pltpu.VMEM((tg, 128), F32), pltpu.VMEM((tg, 128), F32),
                        pltpu.VMEM((N_HEADS, dk, dk), F32)],
        compiler_params=_cparams(("parallel", "arbitrary")), name="gated_delta")(
            qkvz, qkvz, gates, conv0, s0, conv_w, hp, nw.reshape(1, dk), s_stack)


def _mlstm_kernel(q_ref, k_ref, v_ref, z_ref, og_ref, gt_ref, c0_ref, n0_ref, m0_ref, hp_ref, nw_ref, stack_ref,
                  o_ref, co_ref, no_ref, mo_ref,
                  li_scr, lf_scr, cext, m_scr, *, c, nch, sb, tv, dqk, dv):
    gi = pl.program_id(1)
    tg = c * nch
    lane1 = _iota((dqk, 128), 1)
    eye = _iota((dqk, dqk), 0) == _iota((dqk, dqk), 1)

    @pl.when(gi == 0)
    def _():
        for s in range(sb):
            m_scr[s * 8:(s + 1) * 8, :] = jnp.zeros((8, 128), F32)
            for h in range(N_HEADS):
                cext[s * N_HEADS + h, :, 0:dv] = c0_ref[s, h]
                ncol = jnp.sum(jnp.where(eye, n0_ref[s, h:h + 1, :], 0.0), axis=1, keepdims=True)
                cext[s * N_HEADS + h, :, dv:dv + 128] = jnp.broadcast_to(ncol, (dqk, 128))
                m_scr[s * 8 + h:s * 8 + h + 1, :] = jnp.broadcast_to(m0_ref[s, 0:1, h:h + 1], (1, 128))

    hp = hp_ref[...]
    lane = _iota((tg, 128), 1)
    for s in range(sb):
        gt = gt_ref[s]
        li = gt + hp[0:1, :]
        x = gt + hp[1:2, :]
        lf = jnp.minimum(x, 0.0) - jnp.log(1.0 + jnp.exp(-jnp.abs(x)))
        lf = jnp.where(lane >= N_HEADS, jnp.where(lane < 2 * N_HEADS, lf, 0.0), 0.0)
        if tv < c:
            valid = (_iota((tg, 128), 0) % c) < tv
            lf = jnp.where(valid, lf, 0.0)
            li = jnp.where(valid, li, NEG)
        li_scr[s] = li
        lf_scr[s] = lf

    row = _iota((c, c), 0)
    col = _iota((c, c), 1)
    ige = row >= col
    tri = jnp.where(ige, 1.0, 0.0).astype(BF16)
    ones_b = jnp.ones((c, 128), BF16)
    lane_c = _iota((c, 128), 1)
    ones_blk = jnp.ones((c, 128), F32)
    nw = nw_ref[...]
    nrep = dv // 128 + 1

    def wide(x):
        return jnp.concatenate([x] * nrep, axis=1)

    def prep(clist):
        items = [(ci, h) for ci in range(len(clist)) for h in range(N_HEADS)]
        bcums = [_exact_dot(tri, lf_scr[s, pl.ds(r0, c), :]) for s, r0 in clist]
        lichs = [li_scr[s, pl.ds(r0, c), :] for s, r0 in clist]
        qbs = [(q_ref[clist[ci][0], pl.ds(clist[ci][1], c), h * dqk:(h + 1) * dqk] * (dqk ** -0.5)).astype(BF16)
               for ci, h in items]
        ks = [k_ref[clist[ci][0], pl.ds(clist[ci][1], c), h * dqk:(h + 1) * dqk] for ci, h in items]
        qk_raws = [_dot_nt(qb, k.astype(BF16)) for qb, k in zip(qbs, ks)]
        bcols = [jnp.broadcast_to(bcums[ci][:, N_HEADS + h:N_HEADS + h + 1], (c, 128)) for ci, h in items]
        licols = [jnp.broadcast_to(lichs[ci][:, h:h + 1], (c, 128)) for ci, h in items]
        rowvs = [_exact_dot_nt(ones_b, jnp.where(lane_c == 0, licol - bcol, 0.0))
                 for licol, bcol in zip(licols, bcols)]
        dmats = [jnp.where(ige, bcol[:, :c] + rowv, NEG) for bcol, rowv in zip(bcols, rowvs)]
        dmaxs = [jnp.broadcast_to(jnp.max(dmat, axis=-1, keepdims=True), (c, 128)) for dmat in dmats]
        vexts = [jnp.concatenate([v_ref[clist[ci][0], pl.ds(clist[ci][1], c), h * dv:(h + 1) * dv], ones_blk],
                                 axis=1).astype(BF16) for ci, h in items]
        intras = [_dot((qk_raw * jnp.exp(dmat - dmax[:, :c])).astype(BF16), vext)
                  for qk_raw, dmat, dmax, vext in zip(qk_raws, dmats, dmaxs, vexts)]
        blasts = [bcol[c - 1:c, :] for bcol in bcols]
        mus = [dmax[c - 1:c, :] for dmax in dmaxs]
        kvs = [_dot_tn((k * jnp.exp(blast - bcol + licol - mu)).astype(BF16), vext)
               for k, blast, bcol, licol, mu, vext in zip(ks, blasts, bcols, licols, mus, vexts)]
        out = [(qbs[i], bcols[i], dmaxs[i], intras[i], blasts[i], mus[i], kvs[i]) for i in range(len(items))]
        return [out[ci * N_HEADS:(ci + 1) * N_HEADS] for ci in range(len(clist))]

    def chunks(clist):
        hs = range(N_HEADS)
        for (s, r0), heads in zip(clist, prep(clist)):
            c_olds = [cext[s * N_HEADS + h] for h in hs]
            qcs = [_dot(heads[h][0], c_olds[h].astype(BF16)) for h in hs]
            for h in hs:
                qb, bcol, dmax, intra, blast, mu, kv = heads[h]
                mprev = m_scr[s * 8 + h:s * 8 + h + 1, :]
                inter = bcol + mprev
                mt = jnp.maximum(inter, dmax)
                mnew = jnp.maximum(blast + mprev, mu)
                cext[s * N_HEADS + h] = (c_olds[h] * wide(jnp.exp(blast + mprev - mnew))
                                         + wide(jnp.exp(mu - mnew)) * kv)
                m_scr[s * 8 + h:s * 8 + h + 1, :] = mnew
                tot = wide(jnp.exp(inter - mt)) * qcs[h] + wide(jnp.exp(dmax - mt)) * intra
                den = jnp.maximum(jnp.abs(tot[:, dv:dv + 128]), jnp.exp(-mt))
                hh = tot[:, :dv] / jnp.concatenate([den] * (dv // 128), axis=1)
                og = og_ref[s, pl.ds(r0, c), h * dv:(h + 1) * dv]
                zz = z_ref[s, pl.ds(r0, c), h * dv:(h + 1) * dv]
                o_ref[s, pl.ds(r0, c), h * dv:(h + 1) * dv] = _rms(_sigmoid(og) * hh, nw) * _silu(zz)

    chunks([(s, i * c) for i in range(nch) for s in range(sb)])

    @pl.when(gi == pl.num_programs(1) - 1)
    def _():
        for s in range(sb):
            for h in range(N_HEADS):
                co_ref[s, h] = cext[s * N_HEADS + h, :, 0:dv]
                ncol = jnp.broadcast_to(cext[s * N_HEADS + h, :, dv:dv + 1], (dqk, dqk))
                no_ref[s, h:h + 1, :] = jnp.sum(jnp.where(eye, ncol, 0.0), axis=0, keepdims=True)
            mrows = m_scr[s * 8:(s + 1) * 8, :]
            diag = jnp.where(_iota((8, 128), 0) == _iota((8, 128), 1), mrows, 0.0)
            mo_ref[s] = jnp.broadcast_to(jnp.sum(diag, axis=0, keepdims=True), (8, 128))


def _mlstm(q, k, v, z, og, gates, c0, c_idx, n0, m0, hp, nw, c_stack, out_idx, *, bn, c, nch, sb, tv):
    n = q.shape[0]
    dqk, dv = c0.shape[-2], c0.shape[-1]
    tg = c * nch
    rows = n // bn
    steps = rows // tg
    hq, hv = N_HEADS * dqk, N_HEADS * dv
    row = lambda b, g: (b, g, 0)
    q, k, v, z, og, gates = (a.reshape(bn, rows, a.shape[1]) for a in (q, k, v, z, og, gates))
    kern = functools.partial(_mlstm_kernel, c=c, nch=nch, sb=sb, tv=tv, dqk=dqk, dv=dv)
    out = pl.pallas_call(
        kern, grid=(bn // sb, steps),
        in_specs=[pl.BlockSpec((sb, tg, hq), row), pl.BlockSpec((sb, tg, hq), row),
                  pl.BlockSpec((sb, tg, hv), row), pl.BlockSpec((sb, tg, hv), row),
                  pl.BlockSpec((sb, tg, hv), row), pl.BlockSpec((sb, tg, 128), row),
                  pl.BlockSpec((None, sb, N_HEADS, dqk, dv), lambda b, g: (c_idx, b, 0, 0, 0)),
                  pl.BlockSpec((sb, N_HEADS, dqk), lambda b, g: (b, 0, 0)),
                  pl.BlockSpec((sb, 8, 128), lambda b, g: (b, 0, 0)),
                  pl.BlockSpec((8, 128), lambda b, g: (0, 0)),
                  pl.BlockSpec((1, dv), lambda b, g: (0, 0)),
                  pl.BlockSpec(memory_space=pl.ANY)],
        out_specs=[pl.BlockSpec((sb, tg, hv), row),
                   pl.BlockSpec((None, sb, N_HEADS, dqk, dv), lambda b, g: (out_idx, b, 0, 0, 0)),
                   pl.BlockSpec((sb, N_HEADS, dqk), lambda b, g: (b, 0, 0)),
                   pl.BlockSpec((sb, 8, 128), lambda b, g: (b, 0, 0))],
        out_shape=[jax.ShapeDtypeStruct((bn, rows, hv), F32),
                   jax.ShapeDtypeStruct(c_stack.shape, F32),
                   jax.ShapeDtypeStruct((bn, N_HEADS, dqk), F32),
                   jax.ShapeDtypeStruct((bn, 8, 128), F32)],
        input_output_aliases={11: 1},
        scratch_shapes=[pltpu.VMEM((sb, tg, 128), F32), pltpu.VMEM((sb, tg, 128), F32),
                        pltpu.VMEM((sb * N_HEADS, dqk, dv + 128), F32), pltpu.VMEM((sb * 8, 128), F32)],
        compiler_params=_cparams(("parallel", "arbitrary")), name="mlstm")(
            q, k, v, z, og, gates, c0, n0, m0, hp, nw.reshape(1, dv), c_stack)
    return (out[0].reshape(n, hv),) + tuple(out[1:])


def _lambda(lq, lam_init):
    a = jnp.sum(lq[0:1, :] * lq[1:2, :], axis=-1, keepdims=True)
    b = jnp.sum(lq[2:3, :] * lq[3:4, :], axis=-1, keepdims=True)
    return jnp.exp(a) - jnp.exp(b) + lam_init


def _attn_kernel(wq_ref, k_ref, vt_ref, z_ref, lq_ref, nw_ref, o_ref, acc_scr, *, t, hpb, lam_init):
    qi = pl.program_id(2)
    dh = k_ref.shape[1] // hpb
    nl = 4 * t
    acc_scr[...] = jnp.zeros(acc_scr.shape, F32)

    def step(jp, carry, masked):
        rows = [pl.multiple_of((2 * jp + i) * t, t) for i in range(2)]
        wqs = [jnp.concatenate([wq_ref[0, a], wq_ref[1, a]], axis=1) for a in range(hpb)]
        scores = [[_dot(k_ref[pl.ds(rows[i], t), a * dh:(a + 1) * dh], wqs[a]) for a in range(hpb)]
                  for i in range(2)]
        out = list(carry)
        for i in range(2):
            for a in range(hpb):
                m_prev, l_prev = out[a]
                s = scores[i][a]
                if masked:
                    lane = _iota((t, nl), 1)
                    qpos = (lane // (2 * t)) * t + lane % t
                    s = jnp.where(_iota((t, nl), 0) + i * t <= qpos, s, NEG)
                m_new = jnp.maximum(m_prev, jnp.max(s, axis=0, keepdims=True))
                alpha = jnp.exp2(m_prev - m_new)
                p = jnp.exp2(s - m_new)
                l_new = alpha * l_prev + jnp.sum(p, axis=0, keepdims=True)
                acc_scr[a] = alpha * acc_scr[a] + _dot(vt_ref[2 * jp + i, a * dh:(a + 1) * dh, :], p.astype(BF16))
                out[a] = (m_new, l_new)
        return tuple(out)

    init = tuple((jnp.full((1, nl), NEG, F32), jnp.zeros((1, nl), F32)) for _ in range(hpb))
    carry = lax.fori_loop(0, qi, lambda j, c: step(j, c, False), init)
    carry = step(qi, carry, True)

    lam = _lambda(lq_ref[...], lam_init)
    nw = nw_ref[...]
    for a in range(hpb):
        acc = acc_scr[a] * (1.0 / carry[a][1])
        for qb in range(2):
            b0 = 2 * t * qb
            o = (acc[:, b0:b0 + t] - lam * acc[:, b0 + t:b0 + 2 * t]).T
            zz = z_ref[qb * t:(qb + 1) * t, a * dh:(a + 1) * dh]
            o_ref[qb * t:(qb + 1) * t, a * dh:(a + 1) * dh] = _rms(o, nw) * (1.0 - lam_init) * _silu(zz)


def _attn_prompt(wq, k16, vt, z, lq, nw, *, bn, seq, lam_init):
    n = k16.shape[0]
    t = vt.shape[2]
    dh = wq.shape[2]
    hpb = 4
    nq = seq // t
    nq2 = nq // 2
    kern = functools.partial(_attn_kernel, t=t, hpb=hpb, lam_init=lam_init)
    return pl.pallas_call(
        kern, grid=(bn, N_HEADS // hpb, nq2),
        in_specs=[pl.BlockSpec((2, hpb, dh, 2 * t), lambda b, h, i: (b * nq2 + i, h, 0, 0)),
                  pl.BlockSpec((seq, hpb * dh), lambda b, h, i: (b, h), pipeline_mode=pl.Buffered(1)),
                  pl.BlockSpec((nq, hpb * dh, t), lambda b, h, i: (b, h, 0), pipeline_mode=pl.Buffered(1)),
                  pl.BlockSpec((2 * t, hpb * dh), lambda b, h, i: (b * nq2 + i, h)),
                  pl.BlockSpec(lq.shape, lambda b, h, i: (0, 0)),
                  pl.BlockSpec((1, dh), lambda b, h, i: (0, 0))],
        out_specs=pl.BlockSpec((2 * t, hpb * dh), lambda b, h, i: (b * nq2 + i, h)),
        out_shape=jax.ShapeDtypeStruct((n, N_HEADS * dh), F32),
        scratch_shapes=[pltpu.VMEM((hpb, dh, 4 * t), F32)],
        compiler_params=_cparams(("parallel", "parallel", "arbitrary")), name="diff_attn_prompt")(
            wq, k16, vt, z, lq, nw.reshape(1, dh))


def _attn_s_kernel(pt_ref, q_ref, kn_ref, vn_ref, z_ref, lq_ref, nw_ref, *rest, pg, tv, lam_init):
    k_pages = rest[:pg]
    v_pages = rest[pg:2 * pg]
    o_ref, qbd, bias, m_scr, l_scr, acc_scr = rest[2 * pg:]
    j = pl.program_id(1)
    tp, hd = q_ref.shape
    dh = hd // N_HEADS
    dqk = dh // 2
    prow = k_pages[0].shape[0]
    rph = 2 * tp

    @pl.when(j == 0)
    def _():
        q = q_ref[...] * (dqk ** -0.5)
        lane = _iota((tp, dh), 1)
        for h in range(N_HEADS):
            qh = q[:, h * dh:(h + 1) * dh]
            qbd[h * rph:h * rph + tp, :] = jnp.where(lane < dqk, qh, 0.0)
            qbd[h * rph + tp:(h + 1) * rph, :] = jnp.where(lane >= dqk, qh, 0.0)
        rhead = _iota(bias.shape, 0) // rph
        chead = _iota(bias.shape, 1) % N_HEADS
        bias[...] = jnp.where(rhead == chead, 0.0, NEG)
        m_scr[...] = jnp.full(m_scr.shape, NEG, F32)
        l_scr[...] = jnp.zeros(l_scr.shape, F32)
        acc_scr[...] = jnp.zeros(acc_scr.shape, F32)

    def update(s, pv):
        m_prev = m_scr[...]
        m_new = jnp.maximum(m_prev, jnp.max(s, axis=-1, keepdims=True))
        alpha = jnp.exp(m_prev - m_new)
        p = jnp.exp(s - m_new[:, 0:1])
        l_scr[...] = alpha * l_scr[...] + jnp.sum(p, axis=-1, keepdims=True)
        acc_scr[...] = alpha * acc_scr[...] + pv(p)
        m_scr[...] = m_new

    qb = qbd[...].astype(BF16)
    bs_ = bias[...]
    s = jnp.concatenate([_dot_nt(qb, kp[...].astype(BF16)) + bs_ for kp in k_pages], axis=1)

    def pv_pages(p):
        pb = p.astype(BF16)
        acc = _dot(pb[:, 0:prow], v_pages[0][...].astype(BF16))
        for i in range(1, pg):
            acc = acc + _dot(pb[:, i * prow:(i + 1) * prow], v_pages[i][...].astype(BF16))
        return acc

    update(s, pv_pages)

    @pl.when(j == pl.num_programs(1) - 1)
    def _():
        qf = qbd[...]
        kn = kn_ref[...]
        vn = vn_ref[...]
        sn = jnp.concatenate([_dot_nt(qf[h * rph:(h + 1) * rph], kn[:, h * dh:(h + 1) * dh])
                              for h in range(N_HEADS)], axis=0)
        tq = _iota(sn.shape, 0) % tp
        tk = _iota(sn.shape, 1)
        sn = jnp.where(tk <= tq, jnp.where(tk < tv, sn, NEG), NEG)
        update(sn, lambda p: jnp.concatenate(
            [_dot(p[h * rph:(h + 1) * rph], vn[:, h * dh:(h + 1) * dh]) for h in range(N_HEADS)], axis=0))
        acc = acc_scr[...] / l_scr[...]
        lam = _lambda(lq_ref[...], lam_init)
        nw = nw_ref[...]
        normed = jnp.concatenate(
            [_rms(acc[h * rph:h * rph + tp] - lam * acc[h * rph + tp:(h + 1) * rph], nw) for h in range(N_HEADS)],
            axis=1)
        o_ref[...] = normed * (1.0 - lam_init) * _silu(z_ref[...])


def _attn_sample(page_table, q, kn, vn, z, lq, nw, cache_k, cache_v, *, e, tv, lam_init):
    bs, npages = page_table.shape
    n, hd = q.shape
    tp = n // bs
    dh = hd // N_HEADS
    pg = math.gcd(npages, PAGES_PER_STEP)
    prow = cache_k.shape[2]
    steps = npages // pg
    rows = 2 * N_HEADS * tp
    tok = lambda b, j, pt: (b, 0)
    page_specs = [pl.BlockSpec((None, None, prow, dh),
                               lambda b, j, pt, i=i: (e, pt[b, j * pg + i], 0, 0)) for i in range(pg)]
    kern = functools.partial(_attn_s_kernel, pg=pg, tv=tv, lam_init=lam_init)
    grid_spec = pltpu.PrefetchScalarGridSpec(
        num_scalar_prefetch=1, grid=(bs, steps),
        in_specs=[pl.BlockSpec((tp, hd), tok), pl.BlockSpec((tp, hd), tok), pl.BlockSpec((tp, hd), tok),
                  pl.BlockSpec((tp, hd), tok),
                  pl.BlockSpec(lq.shape, lambda b, j, pt: (0, 0)),
                  pl.BlockSpec((1, dh), lambda b, j, pt: (0, 0))] + page_specs + page_specs,
        out_specs=pl.BlockSpec((tp, hd), tok),
        scratch_shapes=[pltpu.VMEM((rows, dh), F32), pltpu.VMEM((rows, prow), F32), pltpu.VMEM((rows, 128), F32),
                        pltpu.VMEM((rows, 128), F32), pltpu.VMEM((rows, dh), F32)])
    return pl.pallas_call(
        kern, grid_spec=grid_spec, out_shape=jax.ShapeDtypeStruct((n, hd), F32),
        compiler_params=_cparams(("parallel", "arbitrary")), name="diff_attn_sample")(
            page_table, q, kn, vn, z, lq, nw.reshape(1, dh), *([cache_k] * pg), *([cache_v] * pg))


def _rope_tables(pos, dqk):
    rope_dim = dqk // 4
    half = rope_dim // 2
    inv = jnp.power(ROPE_THETA, -jnp.arange(half, dtype=F32) / half)
    ang = pos.astype(F32)[:, None] * inv[None, :]
    lane = jnp.arange(128) % dqk
    idx = lane % half
    cos = jnp.where(lane < rope_dim, jnp.cos(ang)[:, idx], 1.0)
    sin = jnp.sin(ang)[:, idx]
    sa = jnp.where((lane >= half) & (lane < rope_dim), sin, 0.0)
    sb = jnp.where(lane < half, -sin, 0.0)
    return cos.astype(F32), sa.astype(F32), sb.astype(F32)


def _lane_row(vals, offset):
    out = jnp.zeros((128,), F32)
    return out.at[offset:offset + vals.shape[0]].set(vals.astype(F32))


def kernel(x_prompt, x_sample, cache_k, cache_v, page_table, state_gdn_conv, state_gdn_s, state_mlstm_c,
           state_mlstm_n, state_mlstm_m, norm_w, final_norm_w, w_in_even, w_out_even, conv_w, a_log, dt_bias,
           gdn_norm_w, lambda_qk, diff_norm_w, w_in_odd, w_out_odd, b_i, b_f, mlstm_norm_w):
    bp, seq, d = x_prompt.shape
    bs, ts, _ = x_sample.shape
    depth = norm_w.shape[0]
    tp = SAMPLE_PAD
    dk_a = state_gdn_s.shape[-1]
    conv_ch = state_gdn_conv.shape[-1]
    w_a = N_HEADS * dk_a
    dh_b = cache_v.shape[-1]
    dqk_b = dh_b // 2
    w_b = N_HEADS * dh_b
    dqk_c, dv_c = state_mlstm_c.shape[-2], state_mlstm_c.shape[-1]
    hq_c, w_c = N_HEADS * dqk_c, N_HEADS * dv_c
    past = page_table.shape[1] * cache_k.shape[2]
    n_pool, page = cache_k.shape[1], cache_k.shape[2]

    xp = x_prompt.reshape(bp * seq, d)
    xs = jnp.pad(x_sample, ((0, 0), (0, tp - ts), (0, 0))).reshape(bs * tp, d)
    tm_p = math.gcd(seq, 256)
    tm_s = math.gcd(bs * tp, 256)
    nch_p = math.gcd(seq // CHUNK, 8)
    seq_s = math.gcd(bs, 8)
    seq_c = math.gcd(bs, 8)

    rope_p = _rope_tables(jnp.arange(seq), dqk_b)
    rope_s = _rope_tables(past + (jnp.arange(tm_s) % tp), dqk_b)
    half = dqk_b // 8
    ang_t = (jnp.power(ROPE_THETA, -jnp.arange(half, dtype=F32) / half)[:, None]
             * jnp.arange(seq).astype(F32)[None, :])
    cos_t, sin_t = jnp.cos(ang_t), jnp.sin(ang_t)

    even_w = conv_ch + w_a
    seg_even = [(0, even_w), (even_w, w_b), (even_w + w_b, w_b), (even_w + 2 * w_b, w_b),
                (even_w + 3 * w_b, w_b), (even_w + 4 * w_b, 128)]
    seg_even_p = [seg_even[0]] + seg_even[2:]
    rope_even = tuple(range(even_w // 128, (even_w + 2 * w_b) // 128))
    odd_w = 2 * hq_c + 3 * w_c
    seg_odd = [(0, hq_c), (hq_c, hq_c), (2 * hq_c, w_c), (2 * hq_c + w_c, w_c), (2 * hq_c + 2 * w_c, w_c),
               (odd_w, 128)]

    ck = cache_k.reshape(cache_k.shape[0], n_pool, page * N_HEADS, dh_b)
    cv = cache_v.reshape(cache_v.shape[0], n_pool, page * N_HEADS, dh_b)

    n_even = (depth + 1) // 2
    krows_p = jnp.zeros((n_even, bp * seq * N_HEADS, dh_b), F32)
    vrows_p = jnp.zeros((n_even, bp * seq * N_HEADS, dh_b), F32)
    k_rows_s, v_rows_s = [], []
    n_odd = depth // 2
    s_p = jnp.zeros((n_even, bp, N_HEADS, dk_a, dk_a), F32)
    s_s = jnp.zeros((n_even, bs, N_HEADS, dk_a, dk_a), F32)
    c_p = jnp.zeros((n_odd, bp, N_HEADS, dqk_c, dv_c), F32)
    c_s = jnp.zeros((n_odd, bs, N_HEADS, dqk_c, dv_c), F32)
    conv_p, conv_s = [], []
    n_p, m_p, n_s, m_s = [], [], [], []

    pend_p = pend_s = None
    for layer in range(depth):
        if layer % 2 == 0:
            e = layer // 2
            lam_init = 0.8 - 0.6 * math.exp(-0.3 * layer)
            w = w_in_even[e]
            g0 = even_w
            w_r = jnp.concatenate([w[:, :g0], w[:, g0 + 2 * N_HEADS:], w[:, g0:g0 + 2 * N_HEADS],
                                   jnp.zeros((d, 128 - 2 * N_HEADS), F32)], axis=1).astype(BF16)
            w_o = w_out_even[e].astype(BF16)
            hp = jnp.zeros((8, 128), F32).at[0].set(_lane_row(a_log[e], N_HEADS)).at[1].set(
                _lane_row(dt_bias[e], N_HEADS))
            lq = lambda_qk[e]

            wq_t = w[:, g0 + 2 * N_HEADS:g0 + 2 * N_HEADS + w_b].T.astype(BF16)
            wv_t = w[:, g0 + 2 * N_HEADS + 2 * w_b:g0 + 2 * N_HEADS + 3 * w_b].T.astype(BF16)
            outs = _proj(xp, norm_w[layer], w_r, seg_even_p, (rope_even, rope_p), tm_p, k16_seg=1,
                         feat=(wq_t, wv_t, cos_t, sin_t), rows=((1, 2), e, (krows_p, vrows_p)), pre=pend_p)
            if pend_p is not None:
                xp = outs.pop()
            az, krows_p, vrows_p, zb, gt, k16, wq, vt = outs
            oa, cst, s_p = _gdn(az, gt, jnp.zeros((bp, 8, conv_ch), F32),
                                jnp.zeros((1, bp, N_HEADS, dk_a, dk_a), F32), 0,
                                conv_w[e], hp, gdn_norm_w[e], s_p, e, bn=bp, c=CHUNK, nch=nch_p, tv=CHUNK)
            ob = _attn_prompt(wq, k16, vt, zb, lq, diff_norm_w[e], bn=bp, seq=seq, lam_init=lam_init)
            pend_p = (oa, ob, w_a, 0, w_o)
            conv_p.append(cst[:, 8 - (CONV_W - 1):])

            outs = _proj(xs, norm_w[layer], w_r, seg_even, (rope_even, rope_s), tm_s, pre=pend_s)
            if pend_s is not None:
                xs = outs.pop()
            az, qb, kb, vb, zb, gt = outs
            conv0 = jnp.pad(state_gdn_conv[e], ((0, 0), (8 - (CONV_W - 1), 0), (0, 0)))
            oa, cst, s_s = _gdn(az, gt, conv0, state_gdn_s, e, conv_w[e], hp, gdn_norm_w[e], s_s, e,
                                bn=bs, c=tp, nch=seq_s, tv=ts, per_seq=True)
            ob = _attn_sample(page_table, qb, kb, vb, zb, lq, diff_norm_w[e], ck, cv, e=e, tv=ts,
                              lam_init=lam_init)
            pend_s = (oa, ob, w_a, 0, w_o)
            k_rows_s.append(kb.reshape(bs, tp, N_HEADS, dh_b)[:, :ts])
            v_rows_s.append(vb.reshape(bs, tp, N_HEADS, dh_b)[:, :ts])
            conv_s.append(cst[:, 8 - (CONV_W - 1):])
        else:
            o = layer // 2
            w_r = jnp.concatenate([w_in_odd[o], jnp.zeros((d, 128 - 2 * N_HEADS), F32)], axis=1).astype(BF16)
            w_o = w_out_odd[o].astype(BF16)
            hp = jnp.zeros((8, 128), F32).at[0].set(_lane_row(b_i[o], 0)).at[1].set(_lane_row(b_f[o], N_HEADS))

            outs = _proj(xp, norm_w[layer], w_r, seg_odd, None, tm_p, pre=pend_p)
            if pend_p is not None:
                xp = outs.pop()
            q, k, v, z, og, gt = outs
            hm, c_p, nf, mf = _mlstm(q, k, v, z, og, gt, jnp.zeros((1, bp, N_HEADS, dqk_c, dv_c), F32), 0,
                                     jnp.zeros((bp, N_HEADS, dqk_c), F32), jnp.zeros((bp, 8, 128), F32),
                                     hp, mlstm_norm_w[o], c_p, o, bn=bp, c=CHUNK, nch=math.gcd(seq // CHUNK, 8),
                                     sb=1, tv=CHUNK)
            pend_p = (hm, hm, w_c // 2, 1, w_o)
            n_p.append(nf)
            m_p.append(mf[:, 0, :N_HEADS])

            outs = _proj(xs, norm_w[layer], w_r, seg_odd, None, tm_s, pre=pend_s)
            if pend_s is not None:
                xs = outs.pop()
            q, k, v, z, og, gt = outs
            m0 = jnp.zeros((bs, 8, 128), F32).at[:, 0, :N_HEADS].set(state_mlstm_m[o])
            hm, c_s, nf, mf = _mlstm(q, k, v, z, og, gt, state_mlstm_c, o, state_mlstm_n[o], m0,
                                     hp, mlstm_norm_w[o], c_s, o, bn=bs, c=tp, nch=1, sb=seq_c, tv=ts)
            pend_s = (hm, hm, w_c // 2, 1, w_o)
            n_s.append(nf)
            m_s.append(mf[:, 0, :N_HEADS])

    xp = _out_proj(*pend_p, xp, final_norm_w, True, tm_p)
    xs = _out_proj(*pend_s, xs, final_norm_w, True, tm_s)
    y_prompt = xp.reshape(bp, seq, d)
    y_sample = xs.reshape(bs, tp, d)[:, :ts]
    st = jnp.stack
    kv_shape = (n_even, bp, seq, N_HEADS, dh_b)
    return (y_prompt, y_sample, krows_p.reshape(kv_shape), vrows_p.reshape(kv_shape), st(k_rows_s), st(v_rows_s),
            st(conv_p), s_p, st(conv_s), s_s,
            c_p, st(n_p), st(m_p), c_s, st(n_s), st(m_s))
```

```python
import functools
import math

import jax
import jax.numpy as jnp
from jax import lax
from jax.experimental import pallas as pl
from jax.experimental.pallas import tpu as pltpu

F32 = jnp.float32
BF16 = jnp.bfloat16

EPS = 1e-6
NEG = -1e30
N_HEADS = 4
CONV_W = 4
ROPE_THETA = 500000.0
CHUNK = 64
SAMPLE_PAD = 8
PAGES_PER_STEP = 16
V7X_VMEM_LIMIT = 56 * 1024 * 1024


def _cparams(sem):
    return pltpu.CompilerParams(dimension_semantics=sem, vmem_limit_bytes=V7X_VMEM_LIMIT)


def _dot(a, b):
    return jnp.dot(a, b, preferred_element_type=F32)


def _dot_nt(a, b):
    return lax.dot_general(a, b, (((1,), (1,)), ((), ())), preferred_element_type=F32)


def _dot_tn(a, b):
    return lax.dot_general(a, b, (((0,), (0,)), ((), ())), preferred_element_type=F32)


def _split3(x):
    hi = x.astype(BF16)
    r = x - hi.astype(F32)
    mid = r.astype(BF16)
    lo = (r - mid.astype(F32)).astype(BF16)
    return hi, mid, lo


def _exact_dot(a_bf16, x):
    hi, mid, lo = _split3(x)
    return _dot(a_bf16, hi) + (_dot(a_bf16, mid) + _dot(a_bf16, lo))


def _exact_dot_nt(a_bf16, x):
    hi, mid, lo = _split3(x)
    return _dot_nt(a_bf16, hi) + (_dot_nt(a_bf16, mid) + _dot_nt(a_bf16, lo))


def _sigmoid(x):
    return 1.0 / (1.0 + jnp.exp(-x))


def _silu(x):
    return x * _sigmoid(x)


def _softplus(x):
    return jnp.maximum(x, 0.0) + jnp.log(1.0 + jnp.exp(-jnp.abs(x)))


def _rms(x, w):
    return x * lax.rsqrt(jnp.mean(x * x, axis=-1, keepdims=True) + EPS) * w


def _iota(shape, dim):
    return lax.broadcasted_iota(jnp.int32, shape, dim)


def _proj_kernel(*refs, segs, rope_blocks, k16_seg, feat_major, row_segs, row_lay, n_alias, fuse_out):
    n_out = len(segs)
    pos = 3
    x_ref, nw_ref, w_ref = refs[:3]
    if rope_blocks:
        cos_ref, sa_ref, sb_ref = refs[pos:pos + 3]
        pos += 3
    if feat_major:
        wqt_ref, wvt_ref, cost_ref, sint_ref = refs[pos:pos + 4]
        pos += 4
    if fuse_out:
        a_ref, b_ref, wo_ref = refs[pos:pos + 3]
        pos += 3
    pos += n_alias
    outs = refs[pos:pos + n_out]
    pos += n_out
    x = x_ref[...]
    if fuse_out:
        ka = a_ref.shape[1]
        x = x + (_dot(a_ref[...].astype(BF16), wo_ref[0:ka, :]) + _dot(b_ref[...].astype(BF16), wo_ref[ka:, :]))
        refs[-1][...] = x
    h = _rms(x, nw_ref[...]).astype(BF16)
    for si, ((c0, width), o_ref) in enumerate(zip(segs, outs)):
        for s0 in range(0, width, 512):
            sw = min(512, width - s0)
            acc = _dot(h, w_ref[:, c0 + s0:c0 + s0 + sw])
            if rope_blocks and (c0 + s0) // 128 in rope_blocks:
                cos = cos_ref[...]
                sa = sa_ref[...]
                sb = sb_ref[...]
                parts = []
                for t0 in range(0, sw, 128):
                    a = acc[:, t0:t0 + 128]
                    parts.append(a * cos + pltpu.roll(a, 8, 1) * sa + pltpu.roll(a, 120, 1) * sb)
                acc = jnp.concatenate(parts, axis=1)
            if si in row_segs:
                dh = sw // N_HEADS
                for hd in range(N_HEADS):
                    o_ref[row_lay, pl.ds(hd, acc.shape[0], stride=N_HEADS), :] = acc[:, hd * dh:(hd + 1) * dh]
                for l in range(o_ref.shape[0]):
                    if l != row_lay:
                        o_ref[l] = jnp.zeros(o_ref.shape[1:], F32)
            else:
                o_ref[:, s0:s0 + sw] = acc
            if si == k16_seg:
                refs[pos][:, s0:s0 + sw] = acc.astype(BF16)
    if feat_major:
        wq_ref, vt_ref = refs[pos + 1], refs[pos + 2]
        tm = x.shape[0]
        vt_ref[0] = _dot_nt(wvt_ref[...], h).astype(BF16)
        qt = _dot_nt(wqt_ref[...], h)
        dqk = qt.shape[0] // (2 * N_HEADS)
        half = dqk // 8
        cos = cost_ref[...]
        sin = sint_ref[...]
        scale = (dqk ** -0.5) * math.log2(math.e)
        zero = jnp.zeros((dqk, tm), F32)
        for hd in range(N_HEADS):
            maps = []
            for m in range(2):
                r0 = (2 * hd + m) * dqk
                x1 = qt[r0:r0 + half]
                x2 = qt[r0 + half:r0 + 2 * half]
                maps.append(jnp.concatenate([x1 * cos - x2 * sin, x2 * cos + x1 * sin,
                                             qt[r0 + 2 * half:r0 + dqk]], axis=0) * scale)
            top = jnp.concatenate([maps[0], zero], axis=1)
            bot = jnp.concatenate([zero, maps[1]], axis=1)
            wq_ref[0, hd] = jnp.concatenate([top, bot], axis=0).astype(BF16)


def _proj(x, nw, w, segs, rope, tm, k16_seg=None, feat=None, rows=None, pre=None):
    n, d = x.shape
    ncols = w.shape[1]
    grid = (n // tm,)
    in_specs = [pl.BlockSpec((tm, d), lambda i: (i, 0)),
                pl.BlockSpec((1, d), lambda i: (0, 0)),
                pl.BlockSpec((d, ncols), lambda i: (0, 0))]
    args = [x, nw.reshape(1, d), w]
    rope_blocks = ()
    if rope is not None:
        rope_blocks, tables = rope
        nt = tables[0].shape[0] // tm
        for t in tables:
            in_specs.append(pl.BlockSpec((tm, 128), lambda i, nt=nt: (i % nt, 0)))
            args.append(t)
    out_shape = [jax.ShapeDtypeStruct((n, width), F32) for _, width in segs]
    out_specs = [pl.BlockSpec((tm, width), lambda i: (i, 0)) for _, width in segs]
    if feat is not None:
        wqt, wvt, cost, sint = feat
        ntt = cost.shape[1] // tm
        in_specs += [pl.BlockSpec(wqt.shape, lambda i: (0, 0)), pl.BlockSpec(wvt.shape, lambda i: (0, 0)),
                     pl.BlockSpec((cost.shape[0], tm), lambda i, ntt=ntt: (0, i % ntt)),
                     pl.BlockSpec((sint.shape[0], tm), lambda i, ntt=ntt: (0, i % ntt))]
        args += [wqt, wvt, cost, sint]
    if pre is not None:
        a, b, ka, b_col, w_out = pre
        kb = w_out.shape[0] - ka
        in_specs += [pl.BlockSpec((tm, ka), lambda i: (i, 0)), pl.BlockSpec((tm, kb), lambda i: (i, b_col)),
                     pl.BlockSpec(w_out.shape, lambda i: (0, 0))]
        args += [a, b, w_out]
    row_segs, aliases, row_lay = (), {}, 0
    if rows is not None:
        row_segs, layer, stacked = rows
        for si, arr in zip(row_segs, stacked):
            dh = segs[si][1] // N_HEADS
            if isinstance(arr, int):
                held, first, row_lay, shape = arr, 0, layer, (arr, n * N_HEADS, dh)
            else:
                held, first, shape = 1, layer, arr.shape
                aliases[len(args)] = si
                in_specs.append(pl.BlockSpec(memory_space=pl.ANY))
                args.append(arr)
            out_shape[si] = jax.ShapeDtypeStruct(shape, F32)
            out_specs[si] = pl.BlockSpec((held, tm * N_HEADS, dh), lambda i, first=first: (first, i, 0))
    if k16_seg is not None:
        kw = segs[k16_seg][1]
        out_shape.append(jax.ShapeDtypeStruct((n, kw), BF16))
        out_specs.append(pl.BlockSpec((tm, kw), lambda i: (i, 0)))
    if feat is not None:
        hq, hv = wqt.shape[0], wvt.shape[0]
        dh = hq // N_HEADS
        out_shape += [jax.ShapeDtypeStruct((n // tm, N_HEADS, dh, 2 * tm), BF16),
                      jax.ShapeDtypeStruct((n // tm, hv, tm), BF16)]
        out_specs += [pl.BlockSpec((1, N_HEADS, dh, 2 * tm), lambda i: (i, 0, 0, 0)),
                      pl.BlockSpec((1, hv, tm), lambda i: (i, 0, 0))]
    if pre is not None:
        out_shape.append(jax.ShapeDtypeStruct((n, d), F32))
        out_specs.append(pl.BlockSpec((tm, d), lambda i: (i, 0)))
    return list(pl.pallas_call(
        functools.partial(_proj_kernel, segs=tuple(segs), rope_blocks=tuple(rope_blocks), k16_seg=k16_seg,
                          feat_major=feat is not None, row_segs=tuple(row_segs), row_lay=row_lay,
                          n_alias=len(aliases), fuse_out=pre is not None),
        grid=grid, in_specs=in_specs, out_specs=out_specs, out_shape=out_shape, input_output_aliases=aliases,
        compiler_params=_cparams(("parallel",)), name="norm_proj")(*args))


def _out_kernel(a_ref, b_ref, w_ref, x_ref, fw_ref, o_ref, *, final):
    ka = a_ref.shape[1]
    y = _dot(a_ref[...].astype(BF16), w_ref[0:ka, :]) + _dot(b_ref[...].astype(BF16), w_ref[ka:, :])
    xn = x_ref[...] + y
    if final:
        xn = _rms(xn, fw_ref[...])
    o_ref[...] = xn


def _out_proj(a, b, ka, b_col, w, x, fw, final, tm):
    n, d = x.shape
    kb = w.shape[0] - ka
    return pl.pallas_call(
        functools.partial(_out_kernel, final=final),
        grid=(n // tm,),
        in_specs=[pl.BlockSpec((tm, ka), lambda i: (i, 0)),
                  pl.BlockSpec((tm, kb), lambda i: (i, b_col)),
                  pl.BlockSpec(w.shape, lambda i: (0, 0)),
                  pl.BlockSpec((tm, d), lambda i: (i, 0)),
                  pl.BlockSpec((1, d), lambda i: (0, 0))],
        out_specs=pl.BlockSpec((tm, d), lambda i: (i, 0)),
        out_shape=jax.ShapeDtypeStruct((n, d), F32),
        compiler_params=_cparams(("parallel",)), name="out_proj")(a, b, w, x, fw.reshape(1, d))


def _neumann(a_list, c):
    eye = jnp.where(_iota((c, c), 0) == _iota((c, c), 1), 1.0, 0.0)
    ts = [eye - a for a in a_list]
    abs_ = [a.astype(BF16) for a in a_list]
    ms = [_dot(ab, ab) for ab in abs_]
    levels = int(math.log2(c)) - 1
    for k in range(levels):
        mbs = [m.astype(BF16) for m in ms]
        ts = [t + _dot(mb, t.astype(BF16)) for mb, t in zip(mbs, ts)]
        if k + 1 < levels:
            ms = [_dot(mb, mb) for mb in mbs]
    return ts


def _gdn_kernel(x_ref, z_ref, gt_ref, conv0_ref, s0_ref, cw_ref, hp_ref, nw_ref, *rest,
                c, nch, tv, dk, unroll, per_seq, lay):
    o_ref, convo_ref, so_ref, xbuf, qkv_scr, g_scr, b_scr, s_scr = rest[-8:]
    gi = pl.program_id(1)

    @pl.when(gi == pl.num_programs(1) - 1)
    def _():
        for l in range(so_ref.shape[0]):
            if l != lay:
                so_ref[l] = jnp.zeros(so_ref.shape[1:], F32)

    tg = c * nch
    hq = N_HEADS * dk
    cw = cw_ref[...]

    def conv(base, rows):
        return _silu((xbuf[base + 5:base + 5 + rows, :] * cw[0:1, :] + xbuf[base + 6:base + 6 + rows, :] * cw[1:2, :])
                     + (xbuf[base + 7:base + 7 + rows, :] * cw[2:3, :] + xbuf[base + 8:base + 8 + rows, :] * cw[3:4, :]))

    if per_seq:
        for s in range(nch):
            base = s * (8 + c)
            xbuf[base:base + 8, :] = conv0_ref[s]
            xbuf[base + 8:base + 8 + c, :] = x_ref[s * c:(s + 1) * c, :]
            qkv_scr[s * c:(s + 1) * c, :] = conv(base, c)
            convo_ref[s, 0:8 - (CONV_W - 1), :] = jnp.zeros((8 - (CONV_W - 1), xbuf.shape[1]), F32)
            convo_ref[s, 8 - (CONV_W - 1):8, :] = xbuf[base + 8 + tv - (CONV_W - 1):base + 8 + tv, :]
    else:
        @pl.when(gi == 0)
        def _():
            xbuf[0:8, :] = conv0_ref[0]
            s_scr[...] = s0_ref[0]

        xbuf[8:8 + tg, :] = x_ref[...]
        qkv_scr[...] = conv(0, tg)
        xbuf[5:8, :] = xbuf[8 + tg - 3:8 + tg, :]

    gt = gt_ref[...]
    hp = hp_ref[...]
    lane = _iota((tg, 128), 1)
    beta = _sigmoid(gt)
    g = -jnp.exp(hp[0:1, :]) * _softplus(gt + hp[1:2, :])
    g = jnp.where(lane >= N_HEADS, jnp.where(lane < 2 * N_HEADS, g, 0.0), 0.0)
    if tv < c:
        valid = (_iota((tg, 128), 0) % c) < tv
        g = jnp.where(valid, g, 0.0)
        beta = jnp.where(valid, beta, 0.0)
    g_scr[...] = g
    b_scr[...] = beta

    row = _iota((c, c), 0)
    col = _iota((c, c), 1)
    ige = row >= col
    igt = row > col
    tri = jnp.where(ige, 1.0, 0.0).astype(BF16)
    lane_c = _iota((c, 128), 1)
    nw = nw_ref[...]

    sels = [jnp.where(lane_c == N_HEADS + h, 1.0, 0.0).astype(BF16) for h in range(N_HEADS)]

    def prep(starts):
        items = [(ci, h) for ci in range(len(starts)) for h in range(N_HEADS)]
        gcums = [_exact_dot(tri, g_scr[pl.ds(r0, c), :]) for r0 in starts]
        bchs = [b_scr[pl.ds(r0, c), :] for r0 in starts]
        qns, kns, vhs = [], [], []
        for ci, h in items:
            r0 = starts[ci]
            qh = qkv_scr[pl.ds(r0, c), h * dk:(h + 1) * dk]
            kh = qkv_scr[pl.ds(r0, c), hq + h * dk:hq + (h + 1) * dk]
            vhs.append(qkv_scr[pl.ds(r0, c), 2 * hq + h * dk:2 * hq + (h + 1) * dk])
            qns.append(qh * lax.rsqrt(jnp.sum(qh * qh, axis=-1, keepdims=True) + EPS) * (dk ** -0.5))
            kns.append(kh * lax.rsqrt(jnp.sum(kh * kh, axis=-1, keepdims=True) + EPS))
        qkks = [_dot_nt(jnp.concatenate([qn, kn], axis=0).astype(BF16), kn.astype(BF16))
                for qn, kn in zip(qns, kns)]
        grows = [_exact_dot_nt(sels[h], gcums[ci]) for ci, h in items]
        gcols = [gcums[ci][:, N_HEADS + h:N_HEADS + h + 1] for ci, h in items]
        bcols = [bchs[ci][:, h:h + 1] for ci, h in items]
        decays = [jnp.where(ige, jnp.exp(jnp.where(ige, gcol - grow, 0.0)), 0.0)
                  for gcol, grow in zip(gcols, grows)]
        t_invs = _neumann([jnp.where(igt, qkk[c:] * decay * bcol, 0.0)
                           for qkk, decay, bcol in zip(qkks, decays, bcols)], c)
        egs = [jnp.exp(gcol) for gcol in gcols]
        sols = [_dot(t_inv.astype(BF16),
                     jnp.concatenate([vh * bcol, kn * (bcol * eg)], axis=1).astype(BF16))
                for t_inv, vh, kn, bcol, eg in zip(t_invs, vhs, kns, bcols, egs)]
        glasts = [gcums[ci][c - 1:c, N_HEADS + h:N_HEADS + h + 1] for ci, h in items]
        kws = [(kn * jnp.exp(glast - gcol)).astype(BF16) for kn, glast, gcol in zip(kns, glasts, gcols)]
        aqks = [(qkk[:c] * decay).astype(BF16) for qkk, decay in zip(qkks, decays)]
        sol_bs = [sol.astype(BF16) for sol in sols]
        ktus = [_dot_tn(kw, sol_b) for kw, sol_b in zip(kws, sol_bs)]
        aus = [_dot(aqk, sol_b) for aqk, sol_b in zip(aqks, sol_bs)]
        out = []
        for i in range(len(items)):
            lhs = jnp.concatenate([-ktus[i][:, dk:], qns[i] * egs[i] - aus[i][:, dk:]], axis=0).astype(BF16)
            out.append((lhs, ktus[i][:, :dk], aus[i][:, :dk], jnp.exp(glasts[i])))
        return [out[ci * N_HEADS:(ci + 1) * N_HEADS] for ci in range(len(starts))]

    def chunks(starts, seq_ids):
        preps = prep(starts)
        for r0, sid, heads in zip(starts, seq_ids, preps):
            s_olds = [s_scr[h] if sid is None else s0_ref[sid, h] for h in range(N_HEADS)]
            res = [_dot(heads[h][0], s_olds[h].astype(BF16)) for h in range(N_HEADS)]
            for h in range(N_HEADS):
                s_new = s_olds[h] * heads[h][3] + (heads[h][1] + res[h][:dk])
                if sid is None:
                    s_scr[h] = s_new
                else:
                    so_ref[lay, sid, h] = s_new
            for h in range(N_HEADS):
                zz = z_ref[pl.ds(r0, c), h * dk:(h + 1) * dk]
                o = heads[h][2] + res[h][dk:]
                o_ref[pl.ds(r0, c), h * dk:(h + 1) * dk] = _rms(o, nw) * _silu(zz)

    if per_seq:
        chunks([s * c for s in range(nch)], list(range(nch)))
    else:
        def body(it, carry):
            chunks([pl.multiple_of((it * unroll + i) * c, c) for i in range(unroll)], [None] * unroll)
            return carry

        lax.fori_loop(0, nch // unroll, body, 0)

        @pl.when(gi == pl.num_programs(1) - 1)
        def _():
            convo_ref[0] = xbuf[0:8, :]
            so_ref[lay, 0] = s_scr[...]


def _gdn(qkvz, gates, conv0, s0, s_idx, conv_w, hp, nw, s_stack, out_idx, *, bn, c, nch, tv, per_seq=False):
    n = qkvz.shape[0]
    dk = s0.shape[-1]
    cc = 3 * N_HEADS * dk
    tg = c * nch
    sb = nch if per_seq else 1
    steps = 1 if per_seq else n // (bn * tg)
    create = isinstance(s_stack, int)
    held, first, lay = (s_stack, 0, out_idx) if create else (1, out_idx, 0)
    stack_shape = (s_stack, bn, N_HEADS, dk, dk) if create else s_stack.shape
    kern = functools.partial(_gdn_kernel, c=c, nch=nch, tv=tv, dk=dk, unroll=math.gcd(nch, 8), per_seq=per_seq,
                             lay=lay)
    return pl.pallas_call(
        kern, grid=(bn // sb, steps),
        in_specs=[pl.BlockSpec((tg, cc), lambda b, g: (b * steps + g, 0)),
                  pl.BlockSpec((tg, N_HEADS * dk), lambda b, g: (b * steps + g, 3)),
                  pl.BlockSpec((tg, 128), lambda b, g: (b * steps + g, 0)),
                  pl.BlockSpec((sb, 8, cc), lambda b, g: (b, 0, 0)),
                  pl.BlockSpec((None, sb, N_HEADS, dk, dk), lambda b, g: (s_idx, b, 0, 0, 0)),
                  pl.BlockSpec((CONV_W, cc), lambda b, g: (0, 0)),
                  pl.BlockSpec((8, 128), lambda b, g: (0, 0)),
                  pl.BlockSpec((1, dk), lambda b, g: (0, 0))] + (
                      [] if create else [pl.BlockSpec(memory_space=pl.ANY)]),
        out_specs=[pl.BlockSpec((tg, N_HEADS * dk), lambda b, g: (b * steps + g, 0)),
                   pl.BlockSpec((sb, 8, cc), lambda b, g: (b, 0, 0)),
                   pl.BlockSpec((held, sb, N_HEADS, dk, dk), lambda b, g: (first, b, 0, 0, 0))],
        out_shape=[jax.ShapeDtypeStruct((n, N_HEADS * dk), F32),
                   jax.ShapeDtypeStruct((bn, 8, cc), F32),
                   jax.ShapeDtypeStruct(stack_shape, F32)],
        input_output_aliases={} if create else {8: 2},
        scratch_shapes=[pltpu.VMEM((sb * 8 + tg, cc), F32), pltpu.VMEM((tg, cc), F32),
                        pltpu.VMEM((tg, 128), F32), pltpu.VMEM((tg, 128), F32),
                        pltpu.VMEM((N_HEADS, dk, dk), F32)],
        compiler_params=_cparams(("parallel", "arbitrary")), name="gated_delta")(
            qkvz, qkvz, gates, conv0, s0, conv_w, hp, nw.reshape(1, dk), *(() if create else (s_stack,)))


def _mlstm_kernel(q_ref, k_ref, v_ref, z_ref, og_ref, gt_ref, c0_ref, n0_ref, m0_ref, hp_ref, nw_ref, *rest,
                  c, nch, sb, tv, dqk, dv, lay):
    o_ref, co_ref, no_ref, mo_ref, li_scr, lf_scr, cext, m_scr = rest[-8:]
    gi = pl.program_id(1)
    tg = c * nch
    lane1 = _iota((dqk, 128), 1)
    eye = _iota((dqk, dqk), 0) == _iota((dqk, dqk), 1)

    @pl.when(gi == 0)
    def _():
        for s in range(sb):
            m_scr[s * 8:(s + 1) * 8, :] = jnp.zeros((8, 128), F32)
            for h in range(N_HEADS):
                cext[s * N_HEADS + h, :, 0:dv] = c0_ref[s, h]
                ncol = jnp.sum(jnp.where(eye, n0_ref[s, h:h + 1, :], 0.0), axis=1, keepdims=True)
                cext[s * N_HEADS + h, :, dv:dv + 128] = jnp.broadcast_to(ncol, (dqk, 128))
                m_scr[s * 8 + h:s * 8 + h + 1, :] = jnp.broadcast_to(m0_ref[s, 0:1, h:h + 1], (1, 128))

    hp = hp_ref[...]
    lane = _iota((tg, 128), 1)
    for s in range(sb):
        gt = gt_ref[s]
        li = gt + hp[0:1, :]
        x = gt + hp[1:2, :]
        lf = jnp.minimum(x, 0.0) - jnp.log(1.0 + jnp.exp(-jnp.abs(x)))
        lf = jnp.where(lane >= N_HEADS, jnp.where(lane < 2 * N_HEADS, lf, 0.0), 0.0)
        if tv < c:
            valid = (_iota((tg, 128), 0) % c) < tv
            lf = jnp.where(valid, lf, 0.0)
            li = jnp.where(valid, li, NEG)
        li_scr[s] = li
        lf_scr[s] = lf

    row = _iota((c, c), 0)
    col = _iota((c, c), 1)
    ige = row >= col
    tri = jnp.where(ige, 1.0, 0.0).astype(BF16)
    ones_b = jnp.ones((c, 128), BF16)
    lane_c = _iota((c, 128), 1)
    ones_blk = jnp.ones((c, 128), F32)
    nw = nw_ref[...]
    nrep = dv // 128 + 1

    def wide(x):
        return jnp.concatenate([x] * nrep, axis=1)

    def prep(clist):
        items = [(ci, h) for ci in range(len(clist)) for h in range(N_HEADS)]
        bcums = [_exact_dot(tri, lf_scr[s, pl.ds(r0, c), :]) for s, r0 in clist]
        lichs = [li_scr[s, pl.ds(r0, c), :] for s, r0 in clist]
        qbs = [(q_ref[clist[ci][0], pl.ds(clist[ci][1], c), h * dqk:(h + 1) * dqk] * (dqk ** -0.5)).astype(BF16)
               for ci, h in items]
        ks = [k_ref[clist[ci][0], pl.ds(clist[ci][1], c), h * dqk:(h + 1) * dqk] for ci, h in items]
        qk_raws = [_dot_nt(qb, k.astype(BF16)) for qb, k in zip(qbs, ks)]
        bcols = [jnp.broadcast_to(bcums[ci][:, N_HEADS + h:N_HEADS + h + 1], (c, 128)) for ci, h in items]
        licols = [jnp.broadcast_to(lichs[ci][:, h:h + 1], (c, 128)) for ci, h in items]
        rowvs = [_exact_dot_nt(ones_b, jnp.where(lane_c == 0, licol - bcol, 0.0))
                 for licol, bcol in zip(licols, bcols)]
        dmats = [jnp.where(ige, bcol[:, :c] + rowv, NEG) for bcol, rowv in zip(bcols, rowvs)]
        dmaxs = [jnp.broadcast_to(jnp.max(dmat, axis=-1, keepdims=True), (c, 128)) for dmat in dmats]
        vexts = [jnp.concatenate([v_ref[clist[ci][0], pl.ds(clist[ci][1], c), h * dv:(h + 1) * dv], ones_blk],
                                 axis=1).astype(BF16) for ci, h in items]
        intras = [_dot((qk_raw * jnp.exp(dmat - dmax[:, :c])).astype(BF16), vext)
                  for qk_raw, dmat, dmax, vext in zip(qk_raws, dmats, dmaxs, vexts)]
        blasts = [bcol[c - 1:c, :] for bcol in bcols]
        mus = [dmax[c - 1:c, :] for dmax in dmaxs]
        kvs = [_dot_tn((k * jnp.exp(blast - bcol + licol - mu)).astype(BF16), vext)
               for k, blast, bcol, licol, mu, vext in zip(ks, blasts, bcols, licols, mus, vexts)]
        out = [(qbs[i], bcols[i], dmaxs[i], intras[i], blasts[i], mus[i], kvs[i]) for i in range(len(items))]
        return [out[ci * N_HEADS:(ci + 1) * N_HEADS] for ci in range(len(clist))]

    def chunks(clist):
        hs = range(N_HEADS)
        for (s, r0), heads in zip(clist, prep(clist)):
            c_olds = [cext[s * N_HEADS + h] for h in hs]
            qcs = [_dot(heads[h][0], c_olds[h].astype(BF16)) for h in hs]
            for h in hs:
                qb, bcol, dmax, intra, blast, mu, kv = heads[h]
                mprev = m_scr[s * 8 + h:s * 8 + h + 1, :]
                inter = bcol + mprev
                mt = jnp.maximum(inter, dmax)
                mnew = jnp.maximum(blast + mprev, mu)
                cext[s * N_HEADS + h] = (c_olds[h] * wide(jnp.exp(blast + mprev - mnew))
                                         + wide(jnp.exp(mu - mnew)) * kv)
                m_scr[s * 8 + h:s * 8 + h + 1, :] = mnew
                tot = wide(jnp.exp(inter - mt)) * qcs[h] + wide(jnp.exp(dmax - mt)) * intra
                den = jnp.maximum(jnp.abs(tot[:, dv:dv + 128]), jnp.exp(-mt))
                hh = tot[:, :dv] / jnp.concatenate([den] * (dv // 128), axis=1)
                og = og_ref[s, pl.ds(r0, c), h * dv:(h + 1) * dv]
                zz = z_ref[s, pl.ds(r0, c), h * dv:(h + 1) * dv]
                o_ref[s, pl.ds(r0, c), h * dv:(h + 1) * dv] = _rms(_sigmoid(og) * hh, nw) * _silu(zz)

    chunks([(s, i * c) for i in range(nch) for s in range(sb)])

    @pl.when(gi == pl.num_programs(1) - 1)
    def _():
        for l in range(co_ref.shape[0]):
            if l != lay:
                co_ref[l] = jnp.zeros(co_ref.shape[1:], F32)
        for s in range(sb):
            for h in range(N_HEADS):
                co_ref[lay, s, h] = cext[s * N_HEADS + h, :, 0:dv]
                ncol = jnp.broadcast_to(cext[s * N_HEADS + h, :, dv:dv + 1], (dqk, dqk))
                no_ref[s, h:h + 1, :] = jnp.sum(jnp.where(eye, ncol, 0.0), axis=0, keepdims=True)
            mrows = m_scr[s * 8:(s + 1) * 8, :]
            diag = jnp.where(_iota((8, 128), 0) == _iota((8, 128), 1), mrows, 0.0)
            mo_ref[s] = jnp.broadcast_to(jnp.sum(diag, axis=0, keepdims=True), (8, 128))


def _mlstm(q, k, v, z, og, gates, c0, c_idx, n0, m0, hp, nw, c_stack, out_idx, *, bn, c, nch, sb, tv):
    n = q.shape[0]
    dqk, dv = c0.shape[-2], c0.shape[-1]
    tg = c * nch
    rows = n // bn
    steps = rows // tg
    hq, hv = N_HEADS * dqk, N_HEADS * dv
    row = lambda b, g: (b, g, 0)
    q, k, v, z, og, gates = (a.reshape(bn, rows, a.shape[1]) for a in (q, k, v, z, og, gates))
    create = isinstance(c_stack, int)
    held, first, lay = (c_stack, 0, out_idx) if create else (1, out_idx, 0)
    stack_shape = (c_stack, bn, N_HEADS, dqk, dv) if create else c_stack.shape
    kern = functools.partial(_mlstm_kernel, c=c, nch=nch, sb=sb, tv=tv, dqk=dqk, dv=dv, lay=lay)
    out = pl.pallas_call(
        kern, grid=(bn // sb, steps),
        in_specs=[pl.BlockSpec((sb, tg, hq), row), pl.BlockSpec((sb, tg, hq), row),
                  pl.BlockSpec((sb, tg, hv), row), pl.BlockSpec((sb, tg, hv), row),
                  pl.BlockSpec((sb, tg, hv), row), pl.BlockSpec((sb, tg, 128), row),
                  pl.BlockSpec((None, sb, N_HEADS, dqk, dv), lambda b, g: (c_idx, b, 0, 0, 0)),
                  pl.BlockSpec((sb, N_HEADS, dqk), lambda b, g: (b, 0, 0)),
                  pl.BlockSpec((sb, 8, 128), lambda b, g: (b, 0, 0)),
                  pl.BlockSpec((8, 128), lambda b, g: (0, 0)),
                  pl.BlockSpec((1, dv), lambda b, g: (0, 0))] + (
                      [] if create else [pl.BlockSpec(memory_space=pl.ANY)]),
        out_specs=[pl.BlockSpec((sb, tg, hv), row),
                   pl.BlockSpec((held, sb, N_HEADS, dqk, dv), lambda b, g: (first, b, 0, 0, 0)),
                   pl.BlockSpec((sb, N_HEADS, dqk), lambda b, g: (b, 0, 0)),
                   pl.BlockSpec((sb, 8, 128), lambda b, g: (b, 0, 0))],
        out_shape=[jax.ShapeDtypeStruct((bn, rows, hv), F32),
                   jax.ShapeDtypeStruct(stack_shape, F32),
                   jax.ShapeDtypeStruct((bn, N_HEADS, dqk), F32),
                   jax.ShapeDtypeStruct((bn, 8, 128), F32)],
        input_output_aliases={} if create else {11: 1},
        scratch_shapes=[pltpu.VMEM((sb, tg, 128), F32), pltpu.VMEM((sb, tg, 128), F32),
                        pltpu.VMEM((sb * N_HEADS, dqk, dv + 128), F32), pltpu.VMEM((sb * 8, 128), F32)],
        compiler_params=_cparams(("parallel", "arbitrary")), name="mlstm")(
            q, k, v, z, og, gates, c0, n0, m0, hp, nw.reshape(1, dv), *(() if create else (c_stack,)))
    return (out[0].reshape(n, hv),) + tuple(out[1:])


def _lambda(lq, lam_init):
    a = jnp.sum(lq[0:1, :] * lq[1:2, :], axis=-1, keepdims=True)
    b = jnp.sum(lq[2:3, :] * lq[3:4, :], axis=-1, keepdims=True)
    return jnp.exp(a) - jnp.exp(b) + lam_init


def _attn_kernel(wq_ref, k_ref, vt_ref, z_ref, lq_ref, nw_ref, o_ref, acc_scr, *, t, hpb, lam_init):
    qi = pl.program_id(2)
    dh = k_ref.shape[1] // hpb
    nl = 4 * t
    acc_scr[...] = jnp.zeros(acc_scr.shape, F32)

    def step(jp, carry, masked):
        rows = [pl.multiple_of((2 * jp + i) * t, t) for i in range(2)]
        wqs = [jnp.concatenate([wq_ref[0, a], wq_ref[1, a]], axis=1) for a in range(hpb)]
        scores = [[_dot(k_ref[pl.ds(rows[i], t), a * dh:(a + 1) * dh], wqs[a]) for a in range(hpb)]
                  for i in range(2)]
        out = list(carry)
        for i in range(2):
            for a in range(hpb):
                m_prev, l_prev = out[a]
                s = scores[i][a]
                if masked:
                    lane = _iota((t, nl), 1)
                    qpos = (lane // (2 * t)) * t + lane % t
                    s = jnp.where(_iota((t, nl), 0) + i * t <= qpos, s, NEG)
                m_new = jnp.maximum(m_prev, jnp.max(s, axis=0, keepdims=True))
                alpha = jnp.exp2(m_prev - m_new)
                p = jnp.exp2(s - m_new)
                l_new = alpha * l_prev + jnp.sum(p, axis=0, keepdims=True)
                acc_scr[a] = alpha * acc_scr[a] + _dot(vt_ref[2 * jp + i, a * dh:(a + 1) * dh, :], p.astype(BF16))
                out[a] = (m_new, l_new)
        return tuple(out)

    init = tuple((jnp.full((1, nl), NEG, F32), jnp.zeros((1, nl), F32)) for _ in range(hpb))
    carry = lax.fori_loop(0, qi, lambda j, c: step(j, c, False), init)
    carry = step(qi, carry, True)

    lam = _lambda(lq_ref[...], lam_init)
    nw = nw_ref[...]
    for a in range(hpb):
        acc = acc_scr[a] * (1.0 / carry[a][1])
        for qb in range(2):
            b0 = 2 * t * qb
            o = (acc[:, b0:b0 + t] - lam * acc[:, b0 + t:b0 + 2 * t]).T
            zz = z_ref[qb * t:(qb + 1) * t, a * dh:(a + 1) * dh]
            o_ref[qb * t:(qb + 1) * t, a * dh:(a + 1) * dh] = _rms(o, nw) * (1.0 - lam_init) * _silu(zz)


def _attn_prompt(wq, k16, vt, z, lq, nw, *, bn, seq, lam_init):
    n = k16.shape[0]
    t = vt.shape[2]
    dh = wq.shape[2]
    hpb = 4
    nq = seq // t
    nq2 = nq // 2
    kern = functools.partial(_attn_kernel, t=t, hpb=hpb, lam_init=lam_init)
    return pl.pallas_call(
        kern, grid=(bn, N_HEADS // hpb, nq2),
        in_specs=[pl.BlockSpec((2, hpb, dh, 2 * t), lambda b, h, i: (b * nq2 + i, h, 0, 0)),
                  pl.BlockSpec((seq, hpb * dh), lambda b, h, i: (b, h), pipeline_mode=pl.Buffered(1)),
                  pl.BlockSpec((nq, hpb * dh, t), lambda b, h, i: (b, h, 0), pipeline_mode=pl.Buffered(1)),
                  pl.BlockSpec((2 * t, hpb * dh), lambda b, h, i: (b * nq2 + i, h)),
                  pl.BlockSpec(lq.shape, lambda b, h, i: (0, 0)),
                  pl.BlockSpec((1, dh), lambda b, h, i: (0, 0))],
        out_specs=pl.BlockSpec((2 * t, hpb * dh), lambda b, h, i: (b * nq2 + i, h)),
        out_shape=jax.ShapeDtypeStruct((n, N_HEADS * dh), F32),
        scratch_shapes=[pltpu.VMEM((hpb, dh, 4 * t), F32)],
        compiler_params=_cparams(("parallel", "parallel", "arbitrary")), name="diff_attn_prompt")(
            wq, k16, vt, z, lq, nw.reshape(1, dh))


def _attn_s_kernel(pt_ref, q_ref, kn_ref, vn_ref, z_ref, lq_ref, nw_ref, *rest, pg, tv, lam_init):
    k_pages = rest[:pg]
    v_pages = rest[pg:2 * pg]
    o_ref, qbd, bias, m_scr, l_scr, acc_scr = rest[2 * pg:]
    j = pl.program_id(1)
    tp, hd = q_ref.shape
    dh = hd // N_HEADS
    dqk = dh // 2
    prow = k_pages[0].shape[0]
    rph = 2 * tp

    @pl.when(j == 0)
    def _():
        q = q_ref[...] * (dqk ** -0.5)
        lane = _iota((tp, dh), 1)
        for h in range(N_HEADS):
            qh = q[:, h * dh:(h + 1) * dh]
            qbd[h * rph:h * rph + tp, :] = jnp.where(lane < dqk, qh, 0.0)
            qbd[h * rph + tp:(h + 1) * rph, :] = jnp.where(lane >= dqk, qh, 0.0)
        rhead = _iota(bias.shape, 0) // rph
        chead = _iota(bias.shape, 1) % N_HEADS
        bias[...] = jnp.where(rhead == chead, 0.0, NEG)
        m_scr[...] = jnp.full(m_scr.shape, NEG, F32)
        l_scr[...] = jnp.zeros(l_scr.shape, F32)
        acc_scr[...] = jnp.zeros(acc_scr.shape, F32)

    def update(s, pv):
        m_prev = m_scr[...]
        m_new = jnp.maximum(m_prev, jnp.max(s, axis=-1, keepdims=True))
        alpha = jnp.exp(m_prev - m_new)
        p = jnp.exp(s - m_new[:, 0:1])
        l_scr[...] = alpha * l_scr[...] + jnp.sum(p, axis=-1, keepdims=True)
        acc_scr[...] = alpha * acc_scr[...] + pv(p)
        m_scr[...] = m_new

    qb = qbd[...].astype(BF16)
    bs_ = bias[...]
    s = jnp.concatenate([_dot_nt(qb, kp[...].astype(BF16)) + bs_ for kp in k_pages], axis=1)

    def pv_pages(p):
        pb = p.astype(BF16)
        acc = _dot(pb[:, 0:prow], v_pages[0][...].astype(BF16))
        for i in range(1, pg):
            acc = acc + _dot(pb[:, i * prow:(i + 1) * prow], v_pages[i][...].astype(BF16))
        return acc

    update(s, pv_pages)

    @pl.when(j == pl.num_programs(1) - 1)
    def _():
        qf = qbd[...]
        kn = kn_ref[...]
        vn = vn_ref[...]
        sn = jnp.concatenate([_dot_nt(qf[h * rph:(h + 1) * rph], kn[:, h * dh:(h + 1) * dh])
                              for h in range(N_HEADS)], axis=0)
        tq = _iota(sn.shape, 0) % tp
        tk = _iota(sn.shape, 1)
        sn = jnp.where(tk <= tq, jnp.where(tk < tv, sn, NEG), NEG)
        update(sn, lambda p: jnp.concatenate(
            [_dot(p[h * rph:(h + 1) * rph], vn[:, h * dh:(h + 1) * dh]) for h in range(N_HEADS)], axis=0))
        acc = acc_scr[...] / l_scr[...]
        lam = _lambda(lq_ref[...], lam_init)
        nw = nw_ref[...]
        normed = jnp.concatenate(
            [_rms(acc[h * rph:h * rph + tp] - lam * acc[h * rph + tp:(h + 1) * rph], nw) for h in range(N_HEADS)],
            axis=1)
        o_ref[...] = normed * (1.0 - lam_init) * _silu(z_ref[...])


def _attn_sample(page_table, q, kn, vn, z, lq, nw, cache_k, cache_v, *, e, tv, lam_init):
    bs, npages = page_table.shape
    n, hd = q.shape
    tp = n // bs
    dh = hd // N_HEADS
    pg = math.gcd(npages, PAGES_PER_STEP)
    prow = cache_k.shape[2]
    steps = npages // pg
    rows = 2 * N_HEADS * tp
    tok = lambda b, j, pt: (b, 0)
    page_specs = [pl.BlockSpec((None, None, prow, dh),
                               lambda b, j, pt, i=i: (e, pt[b, j * pg + i], 0, 0)) for i in range(pg)]
    kern = functools.partial(_attn_s_kernel, pg=pg, tv=tv, lam_init=lam_init)
    grid_spec = pltpu.PrefetchScalarGridSpec(
        num_scalar_prefetch=1, grid=(bs, steps),
        in_specs=[pl.BlockSpec((tp, hd), tok), pl.BlockSpec((tp, hd), tok), pl.BlockSpec((tp, hd), tok),
                  pl.BlockSpec((tp, hd), tok),
                  pl.BlockSpec(lq.shape, lambda b, j, pt: (0, 0)),
                  pl.BlockSpec((1, dh), lambda b, j, pt: (0, 0))] + page_specs + page_specs,
        out_specs=pl.BlockSpec((tp, hd), tok),
        scratch_shapes=[pltpu.VMEM((rows, dh), F32), pltpu.VMEM((rows, prow), F32), pltpu.VMEM((rows, 128), F32),
                        pltpu.VMEM((rows, 128), F32), pltpu.VMEM((rows, dh), F32)])
    return pl.pallas_call(
        kern, grid_spec=grid_spec, out_shape=jax.ShapeDtypeStruct((n, hd), F32),
        compiler_params=_cparams(("parallel", "arbitrary")), name="diff_attn_sample")(
            page_table, q, kn, vn, z, lq, nw.reshape(1, dh), *([cache_k] * pg), *([cache_v] * pg))


def _rope_tables(pos, dqk):
    rope_dim = dqk // 4
    half = rope_dim // 2
    inv = jnp.power(ROPE_THETA, -jnp.arange(half, dtype=F32) / half)
    ang = pos.astype(F32)[:, None] * inv[None, :]
    lane = jnp.arange(128) % dqk
    idx = lane % half
    cos = jnp.where(lane < rope_dim, jnp.cos(ang)[:, idx], 1.0)
    sin = jnp.sin(ang)[:, idx]
    sa = jnp.where((lane >= half) & (lane < rope_dim), sin, 0.0)
    sb = jnp.where(lane < half, -sin, 0.0)
    return cos.astype(F32), sa.astype(F32), sb.astype(F32)


def _lane_row(vals, offset):
    out = jnp.zeros((128,), F32)
    return out.at[offset:offset + vals.shape[0]].set(vals.astype(F32))


def kernel(x_prompt, x_sample, cache_k, cache_v, page_table, state_gdn_conv, state_gdn_s, state_mlstm_c,
           state_mlstm_n, state_mlstm_m, norm_w, final_norm_w, w_in_even, w_out_even, conv_w, a_log, dt_bias,
           gdn_norm_w, lambda_qk, diff_norm_w, w_in_odd, w_out_odd, b_i, b_f, mlstm_norm_w):
    bp, seq, d = x_prompt.shape
    bs, ts, _ = x_sample.shape
    depth = norm_w.shape[0]
    tp = SAMPLE_PAD
    dk_a = state_gdn_s.shape[-1]
    conv_ch = state_gdn_conv.shape[-1]
    w_a = N_HEADS * dk_a
    dh_b = cache_v.shape[-1]
    dqk_b = dh_b // 2
    w_b = N_HEADS * dh_b
    dqk_c, dv_c = state_mlstm_c.shape[-2], state_mlstm_c.shape[-1]
    hq_c, w_c = N_HEADS * dqk_c, N_HEADS * dv_c
    past = page_table.shape[1] * cache_k.shape[2]
    n_pool, page = cache_k.shape[1], cache_k.shape[2]

    xp = x_prompt.reshape(bp * seq, d)
    xs = jnp.pad(x_sample, ((0, 0), (0, tp - ts), (0, 0))).reshape(bs * tp, d)
    tm_p = math.gcd(seq, 256)
    tm_s = math.gcd(bs * tp, 256)
    nch_p = math.gcd(seq // CHUNK, 8)
    seq_s = math.gcd(bs, 8)
    seq_c = math.gcd(bs, 8)

    rope_p = _rope_tables(jnp.arange(seq), dqk_b)
    rope_s = _rope_tables(past + (jnp.arange(tm_s) % tp), dqk_b)
    half = dqk_b // 8
    ang_t = (jnp.power(ROPE_THETA, -jnp.arange(half, dtype=F32) / half)[:, None]
             * jnp.arange(seq).astype(F32)[None, :])
    cos_t, sin_t = jnp.cos(ang_t), jnp.sin(ang_t)

    even_w = conv_ch + w_a
    seg_even = [(0, even_w), (even_w, w_b), (even_w + w_b, w_b), (even_w + 2 * w_b, w_b),
                (even_w + 3 * w_b, w_b), (even_w + 4 * w_b, 128)]
    seg_even_p = [seg_even[0]] + seg_even[2:]
    rope_even = tuple(range(even_w // 128, (even_w + 2 * w_b) // 128))
    odd_w = 2 * hq_c + 3 * w_c
    seg_odd = [(0, hq_c), (hq_c, hq_c), (2 * hq_c, w_c), (2 * hq_c + w_c, w_c), (2 * hq_c + 2 * w_c, w_c),
               (odd_w, 128)]

    ck = cache_k.reshape(cache_k.shape[0], n_pool, page * N_HEADS, dh_b)
    cv = cache_v.reshape(cache_v.shape[0], n_pool, page * N_HEADS, dh_b)

    n_even = (depth + 1) // 2
    krows_p = vrows_p = n_even
    k_rows_s, v_rows_s = [], []
    n_odd = depth // 2
    s_p = s_s = n_even
    c_p = c_s = n_odd
    conv_p, conv_s = [], []
    n_p, m_p, n_s, m_s = [], [], [], []

    pend_p = pend_s = None
    for layer in range(depth):
        if layer % 2 == 0:
            e = layer // 2
            lam_init = 0.8 - 0.6 * math.exp(-0.3 * layer)
            w = w_in_even[e]
            g0 = even_w
            w_r = jnp.concatenate([w[:, :g0], w[:, g0 + 2 * N_HEADS:], w[:, g0:g0 + 2 * N_HEADS],
                                   jnp.zeros((d, 128 - 2 * N_HEADS), F32)], axis=1).astype(BF16)
            w_o = w_out_even[e].astype(BF16)
            hp = jnp.zeros((8, 128), F32).at[0].set(_lane_row(a_log[e], N_HEADS)).at[1].set(
                _lane_row(dt_bias[e], N_HEADS))
            lq = lambda_qk[e]

            wq_t = w[:, g0 + 2 * N_HEADS:g0 + 2 * N_HEADS + w_b].T.astype(BF16)
            wv_t = w[:, g0 + 2 * N_HEADS + 2 * w_b:g0 + 2 * N_HEADS + 3 * w_b].T.astype(BF16)
            outs = _proj(xp, norm_w[layer], w_r, seg_even_p, (rope_even, rope_p), tm_p, k16_seg=1,
                         feat=(wq_t, wv_t, cos_t, sin_t), rows=((1, 2), e, (krows_p, vrows_p)), pre=pend_p)
            if pend_p is not None:
                xp = outs.pop()
            az, krows_p, vrows_p, zb, gt, k16, wq, vt = outs
            oa, cst, s_p = _gdn(az, gt, jnp.zeros((bp, 8, conv_ch), F32),
                                jnp.zeros((1, bp, N_HEADS, dk_a, dk_a), F32), 0,
                                conv_w[e], hp, gdn_norm_w[e], s_p, e, bn=bp, c=CHUNK, nch=nch_p, tv=CHUNK)
            ob = _attn_prompt(wq, k16, vt, zb, lq, diff_norm_w[e], bn=bp, seq=seq, lam_init=lam_init)
            pend_p = (oa, ob, w_a, 0, w_o)
            conv_p.append(cst[:, 8 - (CONV_W - 1):])

            outs = _proj(xs, norm_w[layer], w_r, seg_even, (rope_even, rope_s), tm_s, pre=pend_s)
            if pend_s is not None:
                xs = outs.pop()
            az, qb, kb, vb, zb, gt = outs
            conv0 = jnp.pad(state_gdn_conv[e], ((0, 0), (8 - (CONV_W - 1), 0), (0, 0)))
            oa, cst, s_s = _gdn(az, gt, conv0, state_gdn_s, e, conv_w[e], hp, gdn_norm_w[e], s_s, e,
                                bn=bs, c=tp, nch=seq_s, tv=ts, per_seq=True)
            ob = _attn_sample(page_table, qb, kb, vb, zb, lq, diff_norm_w[e], ck, cv, e=e, tv=ts,
                              lam_init=lam_init)
            pend_s = (oa, ob, w_a, 0, w_o)
            k_rows_s.append(kb.reshape(bs, tp, N_HEADS, dh_b)[:, :ts])
            v_rows_s.append(vb.reshape(bs, tp, N_HEADS, dh_b)[:, :ts])
            conv_s.append(cst[:, 8 - (CONV_W - 1):])
        else:
            o = layer // 2
            w_r = jnp.concatenate([w_in_odd[o], jnp.zeros((d, 128 - 2 * N_HEADS), F32)], axis=1).astype(BF16)
            w_o = w_out_odd[o].astype(BF16)
            hp = jnp.zeros((8, 128), F32).at[0].set(_lane_row(b_i[o], 0)).at[1].set(_lane_row(b_f[o], N_HEADS))

            outs = _proj(xp, norm_w[layer], w_r, seg_odd, None, tm_p, pre=pend_p)
            if pend_p is not None:
                xp = outs.pop()
            q, k, v, z, og, gt = outs
            hm, c_p, nf, mf = _mlstm(q, k, v, z, og, gt, jnp.zeros((1, bp, N_HEADS, dqk_c, dv_c), F32), 0,
                                     jnp.zeros((bp, N_HEADS, dqk_c), F32), jnp.zeros((bp, 8, 128), F32),
                                     hp, mlstm_norm_w[o], c_p, o, bn=bp, c=CHUNK, nch=math.gcd(seq // CHUNK, 8),
                                     sb=1, tv=CHUNK)
            pend_p = (hm, hm, w_c // 2, 1, w_o)
            n_p.append(nf)
            m_p.append(mf[:, 0, :N_HEADS])

            outs = _proj(xs, norm_w[layer], w_r, seg_odd, None, tm_s, pre=pend_s)
            if pend_s is not None:
                xs = outs.pop()
            q, k, v, z, og, gt = outs
            m0 = jnp.zeros((bs, 8, 128), F32).at[:, 0, :N_HEADS].set(state_mlstm_m[o])
            hm, c_s, nf, mf = _mlstm(q, k, v, z, og, gt, state_mlstm_c, o, state_mlstm_n[o], m0,
                                     hp, mlstm_norm_w[o], c_s, o, bn=bs, c=tp, nch=1, sb=seq_c, tv=ts)
            pend_s = (hm, hm, w_c // 2, 1, w_o)
            n_s.append(nf)
            m_s.append(mf[:, 0, :N_HEADS])

    xp = _out_proj(*pend_p, xp, final_norm_w, True, tm_p)
    xs = _out_proj(*pend_s, xs, final_norm_w, True, tm_s)
    y_prompt = xp.reshape(bp, seq, d)
    y_sample = xs.reshape(bs, tp, d)[:, :ts]
    st = jnp.stack
    kv_shape = (n_even, bp, seq, N_HEADS, dh_b)
    return (y_prompt, y_sample, krows_p.reshape(kv_shape), vrows_p.reshape(kv_shape), st(k_rows_s), st(v_rows_s),
            st(conv_p), s_p, st(conv_s), s_s,
            c_p, st(n_p), st(m_p), c_s, st(n_s), st(m_s))
```

```python
import functools
import math

import jax
import jax.numpy as jnp
from jax import lax
from jax.experimental import pallas as pl
from jax.experimental.pallas import tpu as pltpu

F32 = jnp.float32
BF16 = jnp.bfloat16

EPS = 1e-6
NEG = -1e30
N_HEADS = 4
CONV_W = 4
ROPE_THETA = 500000.0
CHUNK = 64
SAMPLE_PAD = 8
PAGES_PER_STEP = 16
V7X_VMEM_LIMIT = 56 * 1024 * 1024


def _cparams(sem):
    return pltpu.CompilerParams(dimension_semantics=sem, vmem_limit_bytes=V7X_VMEM_LIMIT)


def _dot(a, b):
    return jnp.dot(a, b, preferred_element_type=F32)


def _dot_nt(a, b):
    return lax.dot_general(a, b, (((1,), (1,)), ((), ())), preferred_element_type=F32)


def _dot_tn(a, b):
    return lax.dot_general(a, b, (((0,), (0,)), ((), ())), preferred_element_type=F32)


def _split3(x):
    hi = x.astype(BF16)
    r = x - hi.astype(F32)
    mid = r.astype(BF16)
    lo = (r - mid.astype(F32)).astype(BF16)
    return hi, mid, lo


def _exact_dot(a_bf16, x):
    hi, mid, lo = _split3(x)
    return _dot(a_bf16, hi) + (_dot(a_bf16, mid) + _dot(a_bf16, lo))


def _exact_dot_nt(a_bf16, x):
    hi, mid, lo = _split3(x)
    return _dot_nt(a_bf16, hi) + (_dot_nt(a_bf16, mid) + _dot_nt(a_bf16, lo))


def _sigmoid(x):
    return 1.0 / (1.0 + jnp.exp(-x))


def _silu(x):
    return x * _sigmoid(x)


def _softplus(x):
    return jnp.maximum(x, 0.0) + jnp.log(1.0 + jnp.exp(-jnp.abs(x)))


def _rms(x, w):
    return x * lax.rsqrt(jnp.mean(x * x, axis=-1, keepdims=True) + EPS) * w


def _iota(shape, dim):
    return lax.broadcasted_iota(jnp.int32, shape, dim)


def _proj_kernel(*refs, segs, rope_blocks, k16_seg, feat_major, row_segs, row_lay, n_alias, fuse_out):
    n_out = len(segs)
    pos = 3
    x_ref, nw_ref, w_ref = refs[:3]
    if rope_blocks:
        cos_ref, sa_ref, sb_ref = refs[pos:pos + 3]
        pos += 3
    if feat_major:
        wqt_ref, wvt_ref, cost_ref, sint_ref = refs[pos:pos + 4]
        pos += 4
    if fuse_out:
        a_ref, b_ref, wo_ref = refs[pos:pos + 3]
        pos += 3
    pos += n_alias
    outs = refs[pos:pos + n_out]
    pos += n_out
    x = x_ref[...]
    if fuse_out:
        ka = a_ref.shape[1]
        x = x + (_dot(a_ref[...].astype(BF16), wo_ref[0:ka, :]) + _dot(b_ref[...].astype(BF16), wo_ref[ka:, :]))
        refs[-1][...] = x
    h = _rms(x, nw_ref[...]).astype(BF16)
    for si, ((c0, width), o_ref) in enumerate(zip(segs, outs)):
        for s0 in range(0, width, 512):
            sw = min(512, width - s0)
            acc = _dot(h, w_ref[:, c0 + s0:c0 + s0 + sw])
            if rope_blocks and (c0 + s0) // 128 in rope_blocks:
                cos = cos_ref[...]
                sa = sa_ref[...]
                sb = sb_ref[...]
                parts = []
                for t0 in range(0, sw, 128):
                    a = acc[:, t0:t0 + 128]
                    parts.append(a * cos + pltpu.roll(a, 8, 1) * sa + pltpu.roll(a, 120, 1) * sb)
                acc = jnp.concatenate(parts, axis=1)
            if si in row_segs:
                dh = sw // N_HEADS
                for hd in range(N_HEADS):
                    o_ref[row_lay, pl.ds(hd, acc.shape[0], stride=N_HEADS), :] = acc[:, hd * dh:(hd + 1) * dh]
                for l in range(o_ref.shape[0]):
                    if l != row_lay:
                        o_ref[l] = jnp.zeros(o_ref.shape[1:], F32)
            else:
                o_ref[:, s0:s0 + sw] = acc
            if si == k16_seg:
                refs[pos][:, s0:s0 + sw] = acc.astype(BF16)
    if feat_major:
        wq_ref, vt_ref = refs[pos + 1], refs[pos + 2]
        tm = x.shape[0]
        vt_ref[0] = _dot_nt(wvt_ref[...], h).astype(BF16)
        qt = _dot_nt(wqt_ref[...], h)
        dqk = qt.shape[0] // (2 * N_HEADS)
        half = dqk // 8
        cos = cost_ref[...]
        sin = sint_ref[...]
        scale = (dqk ** -0.5) * math.log2(math.e)
        zero = jnp.zeros((dqk, tm), F32)
        for hd in range(N_HEADS):
            maps = []
            for m in range(2):
                r0 = (2 * hd + m) * dqk
                x1 = qt[r0:r0 + half]
                x2 = qt[r0 + half:r0 + 2 * half]
                maps.append(jnp.concatenate([x1 * cos - x2 * sin, x2 * cos + x1 * sin,
                                             qt[r0 + 2 * half:r0 + dqk]], axis=0) * scale)
            top = jnp.concatenate([maps[0], zero], axis=1)
            bot = jnp.concatenate([zero, maps[1]], axis=1)
            wq_ref[0, hd] = jnp.concatenate([top, bot], axis=0).astype(BF16)


def _proj(x, nw, w, segs, rope, tm, k16_seg=None, feat=None, rows=None, pre=None):
    n, d = x.shape
    ncols = w.shape[1]
    grid = (n // tm,)
    in_specs = [pl.BlockSpec((tm, d), lambda i: (i, 0)),
                pl.BlockSpec((1, d), lambda i: (0, 0)),
                pl.BlockSpec((d, ncols), lambda i: (0, 0))]
    args = [x, nw.reshape(1, d), w]
    rope_blocks = ()
    if rope is not None:
        rope_blocks, tables = rope
        nt = tables[0].shape[0] // tm
        for t in tables:
            in_specs.append(pl.BlockSpec((tm, 128), lambda i, nt=nt: (i % nt, 0)))
            args.append(t)
    out_shape = [jax.ShapeDtypeStruct((n, width), F32) for _, width in segs]
    out_specs = [pl.BlockSpec((tm, width), lambda i: (i, 0)) for _, width in segs]
    if feat is not None:
        wqt, wvt, cost, sint = feat
        ntt = cost.shape[1] // tm
        in_specs += [pl.BlockSpec(wqt.shape, lambda i: (0, 0)), pl.BlockSpec(wvt.shape, lambda i: (0, 0)),
                     pl.BlockSpec((cost.shape[0], tm), lambda i, ntt=ntt: (0, i % ntt)),
                     pl.BlockSpec((sint.shape[0], tm), lambda i, ntt=ntt: (0, i % ntt))]
        args += [wqt, wvt, cost, sint]
    if pre is not None:
        a, b, ka, b_col, w_out = pre
        kb = w_out.shape[0] - ka
        in_specs += [pl.BlockSpec((tm, ka), lambda i: (i, 0)), pl.BlockSpec((tm, kb), lambda i: (i, b_col)),
                     pl.BlockSpec(w_out.shape, lambda i: (0, 0))]
        args += [a, b, w_out]
    row_segs, aliases, row_lay = (), {}, 0
    if rows is not None:
        row_segs, layer, stacked = rows
        for si, arr in zip(row_segs, stacked):
            dh = segs[si][1] // N_HEADS
            if isinstance(arr, int):
                held, first, row_lay, shape = arr, 0, layer, (arr, n * N_HEADS, dh)
            else:
                held, first, shape = 1, layer, arr.shape
                aliases[len(args)] = si
                in_specs.append(pl.BlockSpec(memory_space=pl.ANY))
                args.append(arr)
            out_shape[si] = jax.ShapeDtypeStruct(shape, F32)
            out_specs[si] = pl.BlockSpec((held, tm * N_HEADS, dh), lambda i, first=first: (first, i, 0))
    if k16_seg is not None:
        kw = segs[k16_seg][1]
        out_shape.append(jax.ShapeDtypeStruct((n, kw), BF16))
        out_specs.append(pl.BlockSpec((tm, kw), lambda i: (i, 0)))
    if feat is not None:
        hq, hv = wqt.shape[0], wvt.shape[0]
        dh = hq // N_HEADS
        out_shape += [jax.ShapeDtypeStruct((n // tm, N_HEADS, dh, 2 * tm), BF16),
                      jax.ShapeDtypeStruct((n // tm, hv, tm), BF16)]
        out_specs += [pl.BlockSpec((1, N_HEADS, dh, 2 * tm), lambda i: (i, 0, 0, 0)),
                      pl.BlockSpec((1, hv, tm), lambda i: (i, 0, 0))]
    if pre is not None:
        out_shape.append(jax.ShapeDtypeStruct((n, d), F32))
        out_specs.append(pl.BlockSpec((tm, d), lambda i: (i, 0)))
    return list(pl.pallas_call(
        functools.partial(_proj_kernel, segs=tuple(segs), rope_blocks=tuple(rope_blocks), k16_seg=k16_seg,
                          feat_major=feat is not None, row_segs=tuple(row_segs), row_lay=row_lay,
                          n_alias=len(aliases), fuse_out=pre is not None),
        grid=grid, in_specs=in_specs, out_specs=out_specs, out_shape=out_shape, input_output_aliases=aliases,
        compiler_params=_cparams(("parallel",)), name="norm_proj")(*args))


def _out_kernel(a_ref, b_ref, w_ref, x_ref, fw_ref, o_ref, *, final):
    ka = a_ref.shape[1]
    y = _dot(a_ref[...].astype(BF16), w_ref[0:ka, :]) + _dot(b_ref[...].astype(BF16), w_ref[ka:, :])
    xn = x_ref[...] + y
    if final:
        xn = _rms(xn, fw_ref[...])
    o_ref[...] = xn


def _out_proj(a, b, ka, b_col, w, x, fw, final, tm):
    n, d = x.shape
    kb = w.shape[0] - ka
    return pl.pallas_call(
        functools.partial(_out_kernel, final=final),
        grid=(n // tm,),
        in_specs=[pl.BlockSpec((tm, ka), lambda i: (i, 0)),
                  pl.BlockSpec((tm, kb), lambda i: (i, b_col)),
                  pl.BlockSpec(w.shape, lambda i: (0, 0)),
                  pl.BlockSpec((tm, d), lambda i: (i, 0)),
                  pl.BlockSpec((1, d), lambda i: (0, 0))],
        out_specs=pl.BlockSpec((tm, d), lambda i: (i, 0)),
        out_shape=jax.ShapeDtypeStruct((n, d), F32),
        compiler_params=_cparams(("parallel",)), name="out_proj")(a, b, w, x, fw.reshape(1, d))


def _neumann(a_list, c):
    eye = jnp.where(_iota((c, c), 0) == _iota((c, c), 1), 1.0, 0.0)
    ts = [eye - a for a in a_list]
    abs_ = [a.astype(BF16) for a in a_list]
    ms = [_dot(ab, ab) for ab in abs_]
    levels = int(math.log2(c)) - 1
    for k in range(levels):
        mbs = [m.astype(BF16) for m in ms]
        ts = [t + _dot(mb, t.astype(BF16)) for mb, t in zip(mbs, ts)]
        if k + 1 < levels:
            ms = [_dot(mb, mb) for mb in mbs]
    return ts


def _gdn_kernel(x_ref, z_ref, gt_ref, conv0_ref, s0_ref, cw_ref, hp_ref, nw_ref, *rest,
                c, nch, tv, dk, unroll, per_seq, lay):
    o_ref, convo_ref, so_ref, xbuf, qkv_scr, g_scr, b_scr, s_scr = rest[-8:]
    gi = pl.program_id(1)

    @pl.when(gi == pl.num_programs(1) - 1)
    def _():
        for l in range(so_ref.shape[0]):
            if l != lay:
                so_ref[l] = jnp.zeros(so_ref.shape[1:], F32)

    tg = c * nch
    hq = N_HEADS * dk
    cw = cw_ref[...]

    def conv(e):
        r = e * cw[0:1, :]
        for j in range(1, CONV_W):
            r = pltpu.roll(r, 1, 0) + e * cw[j:j + 1, :]
        return _silu(r[8:, :])

    if per_seq:
        for s in range(nch):
            base = s * (8 + c)
            xbuf[base:base + 8, :] = conv0_ref[s]
            xbuf[base + 8:base + 8 + c, :] = x_ref[s * c:(s + 1) * c, :]
            qkv_scr[s * c:(s + 1) * c, :] = conv(xbuf[base:base + 8 + c, :])
            convo_ref[s, 0:8 - (CONV_W - 1), :] = jnp.zeros((8 - (CONV_W - 1), xbuf.shape[1]), F32)
            convo_ref[s, 8 - (CONV_W - 1):8, :] = xbuf[base + 8 + tv - (CONV_W - 1):base + 8 + tv, :]
    else:
        @pl.when(gi == 0)
        def _():
            xbuf[0:8, :] = conv0_ref[0]
            s_scr[...] = s0_ref[0]

        x = x_ref[...]
        qkv_scr[...] = conv(jnp.concatenate([xbuf[0:8, :], x], axis=0))
        xbuf[0:8, :] = x[tg - 8:, :]

    gt = gt_ref[...]
    hp = hp_ref[...]
    lane = _iota((tg, 128), 1)
    beta = _sigmoid(gt)
    g = -jnp.exp(hp[0:1, :]) * _softplus(gt + hp[1:2, :])
    g = jnp.where(lane >= N_HEADS, jnp.where(lane < 2 * N_HEADS, g, 0.0), 0.0)
    if tv < c:
        valid = (_iota((tg, 128), 0) % c) < tv
        g = jnp.where(valid, g, 0.0)
        beta = jnp.where(valid, beta, 0.0)
    g_scr[...] = g
    b_scr[...] = beta

    row = _iota((c, c), 0)
    col = _iota((c, c), 1)
    ige = row >= col
    igt = row > col
    tri = jnp.where(ige, 1.0, 0.0).astype(BF16)
    lane_c = _iota((c, 128), 1)
    nw = nw_ref[...]

    sels = [jnp.where(lane_c == N_HEADS + h, 1.0, 0.0).astype(BF16) for h in range(N_HEADS)]

    def prep(starts):
        items = [(ci, h) for ci in range(len(starts)) for h in range(N_HEADS)]
        gcums = [_exact_dot(tri, g_scr[pl.ds(r0, c), :]) for r0 in starts]
        bchs = [b_scr[pl.ds(r0, c), :] for r0 in starts]
        qns, kns, vhs = [], [], []
        for ci, h in items:
            r0 = starts[ci]
            qh = qkv_scr[pl.ds(r0, c), h * dk:(h + 1) * dk]
            kh = qkv_scr[pl.ds(r0, c), hq + h * dk:hq + (h + 1) * dk]
            vhs.append(qkv_scr[pl.ds(r0, c), 2 * hq + h * dk:2 * hq + (h + 1) * dk])
            qns.append(qh * lax.rsqrt(jnp.sum(qh * qh, axis=-1, keepdims=True) + EPS) * (dk ** -0.5))
            kns.append(kh * lax.rsqrt(jnp.sum(kh * kh, axis=-1, keepdims=True) + EPS))
        qkks = [_dot_nt(jnp.concatenate([qn, kn], axis=0).astype(BF16), kn.astype(BF16))
                for qn, kn in zip(qns, kns)]
        grows = [_exact_dot_nt(sels[h], gcums[ci]) for ci, h in items]
        gcols = [gcums[ci][:, N_HEADS + h:N_HEADS + h + 1] for ci, h in items]
        bcols = [bchs[ci][:, h:h + 1] for ci, h in items]
        decays = [jnp.where(ige, jnp.exp(jnp.where(ige, gcol - grow, 0.0)), 0.0)
                  for gcol, grow in zip(gcols, grows)]
        t_invs = _neumann([jnp.where(igt, qkk[c:] * decay * bcol, 0.0)
                           for qkk, decay, bcol in zip(qkks, decays, bcols)], c)
        egs = [jnp.exp(gcol) for gcol in gcols]
        sols = [_dot(t_inv.astype(BF16),
                     jnp.concatenate([vh * bcol, kn * (bcol * eg)], axis=1).astype(BF16))
                for t_inv, vh, kn, bcol, eg in zip(t_invs, vhs, kns, bcols, egs)]
        glasts = [gcums[ci][c - 1:c, N_HEADS + h:N_HEADS + h + 1] for ci, h in items]
        kws = [(kn * jnp.exp(glast - gcol)).astype(BF16) for kn, glast, gcol in zip(kns, glasts, gcols)]
        aqks = [(qkk[:c] * decay).astype(BF16) for qkk, decay in zip(qkks, decays)]
        sol_bs = [sol.astype(BF16) for sol in sols]
        ktus = [_dot_tn(kw, sol_b) for kw, sol_b in zip(kws, sol_bs)]
        aus = [_dot(aqk, sol_b) for aqk, sol_b in zip(aqks, sol_bs)]
        out = []
        for i in range(len(items)):
            lhs = jnp.concatenate([-ktus[i][:, dk:], qns[i] * egs[i] - aus[i][:, dk:]], axis=0).astype(BF16)
            out.append((lhs, ktus[i][:, :dk], aus[i][:, :dk], jnp.exp(glasts[i])))
        return [out[ci * N_HEADS:(ci + 1) * N_HEADS] for ci in range(len(starts))]

    def chunks(starts, seq_ids):
        preps = prep(starts)
        for r0, sid, heads in zip(starts, seq_ids, preps):
            s_olds = [s_scr[h] if sid is None else s0_ref[sid, h] for h in range(N_HEADS)]
            res = [_dot(heads[h][0], s_olds[h].astype(BF16)) for h in range(N_HEADS)]
            for h in range(N_HEADS):
                s_new = s_olds[h] * heads[h][3] + (heads[h][1] + res[h][:dk])
                if sid is None:
                    s_scr[h] = s_new
                else:
                    so_ref[lay, sid, h] = s_new
            for h in range(N_HEADS):
                zz = z_ref[pl.ds(r0, c), h * dk:(h + 1) * dk]
                o = heads[h][2] + res[h][dk:]
                o_ref[pl.ds(r0, c), h * dk:(h + 1) * dk] = _rms(o, nw) * _silu(zz)

    if per_seq:
        chunks([s * c for s in range(nch)], list(range(nch)))
    else:
        def body(it, carry):
            chunks([pl.multiple_of((it * unroll + i) * c, c) for i in range(unroll)], [None] * unroll)
            return carry

        lax.fori_loop(0, nch // unroll, body, 0)

        @pl.when(gi == pl.num_programs(1) - 1)
        def _():
            convo_ref[0] = xbuf[0:8, :]
            so_ref[lay, 0] = s_scr[...]


def _gdn(qkvz, gates, conv0, s0, s_idx, conv_w, hp, nw, s_stack, out_idx, *, bn, c, nch, tv, per_seq=False):
    n = qkvz.shape[0]
    dk = s0.shape[-1]
    cc = 3 * N_HEADS * dk
    tg = c * nch
    sb = nch if per_seq else 1
    steps = 1 if per_seq else n // (bn * tg)
    create = isinstance(s_stack, int)
    held, first, lay = (s_stack, 0, out_idx) if create else (1, out_idx, 0)
    stack_shape = (s_stack, bn, N_HEADS, dk, dk) if create else s_stack.shape
    kern = functools.partial(_gdn_kernel, c=c, nch=nch, tv=tv, dk=dk, unroll=math.gcd(nch, 8), per_seq=per_seq,
                             lay=lay)
    return pl.pallas_call(
        kern, grid=(bn // sb, steps),
        in_specs=[pl.BlockSpec((tg, cc), lambda b, g: (b * steps + g, 0)),
                  pl.BlockSpec((tg, N_HEADS * dk), lambda b, g: (b * steps + g, 3)),
                  pl.BlockSpec((tg, 128), lambda b, g: (b * steps + g, 0)),
                  pl.BlockSpec((sb, 8, cc), lambda b, g: (b, 0, 0)),
                  pl.BlockSpec((None, sb, N_HEADS, dk, dk), lambda b, g: (s_idx, b, 0, 0, 0)),
                  pl.BlockSpec((CONV_W, cc), lambda b, g: (0, 0)),
                  pl.BlockSpec((8, 128), lambda b, g: (0, 0)),
                  pl.BlockSpec((1, dk), lambda b, g: (0, 0))] + (
                      [] if create else [pl.BlockSpec(memory_space=pl.ANY)]),
        out_specs=[pl.BlockSpec((tg, N_HEADS * dk), lambda b, g: (b * steps + g, 0)),
                   pl.BlockSpec((sb, 8, cc), lambda b, g: (b, 0, 0)),
                   pl.BlockSpec((held, sb, N_HEADS, dk, dk), lambda b, g: (first, b, 0, 0, 0))],
        out_shape=[jax.ShapeDtypeStruct((n, N_HEADS * dk), F32),
                   jax.ShapeDtypeStruct((bn, 8, cc), F32),
                   jax.ShapeDtypeStruct(stack_shape, F32)],
        input_output_aliases={} if create else {8: 2},
        scratch_shapes=[pltpu.VMEM((sb * (8 + c) if per_seq else 8, cc), F32), pltpu.VMEM((tg, cc), F32),
                        pltpu.VMEM((tg, 128), F32), pltpu.VMEM((tg, 128), F32),
                        pltpu.VMEM((N_HEADS, dk, dk), F32)],
        compiler_params=_cparams(("parallel", "arbitrary")), name="gated_delta")(
            qkvz, qkvz, gates, conv0, s0, conv_w, hp, nw.reshape(1, dk), *(() if create else (s_stack,)))


def _mlstm_kernel(q_ref, k_ref, v_ref, z_ref, og_ref, gt_ref, c0_ref, n0_ref, m0_ref, hp_ref, nw_ref, *rest,
                  c, nch, sb, tv, dqk, dv, lay):
    o_ref, co_ref, no_ref, mo_ref, li_scr, lf_scr, cext, m_scr = rest[-8:]
    gi = pl.program_id(1)
    tg = c * nch
    lane1 = _iota((dqk, 128), 1)
    eye = _iota((dqk, dqk), 0) == _iota((dqk, dqk), 1)

    @pl.when(gi == 0)
    def _():
        for s in range(sb):
            m_scr[s * 8:(s + 1) * 8, :] = jnp.zeros((8, 128), F32)
            for h in range(N_HEADS):
                cext[s * N_HEADS + h, :, 0:dv] = c0_ref[s, h]
                ncol = jnp.sum(jnp.where(eye, n0_ref[s, h:h + 1, :], 0.0), axis=1, keepdims=True)
                cext[s * N_HEADS + h, :, dv:dv + 128] = jnp.broadcast_to(ncol, (dqk, 128))
                m_scr[s * 8 + h:s * 8 + h + 1, :] = jnp.broadcast_to(m0_ref[s, 0:1, h:h + 1], (1, 128))

    hp = hp_ref[...]
    lane = _iota((tg, 128), 1)
    for s in range(sb):
        gt = gt_ref[s]
        li = gt + hp[0:1, :]
        x = gt + hp[1:2, :]
        lf = jnp.minimum(x, 0.0) - jnp.log(1.0 + jnp.exp(-jnp.abs(x)))
        lf = jnp.where(lane >= N_HEADS, jnp.where(lane < 2 * N_HEADS, lf, 0.0), 0.0)
        if tv < c:
            valid = (_iota((tg, 128), 0) % c) < tv
            lf = jnp.where(valid, lf, 0.0)
            li = jnp.where(valid, li, NEG)
        li_scr[s] = li
        lf_scr[s] = lf

    row = _iota((c, c), 0)
    col = _iota((c, c), 1)
    ige = row >= col
    tri = jnp.where(ige, 1.0, 0.0).astype(BF16)
    ones_b = jnp.ones((c, 128), BF16)
    lane_c = _iota((c, 128), 1)
    ones_blk = jnp.ones((c, 128), F32)
    nw = nw_ref[...]
    nrep = dv // 128 + 1

    def wide(x):
        return jnp.concatenate([x] * nrep, axis=1)

    def prep(clist):
        items = [(ci, h) for ci in range(len(clist)) for h in range(N_HEADS)]
        bcums = [_exact_dot(tri, lf_scr[s, pl.ds(r0, c), :]) for s, r0 in clist]
        lichs = [li_scr[s, pl.ds(r0, c), :] for s, r0 in clist]
        qbs = [(q_ref[clist[ci][0], pl.ds(clist[ci][1], c), h * dqk:(h + 1) * dqk] * (dqk ** -0.5)).astype(BF16)
               for ci, h in items]
        ks = [k_ref[clist[ci][0], pl.ds(clist[ci][1], c), h * dqk:(h + 1) * dqk] for ci, h in items]
        qk_raws = [_dot_nt(qb, k.astype(BF16)) for qb, k in zip(qbs, ks)]
        bcols = [jnp.broadcast_to(bcums[ci][:, N_HEADS + h:N_HEADS + h + 1], (c, 128)) for ci, h in items]
        licols = [jnp.broadcast_to(lichs[ci][:, h:h + 1], (c, 128)) for ci, h in items]
        rowvs = [_exact_dot_nt(ones_b, jnp.where(lane_c == 0, licol - bcol, 0.0))
                 for licol, bcol in zip(licols, bcols)]
        dmats = [jnp.where(ige, bcol[:, :c] + rowv, NEG) for bcol, rowv in zip(bcols, rowvs)]
        dmaxs = [jnp.broadcast_to(jnp.max(dmat, axis=-1, keepdims=True), (c, 128)) for dmat in dmats]
        vexts = [jnp.concatenate([v_ref[clist[ci][0], pl.ds(clist[ci][1], c), h * dv:(h + 1) * dv], ones_blk],
                                 axis=1).astype(BF16) for ci, h in items]
        intras = [_dot((qk_raw * jnp.exp(dmat - dmax[:, :c])).astype(BF16), vext)
                  for qk_raw, dmat, dmax, vext in zip(qk_raws, dmats, dmaxs, vexts)]
        blasts = [bcol[c - 1:c, :] for bcol in bcols]
        mus = [dmax[c - 1:c, :] for dmax in dmaxs]
        kvs = [_dot_tn((k * jnp.exp(blast - bcol + licol - mu)).astype(BF16), vext)
               for k, blast, bcol, licol, mu, vext in zip(ks, blasts, bcols, licols, mus, vexts)]
        out = [(qbs[i], bcols[i], dmaxs[i], intras[i], blasts[i], mus[i], kvs[i]) for i in range(len(items))]
        return [out[ci * N_HEADS:(ci + 1) * N_HEADS] for ci in range(len(clist))]

    def chunks(clist):
        hs = range(N_HEADS)
        for (s, r0), heads in zip(clist, prep(clist)):
            c_olds = [cext[s * N_HEADS + h] for h in hs]
            qcs = [_dot(heads[h][0], c_olds[h].astype(BF16)) for h in hs]
            for h in hs:
                qb, bcol, dmax, intra, blast, mu, kv = heads[h]
                mprev = m_scr[s * 8 + h:s * 8 + h + 1, :]
                inter = bcol + mprev
                mt = jnp.maximum(inter, dmax)
                mnew = jnp.maximum(blast + mprev, mu)
                cext[s * N_HEADS + h] = (c_olds[h] * wide(jnp.exp(blast + mprev - mnew))
                                         + wide(jnp.exp(mu - mnew)) * kv)
                m_scr[s * 8 + h:s * 8 + h + 1, :] = mnew
                tot = wide(jnp.exp(inter - mt)) * qcs[h] + wide(jnp.exp(dmax - mt)) * intra
                den = jnp.maximum(jnp.abs(tot[:, dv:dv + 128]), jnp.exp(-mt))
                hh = tot[:, :dv] / jnp.concatenate([den] * (dv // 128), axis=1)
                og = og_ref[s, pl.ds(r0, c), h * dv:(h + 1) * dv]
                zz = z_ref[s, pl.ds(r0, c), h * dv:(h + 1) * dv]
                o_ref[s, pl.ds(r0, c), h * dv:(h + 1) * dv] = _rms(_sigmoid(og) * hh, nw) * _silu(zz)

    chunks([(s, i * c) for i in range(nch) for s in range(sb)])

    @pl.when(gi == pl.num_programs(1) - 1)
    def _():
        for l in range(co_ref.shape[0]):
            if l != lay:
                co_ref[l] = jnp.zeros(co_ref.shape[1:], F32)
        for s in range(sb):
            for h in range(N_HEADS):
                co_ref[lay, s, h] = cext[s * N_HEADS + h, :, 0:dv]
                ncol = jnp.broadcast_to(cext[s * N_HEADS + h, :, dv:dv + 1], (dqk, dqk))
                no_ref[s, h:h + 1, :] = jnp.sum(jnp.where(eye, ncol, 0.0), axis=0, keepdims=True)
            mrows = m_scr[s * 8:(s + 1) * 8, :]
            diag = jnp.where(_iota((8, 128), 0) == _iota((8, 128), 1), mrows, 0.0)
            mo_ref[s] = jnp.broadcast_to(jnp.sum(diag, axis=0, keepdims=True), (8, 128))


def _mlstm(q, k, v, z, og, gates, c0, c_idx, n0, m0, hp, nw, c_stack, out_idx, *, bn, c, nch, sb, tv):
    n = q.shape[0]
    dqk, dv = c0.shape[-2], c0.shape[-1]
    tg = c * nch
    rows = n // bn
    steps = rows // tg
    hq, hv = N_HEADS * dqk, N_HEADS * dv
    row = lambda b, g: (b, g, 0)
    q, k, v, z, og, gates = (a.reshape(bn, rows, a.shape[1]) for a in (q, k, v, z, og, gates))
    create = isinstance(c_stack, int)
    held, first, lay = (c_stack, 0, out_idx) if create else (1, out_idx, 0)
    stack_shape = (c_stack, bn, N_HEADS, dqk, dv) if create else c_stack.shape
    kern = functools.partial(_mlstm_kernel, c=c, nch=nch, sb=sb, tv=tv, dqk=dqk, dv=dv, lay=lay)
    out = pl.pallas_call(
        kern, grid=(bn // sb, steps),
        in_specs=[pl.BlockSpec((sb, tg, hq), row), pl.BlockSpec((sb, tg, hq), row),
                  pl.BlockSpec((sb, tg, hv), row), pl.BlockSpec((sb, tg, hv), row),
                  pl.BlockSpec((sb, tg, hv), row), pl.BlockSpec((sb, tg, 128), row),
                  pl.BlockSpec((None, sb, N_HEADS, dqk, dv), lambda b, g: (c_idx, b, 0, 0, 0)),
                  pl.BlockSpec((sb, N_HEADS, dqk), lambda b, g: (b, 0, 0)),
                  pl.BlockSpec((sb, 8, 128), lambda b, g: (b, 0, 0)),
                  pl.BlockSpec((8, 128), lambda b, g: (0, 0)),
                  pl.BlockSpec((1, dv), lambda b, g: (0, 0))] + (
                      [] if create else [pl.BlockSpec(memory_space=pl.ANY)]),
        out_specs=[pl.BlockSpec((sb, tg, hv), row),
                   pl.BlockSpec((held, sb, N_HEADS, dqk, dv), lambda b, g: (first, b, 0, 0, 0)),
                   pl.BlockSpec((sb, N_HEADS, dqk), lambda b, g: (b, 0, 0)),
                   pl.BlockSpec((sb, 8, 128), lambda b, g: (b, 0, 0))],
        out_shape=[jax.ShapeDtypeStruct((bn, rows, hv), F32),
                   jax.ShapeDtypeStruct(stack_shape, F32),
                   jax.ShapeDtypeStruct((bn, N_HEADS, dqk), F32),
                   jax.ShapeDtypeStruct((bn, 8, 128), F32)],
        input_output_aliases={} if create else {11: 1},
        scratch_shapes=[pltpu.VMEM((sb, tg, 128), F32), pltpu.VMEM((sb, tg, 128), F32),
                        pltpu.VMEM((sb * N_HEADS, dqk, dv + 128), F32), pltpu.VMEM((sb * 8, 128), F32)],
        compiler_params=_cparams(("parallel", "arbitrary")), name="mlstm")(
            q, k, v, z, og, gates, c0, n0, m0, hp, nw.reshape(1, dv), *(() if create else (c_stack,)))
    return (out[0].reshape(n, hv),) + tuple(out[1:])


def _lambda(lq, lam_init):
    a = jnp.sum(lq[0:1, :] * lq[1:2, :], axis=-1, keepdims=True)
    b = jnp.sum(lq[2:3, :] * lq[3:4, :], axis=-1, keepdims=True)
    return jnp.exp(a) - jnp.exp(b) + lam_init


def _attn_kernel(wq_ref, k_ref, vt_ref, z_ref, lq_ref, nw_ref, o_ref, acc_scr, *, t, hpb, lam_init):
    qi = pl.program_id(2)
    dh = k_ref.shape[1] // hpb
    nl = 4 * t
    acc_scr[...] = jnp.zeros(acc_scr.shape, F32)

    def step(jp, carry, masked):
        rows = [pl.multiple_of((2 * jp + i) * t, t) for i in range(2)]
        wqs = [jnp.concatenate([wq_ref[0, a], wq_ref[1, a]], axis=1) for a in range(hpb)]
        scores = [[_dot(k_ref[pl.ds(rows[i], t), a * dh:(a + 1) * dh], wqs[a]) for a in range(hpb)]
                  for i in range(2)]
        out = list(carry)
        for i in range(2):
            for a in range(hpb):
                m_prev, l_prev = out[a]
                s = scores[i][a]
                if masked:
                    lane = _iota((t, nl), 1)
                    qpos = (lane // (2 * t)) * t + lane % t
                    s = jnp.where(_iota((t, nl), 0) + i * t <= qpos, s, NEG)
                m_new = jnp.maximum(m_prev, jnp.max(s, axis=0, keepdims=True))
                alpha = jnp.exp2(m_prev - m_new)
                p = jnp.exp2(s - m_new)
                l_new = alpha * l_prev + jnp.sum(p, axis=0, keepdims=True)
                acc_scr[a] = alpha * acc_scr[a] + _dot(vt_ref[2 * jp + i, a * dh:(a + 1) * dh, :], p.astype(BF16))
                out[a] = (m_new, l_new)
        return tuple(out)

    init = tuple((jnp.full((1, nl), NEG, F32), jnp.zeros((1, nl), F32)) for _ in range(hpb))
    carry = lax.fori_loop(0, qi, lambda j, c: step(j, c, False), init)
    carry = step(qi, carry, True)

    lam = _lambda(lq_ref[...], lam_init)
    nw = nw_ref[...]
    for a in range(hpb):
        acc = acc_scr[a] * (1.0 / carry[a][1])
        for qb in range(2):
            b0 = 2 * t * qb
            o = (acc[:, b0:b0 + t] - lam * acc[:, b0 + t:b0 + 2 * t]).T
            zz = z_ref[qb * t:(qb + 1) * t, a * dh:(a + 1) * dh]
            o_ref[qb * t:(qb + 1) * t, a * dh:(a + 1) * dh] = _rms(o, nw) * (1.0 - lam_init) * _silu(zz)


def _attn_prompt(wq, k16, vt, z, lq, nw, *, bn, seq, lam_init):
    n = k16.shape[0]
    t = vt.shape[2]
    dh = wq.shape[2]
    hpb = 4
    nq = seq // t
    nq2 = nq // 2
    kern = functools.partial(_attn_kernel, t=t, hpb=hpb, lam_init=lam_init)
    return pl.pallas_call(
        kern, grid=(bn, N_HEADS // hpb, nq2),
        in_specs=[pl.BlockSpec((2, hpb, dh, 2 * t), lambda b, h, i: (b * nq2 + i, h, 0, 0)),
                  pl.BlockSpec((seq, hpb * dh), lambda b, h, i: (b, h), pipeline_mode=pl.Buffered(1)),
                  pl.BlockSpec((nq, hpb * dh, t), lambda b, h, i: (b, h, 0), pipeline_mode=pl.Buffered(1)),
                  pl.BlockSpec((2 * t, hpb * dh), lambda b, h, i: (b * nq2 + i, h)),
                  pl.BlockSpec(lq.shape, lambda b, h, i: (0, 0)),
                  pl.BlockSpec((1, dh), lambda b, h, i: (0, 0))],
        out_specs=pl.BlockSpec((2 * t, hpb * dh), lambda b, h, i: (b * nq2 + i, h)),
        out_shape=jax.ShapeDtypeStruct((n, N_HEADS * dh), F32),
        scratch_shapes=[pltpu.VMEM((hpb, dh, 4 * t), F32)],
        compiler_params=_cparams(("parallel", "parallel", "arbitrary")), name="diff_attn_prompt")(
            wq, k16, vt, z, lq, nw.reshape(1, dh))


def _attn_s_kernel(pt_ref, q_ref, kn_ref, vn_ref, z_ref, lq_ref, nw_ref, *rest, pg, tv, lam_init):
    k_pages = rest[:pg]
    v_pages = rest[pg:2 * pg]
    o_ref, qbd, bias, m_scr, l_scr, acc_scr = rest[2 * pg:]
    j = pl.program_id(1)
    tp, hd = q_ref.shape
    dh = hd // N_HEADS
    dqk = dh // 2
    prow = k_pages[0].shape[0]
    rph = 2 * tp

    @pl.when(j == 0)
    def _():
        q = q_ref[...] * (dqk ** -0.5)
        lane = _iota((tp, dh), 1)
        for h in range(N_HEADS):
            qh = q[:, h * dh:(h + 1) * dh]
            qbd[h * rph:h * rph + tp, :] = jnp.where(lane < dqk, qh, 0.0)
            qbd[h * rph + tp:(h + 1) * rph, :] = jnp.where(lane >= dqk, qh, 0.0)
        rhead = _iota(bias.shape, 0) // rph
        chead = _iota(bias.shape, 1) % N_HEADS
        bias[...] = jnp.where(rhead == chead, 0.0, NEG)
        m_scr[...] = jnp.full(m_scr.shape, NEG, F32)
        l_scr[...] = jnp.zeros(l_scr.shape, F32)
        acc_scr[...] = jnp.zeros(acc_scr.shape, F32)

    def update(s, pv):
        m_prev = m_scr[...]
        m_new = jnp.maximum(m_prev, jnp.max(s, axis=-1, keepdims=True))
        alpha = jnp.exp(m_prev - m_new)
        p = jnp.exp(s - m_new[:, 0:1])
        l_scr[...] = alpha * l_scr[...] + jnp.sum(p, axis=-1, keepdims=True)
        acc_scr[...] = alpha * acc_scr[...] + pv(p)
        m_scr[...] = m_new

    qb = qbd[...].astype(BF16)
    bs_ = bias[...]
    s = jnp.concatenate([_dot_nt(qb, kp[...].astype(BF16)) + bs_ for kp in k_pages], axis=1)

    def pv_pages(p):
        pb = p.astype(BF16)
        acc = _dot(pb[:, 0:prow], v_pages[0][...].astype(BF16))
        for i in range(1, pg):
            acc = acc + _dot(pb[:, i * prow:(i + 1) * prow], v_pages[i][...].astype(BF16))
        return acc

    update(s, pv_pages)

    @pl.when(j == pl.num_programs(1) - 1)
    def _():
        qf = qbd[...]
        kn = kn_ref[...]
        vn = vn_ref[...]
        sn = jnp.concatenate([_dot_nt(qf[h * rph:(h + 1) * rph], kn[:, h * dh:(h + 1) * dh])
                              for h in range(N_HEADS)], axis=0)
        tq = _iota(sn.shape, 0) % tp
        tk = _iota(sn.shape, 1)
        sn = jnp.where(tk <= tq, jnp.where(tk < tv, sn, NEG), NEG)
        update(sn, lambda p: jnp.concatenate(
            [_dot(p[h * rph:(h + 1) * rph], vn[:, h * dh:(h + 1) * dh]) for h in range(N_HEADS)], axis=0))
        acc = acc_scr[...] / l_scr[...]
        lam = _lambda(lq_ref[...], lam_init)
        nw = nw_ref[...]
        normed = jnp.concatenate(
            [_rms(acc[h * rph:h * rph + tp] - lam * acc[h * rph + tp:(h + 1) * rph], nw) for h in range(N_HEADS)],
            axis=1)
        o_ref[...] = normed * (1.0 - lam_init) * _silu(z_ref[...])


def _attn_sample(page_table, q, kn, vn, z, lq, nw, cache_k, cache_v, *, e, tv, lam_init):
    bs, npages = page_table.shape
    n, hd = q.shape
    tp = n // bs
    dh = hd // N_HEADS
    pg = math.gcd(npages, PAGES_PER_STEP)
    prow = cache_k.shape[2]
    steps = npages // pg
    rows = 2 * N_HEADS * tp
    tok = lambda b, j, pt: (b, 0)
    page_specs = [pl.BlockSpec((None, None, prow, dh),
                               lambda b, j, pt, i=i: (e, pt[b, j * pg + i], 0, 0)) for i in range(pg)]
    kern = functools.partial(_attn_s_kernel, pg=pg, tv=tv, lam_init=lam_init)
    grid_spec = pltpu.PrefetchScalarGridSpec(
        num_scalar_prefetch=1, grid=(bs, steps),
        in_specs=[pl.BlockSpec((tp, hd), tok), pl.BlockSpec((tp, hd), tok), pl.BlockSpec((tp, hd), tok),
                  pl.BlockSpec((tp, hd), tok),
                  pl.BlockSpec(lq.shape, lambda b, j, pt: (0, 0)),
                  pl.BlockSpec((1, dh), lambda b, j, pt: (0, 0))] + page_specs + page_specs,
        out_specs=pl.BlockSpec((tp, hd), tok),
        scratch_shapes=[pltpu.VMEM((rows, dh), F32), pltpu.VMEM((rows, prow), F32), pltpu.VMEM((rows, 128), F32),
                        pltpu.VMEM((rows, 128), F32), pltpu.VMEM((rows, dh), F32)])
    return pl.pallas_call(
        kern, grid_spec=grid_spec, out_shape=jax.ShapeDtypeStruct((n, hd), F32),
        compiler_params=_cparams(("parallel", "arbitrary")), name="diff_attn_sample")(
            page_table, q, kn, vn, z, lq, nw.reshape(1, dh), *([cache_k] * pg), *([cache_v] * pg))


def _rope_tables(pos, dqk):
    rope_dim = dqk // 4
    half = rope_dim // 2
    inv = jnp.power(ROPE_THETA, -jnp.arange(half, dtype=F32) / half)
    ang = pos.astype(F32)[:, None] * inv[None, :]
    lane = jnp.arange(128) % dqk
    idx = lane % half
    cos = jnp.where(lane < rope_dim, jnp.cos(ang)[:, idx], 1.0)
    sin = jnp.sin(ang)[:, idx]
    sa = jnp.where((lane >= half) & (lane < rope_dim), sin, 0.0)
    sb = jnp.where(lane < half, -sin, 0.0)
    return cos.astype(F32), sa.astype(F32), sb.astype(F32)


def _lane_row(vals, offset):
    out = jnp.zeros((128,), F32)
    return out.at[offset:offset + vals.shape[0]].set(vals.astype(F32))


def kernel(x_prompt, x_sample, cache_k, cache_v, page_table, state_gdn_conv, state_gdn_s, state_mlstm_c,
           state_mlstm_n, state_mlstm_m, norm_w, final_norm_w, w_in_even, w_out_even, conv_w, a_log, dt_bias,
           gdn_norm_w, lambda_qk, diff_norm_w, w_in_odd, w_out_odd, b_i, b_f, mlstm_norm_w):
    bp, seq, d = x_prompt.shape
    bs, ts, _ = x_sample.shape
    depth = norm_w.shape[0]
    tp = SAMPLE_PAD
    dk_a = state_gdn_s.shape[-1]
    conv_ch = state_gdn_conv.shape[-1]
    w_a = N_HEADS * dk_a
    dh_b = cache_v.shape[-1]
    dqk_b = dh_b // 2
    w_b = N_HEADS * dh_b
    dqk_c, dv_c = state_mlstm_c.shape[-2], state_mlstm_c.shape[-1]
    hq_c, w_c = N_HEADS * dqk_c, N_HEADS * dv_c
    past = page_table.shape[1] * cache_k.shape[2]
    n_pool, page = cache_k.shape[1], cache_k.shape[2]

    xp = x_prompt.reshape(bp * seq, d)
    xs = jnp.pad(x_sample, ((0, 0), (0, tp - ts), (0, 0))).reshape(bs * tp, d)
    tm_p = math.gcd(seq, 256)
    tm_s = math.gcd(bs * tp, 256)
    nch_p = math.gcd(seq // CHUNK, 8)
    seq_s = math.gcd(bs, 8)
    seq_c = math.gcd(bs, 8)

    rope_p = _rope_tables(jnp.arange(seq), dqk_b)
    rope_s = _rope_tables(past + (jnp.arange(tm_s) % tp), dqk_b)
    half = dqk_b // 8
    ang_t = (jnp.power(ROPE_THETA, -jnp.arange(half, dtype=F32) / half)[:, None]
             * jnp.arange(seq).astype(F32)[None, :])
    cos_t, sin_t = jnp.cos(ang_t), jnp.sin(ang_t)

    even_w = conv_ch + w_a
    seg_even = [(0, even_w), (even_w, w_b), (even_w + w_b, w_b), (even_w + 2 * w_b, w_b),
                (even_w + 3 * w_b, w_b), (even_w + 4 * w_b, 128)]
    seg_even_p = [seg_even[0]] + seg_even[2:]
    rope_even = tuple(range(even_w // 128, (even_w + 2 * w_b) // 128))
    odd_w = 2 * hq_c + 3 * w_c
    seg_odd = [(0, hq_c), (hq_c, hq_c), (2 * hq_c, w_c), (2 * hq_c + w_c, w_c), (2 * hq_c + 2 * w_c, w_c),
               (odd_w, 128)]

    ck = cache_k.reshape(cache_k.shape[0], n_pool, page * N_HEADS, dh_b)
    cv = cache_v.reshape(cache_v.shape[0], n_pool, page * N_HEADS, dh_b)

    n_even = (depth + 1) // 2
    krows_p = vrows_p = n_even
    k_rows_s, v_rows_s = [], []
    n_odd = depth // 2
    s_p = s_s = n_even
    c_p = c_s = n_odd
    conv_p, conv_s = [], []
    n_p, m_p, n_s, m_s = [], [], [], []

    pend_p = pend_s = None
    for layer in range(depth):
        if layer % 2 == 0:
            e = layer // 2
            lam_init = 0.8 - 0.6 * math.exp(-0.3 * layer)
            w = w_in_even[e]
            g0 = even_w
            w_r = jnp.concatenate([w[:, :g0], w[:, g0 + 2 * N_HEADS:], w[:, g0:g0 + 2 * N_HEADS],
                                   jnp.zeros((d, 128 - 2 * N_HEADS), F32)], axis=1).astype(BF16)
            w_o = w_out_even[e].astype(BF16)
            hp = jnp.zeros((8, 128), F32).at[0].set(_lane_row(a_log[e], N_HEADS)).at[1].set(
                _lane_row(dt_bias[e], N_HEADS))
            lq = lambda_qk[e]

            wq_t = w[:, g0 + 2 * N_HEADS:g0 + 2 * N_HEADS + w_b].T.astype(BF16)
            wv_t = w[:, g0 + 2 * N_HEADS + 2 * w_b:g0 + 2 * N_HEADS + 3 * w_b].T.astype(BF16)
            outs = _proj(xp, norm_w[layer], w_r, seg_even_p, (rope_even, rope_p), tm_p, k16_seg=1,
                         feat=(wq_t, wv_t, cos_t, sin_t), rows=((1, 2), e, (krows_p, vrows_p)), pre=pend_p)
            if pend_p is not None:
                xp = outs.pop()
            az, krows_p, vrows_p, zb, gt, k16, wq, vt = outs
            oa, cst, s_p = _gdn(az, gt, jnp.zeros((bp, 8, conv_ch), F32),
                                jnp.zeros((1, bp, N_HEADS, dk_a, dk_a), F32), 0,
                                conv_w[e], hp, gdn_norm_w[e], s_p, e, bn=bp, c=CHUNK, nch=nch_p, tv=CHUNK)
            ob = _attn_prompt(wq, k16, vt, zb, lq, diff_norm_w[e], bn=bp, seq=seq, lam_init=lam_init)
            pend_p = (oa, ob, w_a, 0, w_o)
            conv_p.append(cst[:, 8 - (CONV_W - 1):])

            outs = _proj(xs, norm_w[layer], w_r, seg_even, (rope_even, rope_s), tm_s, pre=pend_s)
            if pend_s is not None:
                xs = outs.pop()
            az, qb, kb, vb, zb, gt = outs
            conv0 = jnp.pad(state_gdn_conv[e], ((0, 0), (8 - (CONV_W - 1), 0), (0, 0)))
            oa, cst, s_s = _gdn(az, gt, conv0, state_gdn_s, e, conv_w[e], hp, gdn_norm_w[e], s_s, e,
                                bn=bs, c=tp, nch=seq_s, tv=ts, per_seq=True)
            ob = _attn_sample(page_table, qb, kb, vb, zb, lq, diff_norm_w[e], ck, cv, e=e, tv=ts,
                              lam_init=lam_init)
            pend_s = (oa, ob, w_a, 0, w_o)
            k_rows_s.append(kb.reshape(bs, tp, N_HEADS, dh_b)[:, :ts])
            v_rows_s.append(vb.reshape(bs, tp, N_HEADS, dh_b)[:, :ts])
            conv_s.append(cst[:, 8 - (CONV_W - 1):])
        else:
            o = layer // 2
            w_r = jnp.concatenate([w_in_odd[o], jnp.zeros((d, 128 - 2 * N_HEADS), F32)], axis=1).astype(BF16)
            w_o = w_out_odd[o].astype(BF16)
            hp = jnp.zeros((8, 128), F32).at[0].set(_lane_row(b_i[o], 0)).at[1].set(_lane_row(b_f[o], N_HEADS))

            outs = _proj(xp, norm_w[layer], w_r, seg_odd, None, tm_p, pre=pend_p)
            if pend_p is not None:
                xp = outs.pop()
            q, k, v, z, og, gt = outs
            hm, c_p, nf, mf = _mlstm(q, k, v, z, og, gt, jnp.zeros((1, bp, N_HEADS, dqk_c, dv_c), F32), 0,
                                     jnp.zeros((bp, N_HEADS, dqk_c), F32), jnp.zeros((bp, 8, 128), F32),
                                     hp, mlstm_norm_w[o], c_p, o, bn=bp, c=CHUNK, nch=math.gcd(seq // CHUNK, 8),
                                     sb=1, tv=CHUNK)
            pend_p = (hm, hm, w_c // 2, 1, w_o)
            n_p.append(nf)
            m_p.append(mf[:, 0, :N_HEADS])

            outs = _proj(xs, norm_w[layer], w_r, seg_odd, None, tm_s, pre=pend_s)
            if pend_s is not None:
                xs = outs.pop()
            q, k, v, z, og, gt = outs
            m0 = jnp.zeros((bs, 8, 128), F32).at[:, 0, :N_HEADS].set(state_mlstm_m[o])
            hm, c_s, nf, mf = _mlstm(q, k, v, z, og, gt, state_mlstm_c, o, state_mlstm_n[o], m0,
                                     hp, mlstm_norm_w[o], c_s, o, bn=bs, c=tp, nch=1, sb=seq_c, tv=ts)
            pend_s = (hm, hm, w_c // 2, 1, w_o)
            n_s.append(nf)
            m_s.append(mf[:, 0, :N_HEADS])

    xp = _out_proj(*pend_p, xp, final_norm_w, True, tm_p)
    xs = _out_proj(*pend_s, xs, final_norm_w, True, tm_s)
    y_prompt = xp.reshape(bp, seq, d)
    y_sample = xs.reshape(bs, tp, d)[:, :ts]
    st = jnp.stack
    kv_shape = (n_even, bp, seq, N_HEADS, dh_b)
    return (y_prompt, y_sample, krows_p.reshape(kv_shape), vrows_p.reshape(kv_shape), st(k_rows_s), st(v_rows_s),
            st(conv_p), s_p, st(conv_s), s_s,
            c_p, st(n_p), st(m_p), c_s, st(n_s), st(m_s))
```

```python
import functools
import math

import jax
import jax.numpy as jnp
from jax import lax
from jax.experimental import pallas as pl
from jax.experimental.pallas import tpu as pltpu

F32 = jnp.float32
BF16 = jnp.bfloat16

EPS = 1e-6
NEG = -1e30
N_HEADS = 4
CONV_W = 4
ROPE_THETA = 500000.0
CHUNK = 64
SAMPLE_PAD = 8
PAGES_PER_STEP = 16
V7X_VMEM_LIMIT = 56 * 1024 * 1024


def _cparams(sem):
    return pltpu.CompilerParams(dimension_semantics=sem, vmem_limit_bytes=V7X_VMEM_LIMIT)


def _dot(a, b):
    return jnp.dot(a, b, preferred_element_type=F32)


def _dot_nt(a, b):
    return lax.dot_general(a, b, (((1,), (1,)), ((), ())), preferred_element_type=F32)


def _dot_tn(a, b):
    return lax.dot_general(a, b, (((0,), (0,)), ((), ())), preferred_element_type=F32)


def _split3(x):
    hi = x.astype(BF16)
    r = x - hi.astype(F32)
    mid = r.astype(BF16)
    lo = (r - mid.astype(F32)).astype(BF16)
    return hi, mid, lo


def _exact_dot(a_bf16, x):
    hi, mid, lo = _split3(x)
    return _dot(a_bf16, hi) + (_dot(a_bf16, mid) + _dot(a_bf16, lo))


def _exact_dot_nt(a_bf16, x):
    hi, mid, lo = _split3(x)
    return _dot_nt(a_bf16, hi) + (_dot_nt(a_bf16, mid) + _dot_nt(a_bf16, lo))


def _sigmoid(x):
    return 1.0 / (1.0 + jnp.exp(-x))


def _silu(x):
    return x * _sigmoid(x)


def _softplus(x):
    return jnp.maximum(x, 0.0) + jnp.log(1.0 + jnp.exp(-jnp.abs(x)))


def _rms(x, w):
    return x * lax.rsqrt(jnp.mean(x * x, axis=-1, keepdims=True) + EPS) * w


def _iota(shape, dim):
    return lax.broadcasted_iota(jnp.int32, shape, dim)


def _proj_kernel(*refs, segs, rope_blocks, k16_seg, feat_major, row_segs, row_lay, n_alias, fuse_out):
    n_out = len(segs)
    pos = 3
    x_ref, nw_ref, w_ref = refs[:3]
    if rope_blocks:
        cos_ref, sa_ref, sb_ref = refs[pos:pos + 3]
        pos += 3
    if feat_major:
        wqt_ref, wvt_ref, cost_ref, sint_ref = refs[pos:pos + 4]
        pos += 4
    if fuse_out:
        a_ref, b_ref, wo_ref = refs[pos:pos + 3]
        pos += 3
    pos += n_alias
    outs = refs[pos:pos + n_out]
    pos += n_out
    x = x_ref[...]
    if fuse_out:
        ka = a_ref.shape[1]
        x = x + (_dot(a_ref[...].astype(BF16), wo_ref[0:ka, :]) + _dot(b_ref[...].astype(BF16), wo_ref[ka:, :]))
        refs[-1][...] = x
    h = _rms(x, nw_ref[...]).astype(BF16)
    for si, ((c0, width), o_ref) in enumerate(zip(segs, outs)):
        for s0 in range(0, width, 512):
            sw = min(512, width - s0)
            acc = _dot(h, w_ref[:, c0 + s0:c0 + s0 + sw])
            if rope_blocks and (c0 + s0) // 128 in rope_blocks:
                cos = cos_ref[...]
                sa = sa_ref[...]
                sb = sb_ref[...]
                parts = []
                for t0 in range(0, sw, 128):
                    a = acc[:, t0:t0 + 128]
                    parts.append(a * cos + pltpu.roll(a, 8, 1) * sa + pltpu.roll(a, 120, 1) * sb)
                acc = jnp.concatenate(parts, axis=1)
            if si in row_segs:
                dh = sw // N_HEADS
                for hd in range(N_HEADS):
                    o_ref[row_lay, pl.ds(hd, acc.shape[0], stride=N_HEADS), :] = acc[:, hd * dh:(hd + 1) * dh]
                for l in range(o_ref.shape[0]):
                    if l != row_lay:
                        o_ref[l] = jnp.zeros(o_ref.shape[1:], F32)
            else:
                o_ref[:, s0:s0 + sw] = acc
            if si == k16_seg:
                refs[pos][:, s0:s0 + sw] = acc.astype(BF16)
    if feat_major:
        wq_ref, vt_ref = refs[pos + 1], refs[pos + 2]
        tm = x.shape[0]
        vt_ref[0] = _dot_nt(wvt_ref[...], h).astype(BF16)
        qt = _dot_nt(wqt_ref[...], h)
        dqk = qt.shape[0] // (2 * N_HEADS)
        half = dqk // 8
        cos = cost_ref[...]
        sin = sint_ref[...]
        scale = (dqk ** -0.5) * math.log2(math.e)
        zero = jnp.zeros((dqk, tm), F32)
        for hd in range(N_HEADS):
            maps = []
            for m in range(2):
                r0 = (2 * hd + m) * dqk
                x1 = qt[r0:r0 + half]
                x2 = qt[r0 + half:r0 + 2 * half]
                maps.append(jnp.concatenate([x1 * cos - x2 * sin, x2 * cos + x1 * sin,
                                             qt[r0 + 2 * half:r0 + dqk]], axis=0) * scale)
            top = jnp.concatenate([maps[0], zero], axis=1)
            bot = jnp.concatenate([zero, maps[1]], axis=1)
            wq_ref[0, hd] = jnp.concatenate([top, bot], axis=0).astype(BF16)


def _proj(x, nw, w, segs, rope, tm, k16_seg=None, feat=None, rows=None, pre=None):
    n, d = x.shape
    ncols = w.shape[1]
    grid = (n // tm,)
    in_specs = [pl.BlockSpec((tm, d), lambda i: (i, 0)),
                pl.BlockSpec((1, d), lambda i: (0, 0)),
                pl.BlockSpec((d, ncols), lambda i: (0, 0))]
    args = [x, nw.reshape(1, d), w]
    rope_blocks = ()
    if rope is not None:
        rope_blocks, tables = rope
        nt = tables[0].shape[0] // tm
        for t in tables:
            in_specs.append(pl.BlockSpec((tm, 128), lambda i, nt=nt: (i % nt, 0)))
            args.append(t)
    out_shape = [jax.ShapeDtypeStruct((n, width), F32) for _, width in segs]
    out_specs = [pl.BlockSpec((tm, width), lambda i: (i, 0)) for _, width in segs]
    if feat is not None:
        wqt, wvt, cost, sint = feat
        ntt = cost.shape[1] // tm
        in_specs += [pl.BlockSpec(wqt.shape, lambda i: (0, 0)), pl.BlockSpec(wvt.shape, lambda i: (0, 0)),
                     pl.BlockSpec((cost.shape[0], tm), lambda i, ntt=ntt: (0, i % ntt)),
                     pl.BlockSpec((sint.shape[0], tm), lambda i, ntt=ntt: (0, i % ntt))]
        args += [wqt, wvt, cost, sint]
    if pre is not None:
        a, b, ka, b_col, w_out = pre
        kb = w_out.shape[0] - ka
        in_specs += [pl.BlockSpec((tm, ka), lambda i: (i, 0)), pl.BlockSpec((tm, kb), lambda i: (i, b_col)),
                     pl.BlockSpec(w_out.shape, lambda i: (0, 0))]
        args += [a, b, w_out]
    row_segs, aliases, row_lay = (), {}, 0
    if rows is not None:
        row_segs, layer, stacked = rows
        for si, arr in zip(row_segs, stacked):
            dh = segs[si][1] // N_HEADS
            if isinstance(arr, int):
                held, first, row_lay, shape = arr, 0, layer, (arr, n * N_HEADS, dh)
            else:
                held, first, shape = 1, layer, arr.shape
                aliases[len(args)] = si
                in_specs.append(pl.BlockSpec(memory_space=pl.ANY))
                args.append(arr)
            out_shape[si] = jax.ShapeDtypeStruct(shape, F32)
            out_specs[si] = pl.BlockSpec((held, tm * N_HEADS, dh), lambda i, first=first: (first, i, 0))
    if k16_seg is not None:
        kw = segs[k16_seg][1]
        out_shape.append(jax.ShapeDtypeStruct((n, kw), BF16))
        out_specs.append(pl.BlockSpec((tm, kw), lambda i: (i, 0)))
    if feat is not None:
        hq, hv = wqt.shape[0], wvt.shape[0]
        dh = hq // N_HEADS
        out_shape += [jax.ShapeDtypeStruct((n // tm, N_HEADS, dh, 2 * tm), BF16),
                      jax.ShapeDtypeStruct((n // tm, hv, tm), BF16)]
        out_specs += [pl.BlockSpec((1, N_HEADS, dh, 2 * tm), lambda i: (i, 0, 0, 0)),
                      pl.BlockSpec((1, hv, tm), lambda i: (i, 0, 0))]
    if pre is not None:
        out_shape.append(jax.ShapeDtypeStruct((n, d), F32))
        out_specs.append(pl.BlockSpec((tm, d), lambda i: (i, 0)))
    return list(pl.pallas_call(
        functools.partial(_proj_kernel, segs=tuple(segs), rope_blocks=tuple(rope_blocks), k16_seg=k16_seg,
                          feat_major=feat is not None, row_segs=tuple(row_segs), row_lay=row_lay,
                          n_alias=len(aliases), fuse_out=pre is not None),
        grid=grid, in_specs=in_specs, out_specs=out_specs, out_shape=out_shape, input_output_aliases=aliases,
        compiler_params=_cparams(("parallel",)), name="norm_proj")(*args))


def _out_kernel(a_ref, b_ref, w_ref, x_ref, fw_ref, o_ref, *, final):
    ka = a_ref.shape[1]
    y = _dot(a_ref[...].astype(BF16), w_ref[0:ka, :]) + _dot(b_ref[...].astype(BF16), w_ref[ka:, :])
    xn = x_ref[...] + y
    if final:
        xn = _rms(xn, fw_ref[...])
    o_ref[...] = xn


def _out_proj(a, b, ka, b_col, w, x, fw, final, tm):
    n, d = x.shape
    kb = w.shape[0] - ka
    return pl.pallas_call(
        functools.partial(_out_kernel, final=final),
        grid=(n // tm,),
        in_specs=[pl.BlockSpec((tm, ka), lambda i: (i, 0)),
                  pl.BlockSpec((tm, kb), lambda i: (i, b_col)),
                  pl.BlockSpec(w.shape, lambda i: (0, 0)),
                  pl.BlockSpec((tm, d), lambda i: (i, 0)),
                  pl.BlockSpec((1, d), lambda i: (0, 0))],
        out_specs=pl.BlockSpec((tm, d), lambda i: (i, 0)),
        out_shape=jax.ShapeDtypeStruct((n, d), F32),
        compiler_params=_cparams(("parallel",)), name="out_proj")(a, b, w, x, fw.reshape(1, d))


def _neumann(a_list, c):
    eye = jnp.where(_iota((c, c), 0) == _iota((c, c), 1), 1.0, 0.0)
    ts = [eye - a for a in a_list]
    abs_ = [a.astype(BF16) for a in a_list]
    ms = [_dot(ab, ab) for ab in abs_]
    levels = int(math.log2(c)) - 1
    for k in range(levels):
        mbs = [m.astype(BF16) for m in ms]
        ts = [t + _dot(mb, t.astype(BF16)) for mb, t in zip(mbs, ts)]
        if k + 1 < levels:
            ms = [_dot(mb, mb) for mb in mbs]
    return ts


def _gdn_kernel(x_ref, z_ref, gt_ref, conv0_ref, s0_ref, cw_ref, hp_ref, nw_ref, *rest,
                c, nch, tv, dk, unroll, per_seq, lay):
    o_ref, convo_ref, so_ref, xbuf, qkv_scr, g_scr, b_scr, s_scr = rest[-8:]
    gi = pl.program_id(1)

    @pl.when(gi == pl.num_programs(1) - 1)
    def _():
        for l in range(so_ref.shape[0]):
            if l != lay:
                so_ref[l] = jnp.zeros(so_ref.shape[1:], F32)

    tg = c * nch
    hq = N_HEADS * dk
    cw = cw_ref[...]

    def conv(e, w):
        r = e * w[0:1, :]
        for j in range(1, CONV_W):
            r = pltpu.roll(r, 1, 0) + e * w[j:j + 1, :]
        return _silu(r[8:, :])

    if per_seq:
        for s in range(nch):
            base = s * (8 + c)
            xbuf[base:base + 8, :] = conv0_ref[s]
            xbuf[base + 8:base + 8 + c, :] = x_ref[s * c:(s + 1) * c, :]
            qkv_scr[s * c:(s + 1) * c, :] = conv(xbuf[base:base + 8 + c, :], cw)
            convo_ref[s, 0:8 - (CONV_W - 1), :] = jnp.zeros((8 - (CONV_W - 1), xbuf.shape[1]), F32)
            convo_ref[s, 8 - (CONV_W - 1):8, :] = xbuf[base + 8 + tv - (CONV_W - 1):base + 8 + tv, :]
    else:
        @pl.when(gi == 0)
        def _():
            xbuf[0:8, :] = conv0_ref[0]
            s_scr[...] = s0_ref[0]

    gt = gt_ref[...]
    hp = hp_ref[...]
    lane = _iota((tg, 128), 1)
    beta = _sigmoid(gt)
    g = -jnp.exp(hp[0:1, :]) * _softplus(gt + hp[1:2, :])
    g = jnp.where(lane >= N_HEADS, jnp.where(lane < 2 * N_HEADS, g, 0.0), 0.0)
    if tv < c:
        valid = (_iota((tg, 128), 0) % c) < tv
        g = jnp.where(valid, g, 0.0)
        beta = jnp.where(valid, beta, 0.0)
    g_scr[...] = g
    b_scr[...] = beta

    row = _iota((c, c), 0)
    col = _iota((c, c), 1)
    ige = row >= col
    igt = row > col
    tri = jnp.where(ige, 1.0, 0.0).astype(BF16)
    lane_c = _iota((c, 128), 1)
    nw = nw_ref[...]

    sels = [jnp.where(lane_c == N_HEADS + h, 1.0, 0.0).astype(BF16) for h in range(N_HEADS)]

    def take(ci, r0, col):
        if per_seq:
            return qkv_scr[pl.ds(r0, c), col:col + dk]
        hist = xbuf[0:8, col:col + dk] if ci == 0 else x_ref[pl.ds(pl.multiple_of(r0 - 8, 8), 8), col:col + dk]
        return conv(jnp.concatenate([hist, x_ref[pl.ds(r0, c), col:col + dk]], axis=0), cw[:, col:col + dk])

    def prep(starts):
        items = [(ci, h) for ci in range(len(starts)) for h in range(N_HEADS)]
        gcums = [_exact_dot(tri, g_scr[pl.ds(r0, c), :]) for r0 in starts]
        bchs = [b_scr[pl.ds(r0, c), :] for r0 in starts]
        qns, kns, vhs = [], [], []
        for ci, h in items:
            r0 = starts[ci]
            qh = take(ci, r0, h * dk)
            kh = take(ci, r0, hq + h * dk)
            vhs.append(take(ci, r0, 2 * hq + h * dk))
            qns.append(qh * lax.rsqrt(jnp.sum(qh * qh, axis=-1, keepdims=True) + EPS) * (dk ** -0.5))
            kns.append(kh * lax.rsqrt(jnp.sum(kh * kh, axis=-1, keepdims=True) + EPS))
        qkks = [_dot_nt(jnp.concatenate([qn, kn], axis=0).astype(BF16), kn.astype(BF16))
                for qn, kn in zip(qns, kns)]
        grows = [_exact_dot_nt(sels[h], gcums[ci]) for ci, h in items]
        gcols = [gcums[ci][:, N_HEADS + h:N_HEADS + h + 1] for ci, h in items]
        bcols = [bchs[ci][:, h:h + 1] for ci, h in items]
        decays = [jnp.where(ige, jnp.exp(jnp.where(ige, gcol - grow, 0.0)), 0.0)
                  for gcol, grow in zip(gcols, grows)]
        t_invs = _neumann([jnp.where(igt, qkk[c:] * decay * bcol, 0.0)
                           for qkk, decay, bcol in zip(qkks, decays, bcols)], c)
        egs = [jnp.exp(gcol) for gcol in gcols]
        sols = [_dot(t_inv.astype(BF16),
                     jnp.concatenate([vh * bcol, kn * (bcol * eg)], axis=1).astype(BF16))
                for t_inv, vh, kn, bcol, eg in zip(t_invs, vhs, kns, bcols, egs)]
        glasts = [gcums[ci][c - 1:c, N_HEADS + h:N_HEADS + h + 1] for ci, h in items]
        kws = [(kn * jnp.exp(glast - gcol)).astype(BF16) for kn, glast, gcol in zip(kns, glasts, gcols)]
        aqks = [(qkk[:c] * decay).astype(BF16) for qkk, decay in zip(qkks, decays)]
        sol_bs = [sol.astype(BF16) for sol in sols]
        ktus = [_dot_tn(kw, sol_b) for kw, sol_b in zip(kws, sol_bs)]
        aus = [_dot(aqk, sol_b) for aqk, sol_b in zip(aqks, sol_bs)]
        out = []
        for i in range(len(items)):
            lhs = jnp.concatenate([-ktus[i][:, dk:], qns[i] * egs[i] - aus[i][:, dk:]], axis=0).astype(BF16)
            out.append((lhs, ktus[i][:, :dk], aus[i][:, :dk], jnp.exp(glasts[i])))
        return [out[ci * N_HEADS:(ci + 1) * N_HEADS] for ci in range(len(starts))]

    def chunks(starts, seq_ids):
        preps = prep(starts)
        for r0, sid, heads in zip(starts, seq_ids, preps):
            s_olds = [s_scr[h] if sid is None else s0_ref[sid, h] for h in range(N_HEADS)]
            res = [_dot(heads[h][0], s_olds[h].astype(BF16)) for h in range(N_HEADS)]
            for h in range(N_HEADS):
                s_new = s_olds[h] * heads[h][3] + (heads[h][1] + res[h][:dk])
                if sid is None:
                    s_scr[h] = s_new
                else:
                    so_ref[lay, sid, h] = s_new
            for h in range(N_HEADS):
                zz = z_ref[pl.ds(r0, c), h * dk:(h + 1) * dk]
                o = heads[h][2] + res[h][dk:]
                o_ref[pl.ds(r0, c), h * dk:(h + 1) * dk] = _rms(o, nw) * _silu(zz)

    if per_seq:
        chunks([s * c for s in range(nch)], list(range(nch)))
    else:
        def body(it, carry):
            chunks([pl.multiple_of((it * unroll + i) * c, c) for i in range(unroll)], [None] * unroll)
            xbuf[0:8, :] = x_ref[pl.ds(pl.multiple_of((it + 1) * unroll * c - 8, 8), 8), :]
            return carry

        lax.fori_loop(0, nch // unroll, body, 0)

        @pl.when(gi == pl.num_programs(1) - 1)
        def _():
            convo_ref[0] = xbuf[0:8, :]
            so_ref[lay, 0] = s_scr[...]


def _gdn(qkvz, gates, conv0, s0, s_idx, conv_w, hp, nw, s_stack, out_idx, *, bn, c, nch, tv, per_seq=False):
    n = qkvz.shape[0]
    dk = s0.shape[-1]
    cc = 3 * N_HEADS * dk
    tg = c * nch
    sb = nch if per_seq else 1
    steps = 1 if per_seq else n // (bn * tg)
    create = isinstance(s_stack, int)
    held, first, lay = (s_stack, 0, out_idx) if create else (1, out_idx, 0)
    stack_shape = (s_stack, bn, N_HEADS, dk, dk) if create else s_stack.shape
    kern = functools.partial(_gdn_kernel, c=c, nch=nch, tv=tv, dk=dk, unroll=math.gcd(nch, 8), per_seq=per_seq,
                             lay=lay)
    return pl.pallas_call(
        kern, grid=(bn // sb, steps),
        in_specs=[pl.BlockSpec((tg, cc), lambda b, g: (b * steps + g, 0)),
                  pl.BlockSpec((tg, N_HEADS * dk), lambda b, g: (b * steps + g, 3)),
                  pl.BlockSpec((tg, 128), lambda b, g: (b * steps + g, 0)),
                  pl.BlockSpec((sb, 8, cc), lambda b, g: (b, 0, 0)),
                  pl.BlockSpec((None, sb, N_HEADS, dk, dk), lambda b, g: (s_idx, b, 0, 0, 0)),
                  pl.BlockSpec((CONV_W, cc), lambda b, g: (0, 0)),
                  pl.BlockSpec((8, 128), lambda b, g: (0, 0)),
                  pl.BlockSpec((1, dk), lambda b, g: (0, 0))] + (
                      [] if create else [pl.BlockSpec(memory_space=pl.ANY)]),
        out_specs=[pl.BlockSpec((tg, N_HEADS * dk), lambda b, g: (b * steps + g, 0)),
                   pl.BlockSpec((sb, 8, cc), lambda b, g: (b, 0, 0)),
                   pl.BlockSpec((held, sb, N_HEADS, dk, dk), lambda b, g: (first, b, 0, 0, 0))],
        out_shape=[jax.ShapeDtypeStruct((n, N_HEADS * dk), F32),
                   jax.ShapeDtypeStruct((bn, 8, cc), F32),
                   jax.ShapeDtypeStruct(stack_shape, F32)],
        input_output_aliases={} if create else {8: 2},
        scratch_shapes=[pltpu.VMEM((sb * (8 + c) if per_seq else 8, cc), F32), pltpu.VMEM((tg, cc), F32),
                        pltpu.VMEM((tg, 128), F32), pltpu.VMEM((tg, 128), F32),
                        pltpu.VMEM((N_HEADS, dk, dk), F32)],
        compiler_params=_cparams(("parallel", "arbitrary")), name="gated_delta")(
            qkvz, qkvz, gates, conv0, s0, conv_w, hp, nw.reshape(1, dk), *(() if create else (s_stack,)))


def _mlstm_kernel(q_ref, k_ref, v_ref, z_ref, og_ref, gt_ref, c0_ref, n0_ref, m0_ref, hp_ref, nw_ref, *rest,
                  c, nch, sb, tv, dqk, dv, lay):
    o_ref, co_ref, no_ref, mo_ref, li_scr, lf_scr, cext, m_scr = rest[-8:]
    gi = pl.program_id(1)
    tg = c * nch
    lane1 = _iota((dqk, 128), 1)
    eye = _iota((dqk, dqk), 0) == _iota((dqk, dqk), 1)

    @pl.when(gi == 0)
    def _():
        for s in range(sb):
            m_scr[s * 8:(s + 1) * 8, :] = jnp.zeros((8, 128), F32)
            for h in range(N_HEADS):
                cext[s * N_HEADS + h, :, 0:dv] = c0_ref[s, h]
                ncol = jnp.sum(jnp.where(eye, n0_ref[s, h:h + 1, :], 0.0), axis=1, keepdims=True)
                cext[s * N_HEADS + h, :, dv:dv + 128] = jnp.broadcast_to(ncol, (dqk, 128))
                m_scr[s * 8 + h:s * 8 + h + 1, :] = jnp.broadcast_to(m0_ref[s, 0:1, h:h + 1], (1, 128))

    hp = hp_ref[...]
    lane = _iota((tg, 128), 1)
    for s in range(sb):
        gt = gt_ref[s]
        li = gt + hp[0:1, :]
        x = gt + hp[1:2, :]
        lf = jnp.minimum(x, 0.0) - jnp.log(1.0 + jnp.exp(-jnp.abs(x)))
        lf = jnp.where(lane >= N_HEADS, jnp.where(lane < 2 * N_HEADS, lf, 0.0), 0.0)
        if tv < c:
            valid = (_iota((tg, 128), 0) % c) < tv
            lf = jnp.where(valid, lf, 0.0)
            li = jnp.where(valid, li, NEG)
        li_scr[s] = li
        lf_scr[s] = lf

    row = _iota((c, c), 0)
    col = _iota((c, c), 1)
    ige = row >= col
    tri = jnp.where(ige, 1.0, 0.0).astype(BF16)
    ones_b = jnp.ones((c, 128), BF16)
    lane_c = _iota((c, 128), 1)
    ones_blk = jnp.ones((c, 128), F32)
    nw = nw_ref[...]
    nrep = dv // 128 + 1

    def wide(x):
        return jnp.concatenate([x] * nrep, axis=1)

    def prep(clist):
        items = [(ci, h) for ci in range(len(clist)) for h in range(N_HEADS)]
        bcums = [_exact_dot(tri, lf_scr[s, pl.ds(r0, c), :]) for s, r0 in clist]
        lichs = [li_scr[s, pl.ds(r0, c), :] for s, r0 in clist]
        qbs = [(q_ref[clist[ci][0], pl.ds(clist[ci][1], c), h * dqk:(h + 1) * dqk] * (dqk ** -0.5)).astype(BF16)
               for ci, h in items]
        ks = [k_ref[clist[ci][0], pl.ds(clist[ci][1], c), h * dqk:(h + 1) * dqk] for ci, h in items]
        qk_raws = [_dot_nt(qb, k.astype(BF16)) for qb, k in zip(qbs, ks)]
        bcols = [jnp.broadcast_to(bcums[ci][:, N_HEADS + h:N_HEADS + h + 1], (c, 128)) for ci, h in items]
        licols = [jnp.broadcast_to(lichs[ci][:, h:h + 1], (c, 128)) for ci, h in items]
        rowvs = [_exact_dot_nt(ones_b, jnp.where(lane_c == 0, licol - bcol, 0.0))
                 for licol, bcol in zip(licols, bcols)]
        dmats = [jnp.where(ige, bcol[:, :c] + rowv, NEG) for bcol, rowv in zip(bcols, rowvs)]
        dmaxs = [jnp.broadcast_to(jnp.max(dmat, axis=-1, keepdims=True), (c, 128)) for dmat in dmats]
        vexts = [jnp.concatenate([v_ref[clist[ci][0], pl.ds(clist[ci][1], c), h * dv:(h + 1) * dv], ones_blk],
                                 axis=1).astype(BF16) for ci, h in items]
        intras = [_dot((qk_raw * jnp.exp(dmat - dmax[:, :c])).astype(BF16), vext)
                  for qk_raw, dmat, dmax, vext in zip(qk_raws, dmats, dmaxs, vexts)]
        blasts = [bcol[c - 1:c, :] for bcol in bcols]
        mus = [dmax[c - 1:c, :] for dmax in dmaxs]
        kvs = [_dot_tn((k * jnp.exp(blast - bcol + licol - mu)).astype(BF16), vext)
               for k, blast, bcol, licol, mu, vext in zip(ks, blasts, bcols, licols, mus, vexts)]
        out = [(qbs[i], bcols[i], dmaxs[i], intras[i], blasts[i], mus[i], kvs[i]) for i in range(len(items))]
        return [out[ci * N_HEADS:(ci + 1) * N_HEADS] for ci in range(len(clist))]

    def chunks(clist):
        hs = range(N_HEADS)
        for (s, r0), heads in zip(clist, prep(clist)):
            c_olds = [cext[s * N_HEADS + h] for h in hs]
            qcs = [_dot(heads[h][0], c_olds[h].astype(BF16)) for h in hs]
            for h in hs:
                qb, bcol, dmax, intra, blast, mu, kv = heads[h]
                mprev = m_scr[s * 8 + h:s * 8 + h + 1, :]
                inter = bcol + mprev
                mt = jnp.maximum(inter, dmax)
                mnew = jnp.maximum(blast + mprev, mu)
                cext[s * N_HEADS + h] = (c_olds[h] * wide(jnp.exp(blast + mprev - mnew))
                                         + wide(jnp.exp(mu - mnew)) * kv)
                m_scr[s * 8 + h:s * 8 + h + 1, :] = mnew
                tot = wide(jnp.exp(inter - mt)) * qcs[h] + wide(jnp.exp(dmax - mt)) * intra
                den = jnp.maximum(jnp.abs(tot[:, dv:dv + 128]), jnp.exp(-mt))
                hh = tot[:, :dv] / jnp.concatenate([den] * (dv // 128), axis=1)
                og = og_ref[s, pl.ds(r0, c), h * dv:(h + 1) * dv]
                zz = z_ref[s, pl.ds(r0, c), h * dv:(h + 1) * dv]
                o_ref[s, pl.ds(r0, c), h * dv:(h + 1) * dv] = _rms(_sigmoid(og) * hh, nw) * _silu(zz)

    chunks([(s, i * c) for i in range(nch) for s in range(sb)])

    @pl.when(gi == pl.num_programs(1) - 1)
    def _():
        for l in range(co_ref.shape[0]):
            if l != lay:
                co_ref[l] = jnp.zeros(co_ref.shape[1:], F32)
        for s in range(sb):
            for h in range(N_HEADS):
                co_ref[lay, s, h] = cext[s * N_HEADS + h, :, 0:dv]
                ncol = jnp.broadcast_to(cext[s * N_HEADS + h, :, dv:dv + 1], (dqk, dqk))
                no_ref[s, h:h + 1, :] = jnp.sum(jnp.where(eye, ncol, 0.0), axis=0, keepdims=True)
            mrows = m_scr[s * 8:(s + 1) * 8, :]
            diag = jnp.where(_iota((8, 128), 0) == _iota((8, 128), 1), mrows, 0.0)
            mo_ref[s] = jnp.broadcast_to(jnp.sum(diag, axis=0, keepdims=True), (8, 128))


def _mlstm(q, k, v, z, og, gates, c0, c_idx, n0, m0, hp, nw, c_stack, out_idx, *, bn, c, nch, sb, tv):
    n = q.shape[0]
    dqk, dv = c0.shape[-2], c0.shape[-1]
    tg = c * nch
    rows = n // bn
    steps = rows // tg
    hq, hv = N_HEADS * dqk, N_HEADS * dv
    row = lambda b, g: (b, g, 0)
    q, k, v, z, og, gates = (a.reshape(bn, rows, a.shape[1]) for a in (q, k, v, z, og, gates))
    create = isinstance(c_stack, int)
    held, first, lay = (c_stack, 0, out_idx) if create else (1, out_idx, 0)
    stack_shape = (c_stack, bn, N_HEADS, dqk, dv) if create else c_stack.shape
    kern = functools.partial(_mlstm_kernel, c=c, nch=nch, sb=sb, tv=tv, dqk=dqk, dv=dv, lay=lay)
    out = pl.pallas_call(
        kern, grid=(bn // sb, steps),
        in_specs=[pl.BlockSpec((sb, tg, hq), row), pl.BlockSpec((sb, tg, hq), row),
                  pl.BlockSpec((sb, tg, hv), row), pl.BlockSpec((sb, tg, hv), row),
                  pl.BlockSpec((sb, tg, hv), row), pl.BlockSpec((sb, tg, 128), row),
                  pl.BlockSpec((None, sb, N_HEADS, dqk, dv), lambda b, g: (c_idx, b, 0, 0, 0)),
                  pl.BlockSpec((sb, N_HEADS, dqk), lambda b, g: (b, 0, 0)),
                  pl.BlockSpec((sb, 8, 128), lambda b, g: (b, 0, 0)),
                  pl.BlockSpec((8, 128), lambda b, g: (0, 0)),
                  pl.BlockSpec((1, dv), lambda b, g: (0, 0))] + (
                      [] if create else [pl.BlockSpec(memory_space=pl.ANY)]),
        out_specs=[pl.BlockSpec((sb, tg, hv), row),
                   pl.BlockSpec((held, sb, N_HEADS, dqk, dv), lambda b, g: (first, b, 0, 0, 0)),
                   pl.BlockSpec((sb, N_HEADS, dqk), lambda b, g: (b, 0, 0)),
                   pl.BlockSpec((sb, 8, 128), lambda b, g: (b, 0, 0))],
        out_shape=[jax.ShapeDtypeStruct((bn, rows, hv), F32),
                   jax.ShapeDtypeStruct(stack_shape, F32),
                   jax.ShapeDtypeStruct((bn, N_HEADS, dqk), F32),
                   jax.ShapeDtypeStruct((bn, 8, 128), F32)],
        input_output_aliases={} if create else {11: 1},
        scratch_shapes=[pltpu.VMEM((sb, tg, 128), F32), pltpu.VMEM((sb, tg, 128), F32),
                        pltpu.VMEM((sb * N_HEADS, dqk, dv + 128), F32), pltpu.VMEM((sb * 8, 128), F32)],
        compiler_params=_cparams(("parallel", "arbitrary")), name="mlstm")(
            q, k, v, z, og, gates, c0, n0, m0, hp, nw.reshape(1, dv), *(() if create else (c_stack,)))
    return (out[0].reshape(n, hv),) + tuple(out[1:])


def _lambda(lq, lam_init):
    a = jnp.sum(lq[0:1, :] * lq[1:2, :], axis=-1, keepdims=True)
    b = jnp.sum(lq[2:3, :] * lq[3:4, :], axis=-1, keepdims=True)
    return jnp.exp(a) - jnp.exp(b) + lam_init


def _attn_kernel(wq_ref, k_ref, vt_ref, z_ref, lq_ref, nw_ref, o_ref, acc_scr, *, t, hpb, lam_init):
    qi = pl.program_id(2)
    dh = k_ref.shape[1] // hpb
    nl = 4 * t
    acc_scr[...] = jnp.zeros(acc_scr.shape, F32)

    def step(jp, carry, masked):
        rows = [pl.multiple_of((2 * jp + i) * t, t) for i in range(2)]
        wqs = [jnp.concatenate([wq_ref[0, a], wq_ref[1, a]], axis=1) for a in range(hpb)]
        scores = [[_dot(k_ref[pl.ds(rows[i], t), a * dh:(a + 1) * dh], wqs[a]) for a in range(hpb)]
                  for i in range(2)]
        out = list(carry)
        for i in range(2):
            for a in range(hpb):
                m_prev, l_prev = out[a]
                s = scores[i][a]
                if masked:
                    lane = _iota((t, nl), 1)
                    qpos = (lane // (2 * t)) * t + lane % t
                    s = jnp.where(_iota((t, nl), 0) + i * t <= qpos, s, NEG)
                m_new = jnp.maximum(m_prev, jnp.max(s, axis=0, keepdims=True))
                alpha = jnp.exp2(m_prev - m_new)
                p = jnp.exp2(s - m_new)
                l_new = alpha * l_prev + jnp.sum(p, axis=0, keepdims=True)
                acc_scr[a] = alpha * acc_scr[a] + _dot(vt_ref[2 * jp + i, a * dh:(a + 1) * dh, :], p.astype(BF16))
                out[a] = (m_new, l_new)
        return tuple(out)

    init = tuple((jnp.full((1, nl), NEG, F32), jnp.zeros((1, nl), F32)) for _ in range(hpb))
    carry = lax.fori_loop(0, qi, lambda j, c: step(j, c, False), init)
    carry = step(qi, carry, True)

    lam = _lambda(lq_ref[...], lam_init)
    nw = nw_ref[...]
    for a in range(hpb):
        acc = acc_scr[a] * (1.0 / carry[a][1])
        for qb in range(2):
            b0 = 2 * t * qb
            o = (acc[:, b0:b0 + t] - lam * acc[:, b0 + t:b0 + 2 * t]).T
            zz = z_ref[qb * t:(qb + 1) * t, a * dh:(a + 1) * dh]
            o_ref[qb * t:(qb + 1) * t, a * dh:(a + 1) * dh] = _rms(o, nw) * (1.0 - lam_init) * _silu(zz)


def _attn_prompt(wq, k16, vt, z, lq, nw, *, bn, seq, lam_init):
    n = k16.shape[0]
    t = vt.shape[2]
    dh = wq.shape[2]
    hpb = 4
    nq = seq // t
    nq2 = nq // 2
    kern = functools.partial(_attn_kernel, t=t, hpb=hpb, lam_init=lam_init)
    return pl.pallas_call(
        kern, grid=(bn, N_HEADS // hpb, nq2),
        in_specs=[pl.BlockSpec((2, hpb, dh, 2 * t), lambda b, h, i: (b * nq2 + i, h, 0, 0)),
                  pl.BlockSpec((seq, hpb * dh), lambda b, h, i: (b, h), pipeline_mode=pl.Buffered(1)),
                  pl.BlockSpec((nq, hpb * dh, t), lambda b, h, i: (b, h, 0), pipeline_mode=pl.Buffered(1)),
                  pl.BlockSpec((2 * t, hpb * dh), lambda b, h, i: (b * nq2 + i, h)),
                  pl.BlockSpec(lq.shape, lambda b, h, i: (0, 0)),
                  pl.BlockSpec((1, dh), lambda b, h, i: (0, 0))],
        out_specs=pl.BlockSpec((2 * t, hpb * dh), lambda b, h, i: (b * nq2 + i, h)),
        out_shape=jax.ShapeDtypeStruct((n, N_HEADS * dh), F32),
        scratch_shapes=[pltpu.VMEM((hpb, dh, 4 * t), F32)],
        compiler_params=_cparams(("parallel", "parallel", "arbitrary")), name="diff_attn_prompt")(
            wq, k16, vt, z, lq, nw.reshape(1, dh))


def _attn_s_kernel(pt_ref, q_ref, kn_ref, vn_ref, z_ref, lq_ref, nw_ref, *rest, pg, tv, lam_init):
    k_pages = rest[:pg]
    v_pages = rest[pg:2 * pg]
    o_ref, qbd, bias, m_scr, l_scr, acc_scr = rest[2 * pg:]
    j = pl.program_id(1)
    tp, hd = q_ref.shape
    dh = hd // N_HEADS
    dqk = dh // 2
    prow = k_pages[0].shape[0]
    rph = 2 * tp

    @pl.when(j == 0)
    def _():
        q = q_ref[...] * (dqk ** -0.5)
        lane = _iota((tp, dh), 1)
        for h in range(N_HEADS):
            qh = q[:, h * dh:(h + 1) * dh]
            qbd[h * rph:h * rph + tp, :] = jnp.where(lane < dqk, qh, 0.0)
            qbd[h * rph + tp:(h + 1) * rph, :] = jnp.where(lane >= dqk, qh, 0.0)
        rhead = _iota(bias.shape, 0) // rph
        chead = _iota(bias.shape, 1) % N_HEADS
        bias[...] = jnp.where(rhead == chead, 0.0, NEG)
        m_scr[...] = jnp.full(m_scr.shape, NEG, F32)
        l_scr[...] = jnp.zeros(l_scr.shape, F32)
        acc_scr[...] = jnp.zeros(acc_scr.shape, F32)

    def update(s, pv):
        m_prev = m_scr[...]
        m_new = jnp.maximum(m_prev, jnp.max(s, axis=-1, keepdims=True))
        alpha = jnp.exp(m_prev - m_new)
        p = jnp.exp(s - m_new[:, 0:1])
        l_scr[...] = alpha * l_scr[...] + jnp.sum(p, axis=-1, keepdims=True)
        acc_scr[...] = alpha * acc_scr[...] + pv(p)
        m_scr[...] = m_new

    qb = qbd[...].astype(BF16)
    bs_ = bias[...]
    s = jnp.concatenate([_dot_nt(qb, kp[...].astype(BF16)) + bs_ for kp in k_pages], axis=1)

    def pv_pages(p):
        pb = p.astype(BF16)
        acc = _dot(pb[:, 0:prow], v_pages[0][...].astype(BF16))
        for i in range(1, pg):
            acc = acc + _dot(pb[:, i * prow:(i + 1) * prow], v_pages[i][...].astype(BF16))
        return acc

    update(s, pv_pages)

    @pl.when(j == pl.num_programs(1) - 1)
    def _():
        qf = qbd[...]
        kn = kn_ref[...]
        vn = vn_ref[...]
        sn = jnp.concatenate([_dot_nt(qf[h * rph:(h + 1) * rph], kn[:, h * dh:(h + 1) * dh])
                              for h in range(N_HEADS)], axis=0)
        tq = _iota(sn.shape, 0) % tp
        tk = _iota(sn.shape, 1)
        sn = jnp.where(tk <= tq, jnp.where(tk < tv, sn, NEG), NEG)
        update(sn, lambda p: jnp.concatenate(
            [_dot(p[h * rph:(h + 1) * rph], vn[:, h * dh:(h + 1) * dh]) for h in range(N_HEADS)], axis=0))
        acc = acc_scr[...] / l_scr[...]
        lam = _lambda(lq_ref[...], lam_init)
        nw = nw_ref[...]
        normed = jnp.concatenate(
            [_rms(acc[h * rph:h * rph + tp] - lam * acc[h * rph + tp:(h + 1) * rph], nw) for h in range(N_HEADS)],
            axis=1)
        o_ref[...] = normed * (1.0 - lam_init) * _silu(z_ref[...])


def _attn_sample(page_table, q, kn, vn, z, lq, nw, cache_k, cache_v, *, e, tv, lam_init):
    bs, npages = page_table.shape
    n, hd = q.shape
    tp = n // bs
    dh = hd // N_HEADS
    pg = math.gcd(npages, PAGES_PER_STEP)
    prow = cache_k.shape[2]
    steps = npages // pg
    rows = 2 * N_HEADS * tp
    tok = lambda b, j, pt: (b, 0)
    page_specs = [pl.BlockSpec((None, None, prow, dh),
                               lambda b, j, pt, i=i: (e, pt[b, j * pg + i], 0, 0)) for i in range(pg)]
    kern = functools.partial(_attn_s_kernel, pg=pg, tv=tv, lam_init=lam_init)
    grid_spec = pltpu.PrefetchScalarGridSpec(
        num_scalar_prefetch=1, grid=(bs, steps),
        in_specs=[pl.BlockSpec((tp, hd), tok), pl.BlockSpec((tp, hd), tok), pl.BlockSpec((tp, hd), tok),
                  pl.BlockSpec((tp, hd), tok),
                  pl.BlockSpec(lq.shape, lambda b, j, pt: (0, 0)),
                  pl.BlockSpec((1, dh), lambda b, j, pt: (0, 0))] + page_specs + page_specs,
        out_specs=pl.BlockSpec((tp, hd), tok),
        scratch_shapes=[pltpu.VMEM((rows, dh), F32), pltpu.VMEM((rows, prow), F32), pltpu.VMEM((rows, 128), F32),
                        pltpu.VMEM((rows, 128), F32), pltpu.VMEM((rows, dh), F32)])
    return pl.pallas_call(
        kern, grid_spec=grid_spec, out_shape=jax.ShapeDtypeStruct((n, hd), F32),
        compiler_params=_cparams(("parallel", "arbitrary")), name="diff_attn_sample")(
            page_table, q, kn, vn, z, lq, nw.reshape(1, dh), *([cache_k] * pg), *([cache_v] * pg))


def _rope_tables(pos, dqk):
    rope_dim = dqk // 4
    half = rope_dim // 2
    inv = jnp.power(ROPE_THETA, -jnp.arange(half, dtype=F32) / half)
    ang = pos.astype(F32)[:, None] * inv[None, :]
    lane = jnp.arange(128) % dqk
    idx = lane % half
    cos = jnp.where(lane < rope_dim, jnp.cos(ang)[:, idx], 1.0)
    sin = jnp.sin(ang)[:, idx]
    sa = jnp.where((lane >= half) & (lane < rope_dim), sin, 0.0)
    sb = jnp.where(lane < half, -sin, 0.0)
    return cos.astype(F32), sa.astype(F32), sb.astype(F32)


def _lane_row(vals, offset):
    out = jnp.zeros((128,), F32)
    return out.at[offset:offset + vals.shape[0]].set(vals.astype(F32))


def kernel(x_prompt, x_sample, cache_k, cache_v, page_table, state_gdn_conv, state_gdn_s, state_mlstm_c,
           state_mlstm_n, state_mlstm_m, norm_w, final_norm_w, w_in_even, w_out_even, conv_w, a_log, dt_bias,
           gdn_norm_w, lambda_qk, diff_norm_w, w_in_odd, w_out_odd, b_i, b_f, mlstm_norm_w):
    bp, seq, d = x_prompt.shape
    bs, ts, _ = x_sample.shape
    depth = norm_w.shape[0]
    tp = SAMPLE_PAD
    dk_a = state_gdn_s.shape[-1]
    conv_ch = state_gdn_conv.shape[-1]
    w_a = N_HEADS * dk_a
    dh_b = cache_v.shape[-1]
    dqk_b = dh_b // 2
    w_b = N_HEADS * dh_b
    dqk_c, dv_c = state_mlstm_c.shape[-2], state_mlstm_c.shape[-1]
    hq_c, w_c = N_HEADS * dqk_c, N_HEADS * dv_c
    past = page_table.shape[1] * cache_k.shape[2]
    n_pool, page = cache_k.shape[1], cache_k.shape[2]

    xp = x_prompt.reshape(bp * seq, d)
    xs = jnp.pad(x_sample, ((0, 0), (0, tp - ts), (0, 0))).reshape(bs * tp, d)
    tm_p = math.gcd(seq, 256)
    tm_s = math.gcd(bs * tp, 256)
    nch_p = math.gcd(seq // CHUNK, 8)
    seq_s = math.gcd(bs, 8)
    seq_c = math.gcd(bs, 8)

    rope_p = _rope_tables(jnp.arange(seq), dqk_b)
    rope_s = _rope_tables(past + (jnp.arange(tm_s) % tp), dqk_b)
    half = dqk_b // 8
    ang_t = (jnp.power(ROPE_THETA, -jnp.arange(half, dtype=F32) / half)[:, None]
             * jnp.arange(seq).astype(F32)[None, :])
    cos_t, sin_t = jnp.cos(ang_t), jnp.sin(ang_t)

    even_w = conv_ch + w_a
    seg_even = [(0, even_w), (even_w, w_b), (even_w + w_b, w_b), (even_w + 2 * w_b, w_b),
                (even_w + 3 * w_b, w_b), (even_w + 4 * w_b, 128)]
    seg_even_p = [seg_even[0]] + seg_even[2:]
    rope_even = tuple(range(even_w // 128, (even_w + 2 * w_b) // 128))
    odd_w = 2 * hq_c + 3 * w_c
    seg_odd = [(0, hq_c), (hq_c, hq_c), (2 * hq_c, w_c), (2 * hq_c + w_c, w_c), (2 * hq_c + 2 * w_c, w_c),
               (odd_w, 128)]

    ck = cache_k.reshape(cache_k.shape[0], n_pool, page * N_HEADS, dh_b)
    cv = cache_v.reshape(cache_v.shape[0], n_pool, page * N_HEADS, dh_b)

    n_even = (depth + 1) // 2
    krows_p = vrows_p = n_even
    k_rows_s, v_rows_s = [], []
    n_odd = depth // 2
    s_p = s_s = n_even
    c_p = c_s = n_odd
    conv_p, conv_s = [], []
    n_p, m_p, n_s, m_s = [], [], [], []

    pend_p = pend_s = None
    for layer in range(depth):
        if layer % 2 == 0:
            e = layer // 2
            lam_init = 0.8 - 0.6 * math.exp(-0.3 * layer)
            w = w_in_even[e]
            g0 = even_w
            w_r = jnp.concatenate([w[:, :g0], w[:, g0 + 2 * N_HEADS:], w[:, g0:g0 + 2 * N_HEADS],
                                   jnp.zeros((d, 128 - 2 * N_HEADS), F32)], axis=1).astype(BF16)
            w_o = w_out_even[e].astype(BF16)
            hp = jnp.zeros((8, 128), F32).at[0].set(_lane_row(a_log[e], N_HEADS)).at[1].set(
                _lane_row(dt_bias[e], N_HEADS))
            lq = lambda_qk[e]

            wq_t = w[:, g0 + 2 * N_HEADS:g0 + 2 * N_HEADS + w_b].T.astype(BF16)
            wv_t = w[:, g0 + 2 * N_HEADS + 2 * w_b:g0 + 2 * N_HEADS + 3 * w_b].T.astype(BF16)
            outs = _proj(xp, norm_w[layer], w_r, seg_even_p, (rope_even, rope_p), tm_p, k16_seg=1,
                         feat=(wq_t, wv_t, cos_t, sin_t), rows=((1, 2), e, (krows_p, vrows_p)), pre=pend_p)
            if pend_p is not None:
                xp = outs.pop()
            az, krows_p, vrows_p, zb, gt, k16, wq, vt = outs
            oa, cst, s_p = _gdn(az, gt, jnp.zeros((bp, 8, conv_ch), F32),
                                jnp.zeros((1, bp, N_HEADS, dk_a, dk_a), F32), 0,
                                conv_w[e], hp, gdn_norm_w[e], s_p, e, bn=bp, c=CHUNK, nch=nch_p, tv=CHUNK)
            ob = _attn_prompt(wq, k16, vt, zb, lq, diff_norm_w[e], bn=bp, seq=seq, lam_init=lam_init)
            pend_p = (oa, ob, w_a, 0, w_o)
            conv_p.append(cst[:, 8 - (CONV_W - 1):])

            outs = _proj(xs, norm_w[layer], w_r, seg_even, (rope_even, rope_s), tm_s, pre=pend_s)
            if pend_s is not None:
                xs = outs.pop()
            az, qb, kb, vb, zb, gt = outs
            conv0 = jnp.pad(state_gdn_conv[e], ((0, 0), (8 - (CONV_W - 1), 0), (0, 0)))
            oa, cst, s_s = _gdn(az, gt, conv0, state_gdn_s, e, conv_w[e], hp, gdn_norm_w[e], s_s, e,
                                bn=bs, c=tp, nch=seq_s, tv=ts, per_seq=True)
            ob = _attn_sample(page_table, qb, kb, vb, zb, lq, diff_norm_w[e], ck, cv, e=e, tv=ts,
                              lam_init=lam_init)
            pend_s = (oa, ob, w_a, 0, w_o)
            k_rows_s.append(kb.reshape(bs, tp, N_HEADS, dh_b)[:, :ts])
            v_rows_s.append(vb.reshape(bs, tp, N_HEADS, dh_b)[:, :ts])
            conv_s.append(cst[:, 8 - (CONV_W - 1):])
        else:
            o = layer // 2
            w_r = jnp.concatenate([w_in_odd[o], jnp.zeros((d, 128 - 2 * N_HEADS), F32)], axis=1).astype(BF16)
            w_o = w_out_odd[o].astype(BF16)
            hp = jnp.zeros((8, 128), F32).at[0].set(_lane_row(b_i[o], 0)).at[1].set(_lane_row(b_f[o], N_HEADS))

            outs = _proj(xp, norm_w[layer], w_r, seg_odd, None, tm_p, pre=pend_p)
            if pend_p is not None:
                xp = outs.pop()
            q, k, v, z, og, gt = outs
            hm, c_p, nf, mf = _mlstm(q, k, v, z, og, gt, jnp.zeros((1, bp, N_HEADS, dqk_c, dv_c), F32), 0,
                                     jnp.zeros((bp, N_HEADS, dqk_c), F32), jnp.zeros((bp, 8, 128), F32),
                                     hp, mlstm_norm_w[o], c_p, o, bn=bp, c=CHUNK, nch=math.gcd(seq // CHUNK, 8),
                                     sb=1, tv=CHUNK)
            pend_p = (hm, hm, w_c // 2, 1, w_o)
            n_p.append(nf)
            m_p.append(mf[:, 0, :N_HEADS])

            outs = _proj(xs, norm_w[layer], w_r, seg_odd, None, tm_s, pre=pend_s)
            if pend_s is not None:
                xs = outs.pop()
            q, k, v, z, og, gt = outs
            m0 = jnp.zeros((bs, 8, 128), F32).at[:, 0, :N_HEADS].set(state_mlstm_m[o])
            hm, c_s, nf, mf = _mlstm(q, k, v, z, og, gt, state_mlstm_c, o, state_mlstm_n[o], m0,
                                     hp, mlstm_norm_w[o], c_s, o, bn=bs, c=tp, nch=1, sb=seq_c, tv=ts)
            pend_s = (hm, hm, w_c // 2, 1, w_o)
            n_s.append(nf)
            m_s.append(mf[:, 0, :N_HEADS])

    xp = _out_proj(*pend_p, xp, final_norm_w, True, tm_p)
    xs = _out_proj(*pend_s, xs, final_norm_w, True, tm_s)
    y_prompt = xp.reshape(bp, seq, d)
    y_sample = xs.reshape(bs, tp, d)[:, :ts]
    st = jnp.stack
    kv_shape = (n_even, bp, seq, N_HEADS, dh_b)
    return (y_prompt, y_sample, krows_p.reshape(kv_shape), vrows_p.reshape(kv_shape), st(k_rows_s), st(v_rows_s),
            st(conv_p), s_p, st(conv_s), s_s,
            c_p, st(n_p), st(m_p), c_s, st(n_s), st(m_s))
```
